```python
import jax, jax.numpy as jnp
from jax import lax
import numpy as np

D_MODEL = 2048
BATCH = 4
SEQ = 8192
DEPTH = 1

GLA_HEADS = 4
GLA_DK = (D_MODEL // 2) // GLA_HEADS
GLA_DV = D_MODEL // GLA_HEADS
GLA_GATE_RANK = 16
GLA_GATE_NORMALIZER = 16.0
GLA_CHUNK = 64
MLA_HEADS = 16
MLA_Q_RANK = D_MODEL // 4
MLA_KV_RANK = 512
MLA_NOPE = 128
MLA_ROPE = 64
MLA_V = 128
ROPE_THETA = 10000.0
Q_BLOCK = 128
FFN_HIDDEN = ((8 * D_MODEL // 3 + 255) // 256) * 256
NORM_EPS = 1e-6
N_MOD = 6

IN_WIDTHS = (GLA_HEADS * GLA_DK, GLA_HEADS * GLA_DK, GLA_HEADS * GLA_DV, GLA_HEADS * GLA_DV,
             GLA_GATE_RANK, MLA_Q_RANK, MLA_KV_RANK, MLA_ROPE, D_MODEL, D_MODEL)
IN_TOTAL = sum(IN_WIDTHS)

kernel_name = "hybrid_gla_mla_adaln_block"


def rmsnorm(x, g):
    xf = x.astype(jnp.float32)
    y = xf * lax.rsqrt(jnp.mean(xf * xf, axis=-1, keepdims=True) + NORM_EPS)
    return (y * g.astype(jnp.float32)).astype(x.dtype)


def rope(t, cos, sin):
    t1, t2 = jnp.split(t, 2, axis=-1)
    return jnp.concatenate([t1 * cos - t2 * sin, t2 * cos + t1 * sin], axis=-1)


def gla_chunked(q, k, v, log_a):
    B, H, S, dk = q.shape
    dv = v.shape[-1]
    C = GLA_CHUNK
    N = S // C

    def to_chunks(t):
        return t.astype(jnp.float32).reshape(B, H, N, C, t.shape[-1]).transpose(2, 0, 1, 3, 4)

    qc, kc, vc = to_chunks(q), to_chunks(k), to_chunks(v)
    bc = jnp.cumsum(to_chunks(log_a), axis=3)
    causal = jnp.tril(jnp.ones((C, C), dtype=bool))

    def step(state, inp):
        q_i, k_i, v_i, b_i = inp
        o_inter = jnp.einsum('bhid,bhdv->bhiv', q_i * jnp.exp(b_i), state)
        diff = b_i[:, :, :, None, :] - b_i[:, :, None, :, :]
        w = jnp.exp(jnp.where(causal[:, :, None], diff, -jnp.inf))
        att = jnp.einsum('bhid,bhjd,bhijd->bhij', q_i, k_i, w)
        o_intra = jnp.einsum('bhij,bhjv->bhiv', att, v_i)
        b_last = b_i[:, :, -1, :]
        k_dec = k_i * jnp.exp(b_last[:, :, None, :] - b_i)
        state = jnp.exp(b_last)[..., None] * state + jnp.einsum('bhjd,bhjv->bhdv', k_dec, v_i)
        return state, o_inter + o_intra

    state0 = jnp.zeros((B, H, dk, dv), jnp.float32)
    _, o = lax.scan(step, state0, (qc, kc, vc, bc))
    return o.transpose(1, 2, 0, 3, 4).reshape(B, H, S, dv).astype(v.dtype)


def mla_attention(q_nope, q_rope, k_nope, k_rope, v):
    B, S, H, dn = q_nope.shape
    R = q_rope.shape[-1]
    dv = v.shape[-1]
    NB = S // Q_BLOCK
    scale = (dn + R) ** -0.5
    qn = q_nope.reshape(B, NB, Q_BLOCK, H, dn).transpose(1, 0, 3, 2, 4)
    qr = q_rope.reshape(B, NB, Q_BLOCK, H, R).transpose(1, 0, 3, 2, 4)
    kn = k_nope.transpose(0, 2, 1, 3)
    vv = v.transpose(0, 2, 1, 3)
    key_pos = jnp.arange(S)

    def block(args):
        qn_b, qr_b, i = args
        s = (jnp.einsum('bhqd,bhkd->bhqk', qn_b, kn)
             + jnp.einsum('bhqr,bkr->bhqk', qr_b, k_rope)).astype(jnp.float32) * scale
        q_pos = i * Q_BLOCK + jnp.arange(Q_BLOCK)
        mask = key_pos[None, :] <= q_pos[:, None]
        p = jax.nn.softmax(jnp.where(mask, s, -jnp.inf), axis=-1)
        return jnp.einsum('bhqk,bhkd->bhqd', p.astype(vv.dtype), vv)

    o = lax.map(block, (qn, qr, jnp.arange(NB)))
    return o.transpose(1, 0, 3, 2, 4).reshape(B, S, H, dv)


def token_mix(h, cos, sin, w_in, gla_gk_w, gla_gk_b, gla_onorm_g, gla_wo,
              mla_q_norm_g, mla_wuq, mla_kv_norm_g, mla_wukv, mla_wo, w_out):
    B, S, _ = h.shape
    idx = [sum(IN_WIDTHS[:i + 1]) for i in range(len(IN_WIDTHS) - 1)]
    (p_q, p_k, p_v, p_g, p_gk, p_cq, p_ckv, p_kr, p_ga, p_gb) = jnp.split(h @ w_in, idx, axis=-1)

    def heads(t, d):
        return t.reshape(B, S, GLA_HEADS, d).transpose(0, 2, 1, 3)
    q = heads(p_q, GLA_DK) * (GLA_DK ** -0.5)
    k = heads(p_k, GLA_DK)
    v = heads(p_v, GLA_DV)
    log_a = jax.nn.log_sigmoid((p_gk @ gla_gk_w + gla_gk_b).astype(jnp.float32)) / GLA_GATE_NORMALIZER
    o = gla_chunked(q, k, v, heads(log_a, GLA_DK)).transpose(0, 2, 1, 3)
    o = rmsnorm(o, gla_onorm_g) * jax.nn.silu(p_g.reshape(B, S, GLA_HEADS, GLA_DV))
    y_gla = o.reshape(B, S, GLA_HEADS * GLA_DV) @ gla_wo

    q_lat = (rmsnorm(p_cq, mla_q_norm_g) @ mla_wuq).reshape(B, S, MLA_HEADS, MLA_NOPE + MLA_ROPE)
    q_nope, q_rope = q_lat[..., :MLA_NOPE], q_lat[..., MLA_NOPE:]
    q_rope = rope(q_rope, cos[:, :, None, :], sin[:, :, None, :])
    kv = (rmsnorm(p_ckv, mla_kv_norm_g) @ mla_wukv).reshape(B, S, MLA_HEADS, MLA_NOPE + MLA_V)
    k_nope, v_m = kv[..., :MLA_NOPE], kv[..., MLA_NOPE:]
    k_rope = rope(p_kr, cos, sin)
    o_m = mla_attention(q_nope, q_rope, k_nope, k_rope, v_m)
    y_mla = o_m.reshape(B, S, MLA_HEADS * MLA_V) @ mla_wo

    merged = jax.nn.sigmoid(p_ga) * y_gla + jax.nn.sigmoid(p_gb) * y_mla
    return merged @ w_out


def swiglu(h, w_in, w_down):
    g, u = jnp.split(h @ w_in, 2, axis=-1)
    return (jax.nn.silu(g) * u) @ w_down


def setup_inputs(seed: int = 0) -> dict:
    key = jax.random.key(seed)
    ks = jax.random.split(key, 24)
    L, D = DEPTH, D_MODEL

    def nrm(k, shape, scale):
        return jax.random.normal(k, shape, jnp.float32) * scale

    def gain(k, n):
        return 1.0 + 0.02 * jax.random.normal(k, (L, n), jnp.float32)

    x = jax.random.normal(ks[0], (BATCH, SEQ, D), jnp.float32)
    c = jax.random.normal(ks[1], (BATCH, D), jnp.float32)
    offsets = jax.random.randint(ks[2], (BATCH, 1), 0, 1024, dtype=jnp.int32)
    positions = offsets + jnp.arange(SEQ, dtype=jnp.int32)[None, :]
    return {
        "x": x,
        "c": c,
        "positions": positions,
        "ada_w": nrm(ks[3], (L, D, N_MOD * D), 0.5 * D ** -0.5),
        "ada_b": nrm(ks[4], (L, N_MOD * D), 0.02),
        "norm_mix_g": gain(ks[5], D),
        "w_in": nrm(ks[6], (L, D, IN_TOTAL), D ** -0.5),
        "gla_gk_w": nrm(ks[7], (L, GLA_GATE_RANK, GLA_HEADS * GLA_DK), GLA_GATE_RANK ** -0.5),
        "gla_gk_b": nrm(ks[8], (L, GLA_HEADS * GLA_DK), 0.5),
        "gla_onorm_g": gain(ks[9], GLA_DV),
        "gla_wo": nrm(ks[10], (L, GLA_HEADS * GLA_DV, D), (GLA_HEADS * GLA_DV) ** -0.5),
        "mla_q_norm_g": gain(ks[11], MLA_Q_RANK),
        "mla_wuq": nrm(ks[12], (L, MLA_Q_RANK, MLA_HEADS * (MLA_NOPE + MLA_ROPE)), MLA_Q_RANK ** -0.5),
        "mla_kv_norm_g": gain(ks[13], MLA_KV_RANK),
        "mla_wukv": nrm(ks[14], (L, MLA_KV_RANK, MLA_HEADS * (MLA_NOPE + MLA_V)), MLA_KV_RANK ** -0.5),
        "mla_wo": nrm(ks[15], (L, MLA_HEADS * MLA_V, D), (MLA_HEADS * MLA_V) ** -0.5),
        "w_out": nrm(ks[16], (L, D, D), D ** -0.5),
        "norm_ffn_g": gain(ks[17], D),
        "ffn_w_in": nrm(ks[18], (L, D, 2 * FFN_HIDDEN), D ** -0.5),
        "ffn_w_down": nrm(ks[19], (L, FFN_HIDDEN, D), FFN_HIDDEN ** -0.5),
        "final_norm_g": 1.0 + 0.02 * jax.random.normal(ks[20], (D,), jnp.float32),
    }


def reference(x, c, positions, ada_w, ada_b, norm_mix_g, w_in, gla_gk_w, gla_gk_b, gla_onorm_g,
              gla_wo, mla_q_norm_g, mla_wuq, mla_kv_norm_g, mla_wukv, mla_wo, w_out,
              norm_ffn_g, ffn_w_in, ffn_w_down, final_norm_g):
    B, S, D = x.shape
    inv_freq = ROPE_THETA ** (-jnp.arange(0, MLA_ROPE, 2, dtype=jnp.float32) / MLA_ROPE)
    ang = positions.astype(jnp.float32)[..., None] * inv_freq
    cos = jnp.cos(ang).astype(x.dtype)
    sin = jnp.sin(ang).astype(x.dtype)
    c_act = jax.nn.silu(c)
    for l in range(DEPTH):
        mod = (c_act @ ada_w[l] + ada_b[l]).reshape(B, N_MOD, D)[:, :, None, :]
        shift_m, scale_m, gate_m, shift_f, scale_f, gate_f = [mod[:, i] for i in range(N_MOD)]
        h = rmsnorm(x, norm_mix_g[l]) * (1.0 + scale_m) + shift_m
        x = x + gate_m * token_mix(h, cos, sin, w_in[l], gla_gk_w[l], gla_gk_b[l], gla_onorm_g[l],
                                   gla_wo[l], mla_q_norm_g[l], mla_wuq[l], mla_kv_norm_g[l],
                                   mla_wukv[l], mla_wo[l], w_out[l])
        h = rmsnorm(x, norm_ffn_g[l]) * (1.0 + scale_f) + shift_f
        x = x + gate_f * swiglu(h, ffn_w_in[l], ffn_w_down[l])
    return rmsnorm(x, final_norm_g)
```

```python
import functools

import jax
import jax.numpy as jnp
from jax import lax
from jax.experimental import pallas as pl
from jax.experimental.pallas import tpu as pltpu

F32 = jnp.float32
BF16 = jnp.bfloat16

GLA_HEADS = 4
GLA_GATE_RANK = 16
GLA_GATE_NORMALIZER = 16.0
MLA_HEADS = 16
MLA_NOPE = 128
MLA_ROPE = 64
MLA_V = 128
ROPE_THETA = 10000.0
NORM_EPS = 1e-6
N_MOD = 6
MOD_SHIFT_MIX, MOD_SCALE_MIX, MOD_GATE_MIX, MOD_SHIFT_FFN, MOD_SCALE_FFN, MOD_GATE_FFN = range(N_MOD)

V7X_LANES = 128
V7X_VMEM_LIMIT_BYTES = 56 * 1024 * 1024

MASK_VALUE = -1e30


def _params(*semantics):
    return pltpu.CompilerParams(dimension_semantics=semantics, vmem_limit_bytes=V7X_VMEM_LIMIT_BYTES)


def _dot(a, b):
    return jnp.dot(a, b, preferred_element_type=F32)


def _dot_nt(a, b):
    return lax.dot_general(a, b, (((1,), (1,)), ((), ())), preferred_element_type=F32)


def _dot_tn(a, b):
    return lax.dot_general(a, b, (((0,), (0,)), ((), ())), preferred_element_type=F32)


def _rms(x):
    return x * lax.rsqrt(jnp.mean(x * x, axis=-1, keepdims=True) + NORM_EPS)


def _modnorm(x, gain, scale, shift):
    return _rms(x) * gain * (1.0 + scale) + shift


def _tile(n, want):
    t = min(n, want)
    assert n % t == 0, (n, want)
    return t


def _ada_kernel(c_ref, w_ref, b_ref, o_ref):
    c = c_ref[...]
    act = (c * jax.nn.sigmoid(c)).astype(BF16)
    o_ref[...] = _dot(act, w_ref[...].astype(BF16)) + b_ref[...]


def _ada(c, w, b):
    bsz, d = c.shape
    n = w.shape[1]
    tn = _tile(n, 1024)
    return pl.pallas_call(
        _ada_kernel,
        grid=(n // tn,),
        in_specs=[pl.BlockSpec((bsz, d), lambda j: (0, 0)),
                  pl.BlockSpec((d, tn), lambda j: (0, j)),
                  pl.BlockSpec((1, tn), lambda j: (0, j))],
        out_specs=pl.BlockSpec((bsz, tn), lambda j: (0, j)),
        out_shape=jax.ShapeDtypeStruct((bsz, n), F32),
        compiler_params=_params("arbitrary"),
        name="ada",
    )(c, w, b.reshape(1, n))


def _rope_tab_kernel(pos_ref, f_ref, cos_ref, sin_ref):
    ang = pos_ref[...].astype(F32) * f_ref[...]
    cos_ref[...] = jnp.cos(ang)
    sin_ref[...] = jnp.sin(ang)


def _rope_tab(positions):
    t = positions.size
    half = MLA_ROPE // 2
    per_row = V7X_LANES // half
    inv_freq = ROPE_THETA ** (-jnp.arange(0, MLA_ROPE, 2, dtype=F32) / MLA_ROPE)
    pos = jnp.repeat(positions.reshape(t), half).reshape(t // per_row, V7X_LANES)
    freq = jnp.tile(inv_freq, per_row).reshape(1, V7X_LANES)
    rows = t // per_row
    tr = _tile(rows, 1024)
    spec = pl.BlockSpec((tr, V7X_LANES), lambda i: (i, 0))
    cos, sin = pl.pallas_call(
        _rope_tab_kernel,
        grid=(rows // tr,),
        in_specs=[spec, pl.BlockSpec((1, V7X_LANES), lambda i: (0, 0))],
        out_specs=[spec, spec],
        out_shape=[jax.ShapeDtypeStruct((rows, V7X_LANES), F32)] * 2,
        compiler_params=_params("arbitrary"),
        name="rope_tab",
    )(pos, freq)
    return cos.reshape(t, half), sin.reshape(t, half)


def _inproj_kernel(x_ref, mod_ref, g_ref, w_ref, o_ref, h_ref):
    @pl.when(pl.program_id(1) == 0)
    def _():
        h = _modnorm(x_ref[...], g_ref[...],
                     mod_ref[0, MOD_SCALE_MIX:MOD_SCALE_MIX + 1, :], mod_ref[0, MOD_SHIFT_MIX:MOD_SHIFT_MIX + 1, :])
        h_ref[...] = h.astype(BF16)

    o_ref[...] = _dot(h_ref[...], w_ref[...]).astype(o_ref.dtype)


def _inproj(x2, mod3, gain, w, seq):
    t, d = x2.shape
    n = w.shape[1]
    tm = _tile(seq, 1024)
    tn = _tile(n, 1280)
    return pl.pallas_call(
        _inproj_kernel,
        grid=(t // tm, n // tn),
        in_specs=[pl.BlockSpec((tm, d), lambda i, j: (i, 0)),
                  pl.BlockSpec((1, N_MOD, d), lambda i, j: ((i * tm) // seq, 0, 0)),
                  pl.BlockSpec((1, d), lambda i, j: (0, 0)),
                  pl.BlockSpec((d, tn), lambda i, j: (0, j))],
        out_specs=pl.BlockSpec((tm, tn), lambda i, j: (i, j)),
        out_shape=jax.ShapeDtypeStruct((t, n), BF16),
        scratch_shapes=[pltpu.VMEM((tm, d), BF16)],
        compiler_params=_params("parallel", "arbitrary"),
        name="inproj",
    )(x2, mod3, gain.reshape(1, d), w)


def _mlaproj_kernel(lat_ref, cos_ref, sin_ref, gq_ref, gkv_ref, wuq_ref, wukv_ref, q_ref, k_ref, v_ref, *,
                    q_rank, kv_rank, scale):
    tm = lat_ref.shape[0]
    qw = MLA_NOPE + 2 * MLA_ROPE
    cos = cos_ref[...]
    sin = sin_ref[...]
    pad = jnp.zeros((tm, V7X_LANES - MLA_ROPE), F32)
    cs = jnp.concatenate([cos, cos, pad], axis=1)
    sn = jnp.concatenate([-sin, sin, pad], axis=1)

    def rope_tile(t):
        return t * cs + pltpu.roll(t, MLA_ROPE, axis=1) * sn

    cq = lat_ref[:, 0:q_rank].astype(F32)
    ckv = lat_ref[:, q_rank:q_rank + kv_rank].astype(F32)
    kr = lat_ref[:, q_rank + kv_rank:q_rank + kv_rank + V7X_LANES].astype(F32)
    k_rope = rope_tile(kr).astype(BF16)
    cqn = (_rms(cq) * gq_ref[...]).astype(BF16)
    ckvn = (_rms(ckv) * gkv_ref[...]).astype(BF16)
    for h in range(MLA_HEADS):
        ql = _dot(cqn, wuq_ref[:, h * qw:(h + 1) * qw])
        q_ref[:, h * qw:h * qw + MLA_NOPE] = (ql[:, :MLA_NOPE] * scale).astype(BF16)
        q_ref[:, h * qw + MLA_NOPE:(h + 1) * qw] = (rope_tile(ql[:, MLA_NOPE:]) * scale).astype(BF16)
        kv = _dot(ckvn, wukv_ref[:, h * (MLA_NOPE + MLA_V):(h + 1) * (MLA_NOPE + MLA_V)])
        k_ref[:, h * qw:h * qw + MLA_NOPE] = kv[:, :MLA_NOPE].astype(BF16)
        k_ref[:, h * qw + MLA_NOPE:(h + 1) * qw] = k_rope
        v_ref[:, h * MLA_V:(h + 1) * MLA_V] = kv[:, MLA_NOPE:].astype(BF16)


def _mlaproj(p, lat_block, lat_width, cos, sin, gq, gkv, wuq, wukv):
    t = p.shape[0]
    q_rank, kv_rank = gq.shape[0], gkv.shape[0]
    qw = MLA_NOPE + 2 * MLA_ROPE
    tm = _tile(t, 512)
    half = MLA_ROPE // 2
    kern = functools.partial(_mlaproj_kernel, q_rank=q_rank, kv_rank=kv_rank, scale=(MLA_NOPE + MLA_ROPE) ** -0.5)
    return pl.pallas_call(
        kern,
        grid=(t // tm,),
        in_specs=[pl.BlockSpec((tm, lat_width), lambda i: (i, lat_block)),
                  pl.BlockSpec((tm, half), lambda i: (i, 0)),
                  pl.BlockSpec((tm, half), lambda i: (i, 0)),
                  pl.BlockSpec((1, q_rank), lambda i: (0, 0)),
                  pl.BlockSpec((1, kv_rank), lambda i: (0, 0)),
                  pl.BlockSpec(wuq.shape, lambda i: (0, 0)),
                  pl.BlockSpec(wukv.shape, lambda i: (0, 0))],
        out_specs=[pl.BlockSpec((tm, MLA_HEADS * qw), lambda i: (i, 0)),
                   pl.BlockSpec((tm, MLA_HEADS * qw), lambda i: (i, 0)),
                   pl.BlockSpec((tm, MLA_HEADS * MLA_V), lambda i: (i, 0))],
        out_shape=[jax.ShapeDtypeStruct((t, MLA_HEADS * qw), BF16),
                   jax.ShapeDtypeStruct((t, MLA_HEADS * qw), BF16),
                   jax.ShapeDtypeStruct((t, MLA_HEADS * MLA_V), BF16)],
        compiler_params=_params("parallel"),
        name="mlaproj",
    )(p, cos, sin, gq.reshape(1, q_rank), gkv.reshape(1, kv_rank), wuq, wukv)


def _gla_kernel(q_ref, k_ref, v_ref, g_ref, gk_ref, gkw_ref, gkb_ref, on_ref, o_ref, st_ref, *, dk, dv):
    c = q_ref.shape[0]

    @pl.when(pl.program_id(1) == 0)
    def _():
        st_ref[...] = jnp.zeros_like(st_ref)

    z = _dot(gk_ref[...], gkw_ref[...]) + gkb_ref[...]
    log_a = jax.nn.log_sigmoid(z) * (1.0 / GLA_GATE_NORMALIZER)

    halves = [1 << e for e in range(c.bit_length() - 1)]
    row = lax.broadcasted_iota(jnp.int32, (c, dk), 0)
    ai = lax.broadcasted_iota(jnp.int32, (c, c), 0)
    aj = lax.broadcasted_iota(jnp.int32, (c, c), 1)
    diff = ai ^ aj
    causal_diff = jnp.where(ai > aj, diff, 0)
    level_mask = {s: (causal_diff >> (s.bit_length() - 1)) == 1 for s in halves}
    diag_mask = ai == aj

    for h in range(GLA_HEADS):
        g = log_a[:, h * dk:(h + 1) * dk]
        prefix = g
        total = g
        decay = {}
        for s in halves:
            odd = (row & s) != 0
            sibling = jnp.where(odd, pltpu.roll(total, s, axis=0), pltpu.roll(total, c - s, axis=0))
            decay[s] = jnp.exp(jnp.where(odd, prefix, total - prefix))
            prefix = prefix + jnp.where(odd, sibling, 0.0)
            total = total + sibling
        qh = q_ref[:, h * dk:(h + 1) * dk].astype(F32) * (dk ** -0.5)
        kh = k_ref[:, h * dk:(h + 1) * dk].astype(F32)
        vh = v_ref[:, h * dv:(h + 1) * dv]
        att = jnp.where(diag_mask, _dot_nt(qh.astype(BF16), kh.astype(BF16)), 0.0)
        for s in halves:
            a = _dot_nt((qh * decay[s]).astype(BF16), (kh * decay[s]).astype(BF16))
            att = jnp.where(level_mask[s], a, att)
        st = st_ref[h]
        o = _dot_nt((qh * jnp.exp(prefix)).astype(BF16), st.astype(BF16)) + _dot(att.astype(BF16), vh)
        k_dec = (kh * jnp.exp(total - prefix)).astype(BF16)
        st_ref[h] = st * jnp.exp(total[0:1, :]) + _dot_tn(vh, k_dec)
        gate = g_ref[:, h * dv:(h + 1) * dv].astype(F32)
        o_ref[:, h * dv:(h + 1) * dv] = (_rms(o) * on_ref[...] * (gate * jax.nn.sigmoid(gate))).astype(o_ref.dtype)


def _gla(p, gk_block, gkw, gkb, onorm_g, bsz, seq, dk, dv):
    t = p.shape[0]
    c = _tile(seq, 128)
    nc = seq // c
    hk, hv = GLA_HEADS * dk, GLA_HEADS * dv
    kern = functools.partial(_gla_kernel, dk=dk, dv=dv)
    row = lambda b, i: b * nc + i
    return pl.pallas_call(
        kern,
        grid=(bsz, nc),
        in_specs=[pl.BlockSpec((c, hk), lambda b, i: (row(b, i), 0)),
                  pl.BlockSpec((c, hk), lambda b, i: (row(b, i), 1)),
                  pl.BlockSpec((c, hv), lambda b, i: (row(b, i), (2 * hk) // hv)),
                  pl.BlockSpec((c, hv), lambda b, i: (row(b, i), (2 * hk) // hv + 1)),
                  pl.BlockSpec((c, V7X_LANES), lambda b, i: (row(b, i), gk_block)),
                  pl.BlockSpec(gkw.shape, lambda b, i: (0, 0)),
                  pl.BlockSpec((1, hk), lambda b, i: (0, 0)),
                  pl.BlockSpec((1, dv), lambda b, i: (0, 0))],
        out_specs=pl.BlockSpec((c, hv), lambda b, i: (row(b, i), 0)),
        out_shape=jax.ShapeDtypeStruct((t, hv), BF16),
        scratch_shapes=[pltpu.VMEM((GLA_HEADS, dv, dk), F32)],
        compiler_params=_params("parallel", "arbitrary"),
        name="gla",
    )(p, p, p, p, p, gkw, gkb.reshape(1, hk), onorm_g.reshape(1, dv))


def _flash_kernel(q_ref, k_ref, v_ref, o_ref):
    tq = q_ref.shape[0]
    i = pl.program_id(2)
    q = q_ref[...]

    def step(off, carry, mask):
        m, l, acc = carry
        s = _dot_nt(q, k_ref[pl.ds(off, tq), :])
        if mask is not None:
            s = jnp.where(mask, s, MASK_VALUE)
        m_new = jnp.maximum(m, jnp.max(s, axis=-1, keepdims=True))
        alpha = jnp.exp(m - m_new)
        p = jnp.exp(s - m_new)
        l = alpha * l + jnp.sum(p, axis=-1, keepdims=True)
        acc = alpha * acc + _dot(p.astype(BF16), v_ref[pl.ds(off, tq), :])
        return m_new, l, acc

    init = (jnp.full((tq, 1), MASK_VALUE, F32), jnp.zeros((tq, 1), F32), jnp.zeros((tq, v_ref.shape[1]), F32))
    carry = lax.fori_loop(0, i, lambda j, cr: step(pl.multiple_of(j * tq, tq), cr, None), init)
    causal = lax.broadcasted_iota(jnp.int32, (tq, tq), 0) >= lax.broadcasted_iota(jnp.int32, (tq, tq), 1)
    _, l, acc = step(pl.multiple_of(i * tq, tq), carry, causal)
    o_ref[...] = (acc / l).astype(o_ref.dtype)


def _flash(q, k, v, bsz, seq):
    t = q.shape[0]
    qw = q.shape[1] // MLA_HEADS
    tq = _tile(seq, 512)
    nq = seq // tq
    return pl.pallas_call(
        _flash_kernel,
        grid=(bsz, MLA_HEADS, nq),
        in_specs=[pl.BlockSpec((tq, qw), lambda b, h, i: (b * nq + i, h)),
                  pl.BlockSpec((seq, qw), lambda b, h, i: (b, h)),
                  pl.BlockSpec((seq, MLA_V), lambda b, h, i: (b, h))],
        out_specs=pl.BlockSpec((tq, MLA_V), lambda b, h, i: (b * nq + i, h)),
        out_shape=jax.ShapeDtypeStruct((t, MLA_HEADS * MLA_V), BF16),
        compiler_params=_params("parallel", "parallel", "arbitrary"),
        name="flash",
    )(q, k, v)


def _merge_kernel(og_ref, om_ref, ga_ref, gb_ref, wg_ref, wm_ref, o_ref):
    yg = _dot(og_ref[...], wg_ref[...])
    ym = _dot(om_ref[...], wm_ref[...])
    ga = jax.nn.sigmoid(ga_ref[...].astype(F32))
    gb = jax.nn.sigmoid(gb_ref[...].astype(F32))
    o_ref[...] = (ga * yg + gb * ym).astype(o_ref.dtype)


def _merge(o_gla, o_mla, p, gate_col, w_gla, w_mla):
    t, d = o_gla.shape[0], w_gla.shape[1]
    tm = _tile(t, 1024)
    tn = _tile(d, 512)
    ga_blk = gate_col // tn
    gb_blk = (gate_col + d) // tn
    return pl.pallas_call(
        _merge_kernel,
        grid=(t // tm, d // tn),
        in_specs=[pl.BlockSpec((tm, o_gla.shape[1]), lambda i, j: (i, 0)),
                  pl.BlockSpec((tm, o_mla.shape[1]), lambda i, j: (i, 0)),
                  pl.BlockSpec((tm, tn), lambda i, j: (i, ga_blk + j)),
                  pl.BlockSpec((tm, tn), lambda i, j: (i, gb_blk + j)),
                  pl.BlockSpec((w_gla.shape[0], tn), lambda i, j: (0, j)),
                  pl.BlockSpec((w_mla.shape[0], tn), lambda i, j: (0, j))],
        out_specs=pl.BlockSpec((tm, tn), lambda i, j: (i, j)),
        out_shape=jax.ShapeDtypeStruct((t, d), BF16),
        compiler_params=_params("parallel", "arbitrary"),
        name="merge",
    )(o_gla, o_mla, p, p, w_gla, w_mla)


def _outproj_kernel(m_ref, x_ref, mod_ref, w_ref, o_ref):
    y = _dot(m_ref[...], w_ref[...])
    o_ref[...] = x_ref[...] + mod_ref[0, MOD_GATE_MIX:MOD_GATE_MIX + 1, :] * y


def _outproj(merged, x2, mod3, w, seq):
    t, d = x2.shape
    tm = _tile(seq, 1024)
    tn = _tile(d, 512)
    return pl.pallas_call(
        _outproj_kernel,
        grid=(t // tm, d // tn),
        in_specs=[pl.BlockSpec((tm, merged.shape[1]), lambda i, j: (i, 0)),
                  pl.BlockSpec((tm, tn), lambda i, j: (i, j)),
                  pl.BlockSpec((1, N_MOD, tn), lambda i, j: ((i * tm) // seq, 0, j)),
                  pl.BlockSpec((w.shape[0], tn), lambda i, j: (0, j))],
        out_specs=pl.BlockSpec((tm, tn), lambda i, j: (i, j)),
        out_shape=jax.ShapeDtypeStruct((t, d), F32),
        compiler_params=_params("parallel", "arbitrary"),
        name="outproj",
    )(merged, x2, mod3, w)


def _ffn_kernel(x_ref, mod_ref, g_ref, wg_ref, wu_ref, wd_ref, fg_ref, o_ref, h_ref, acc_ref, *, final_norm):
    f = pl.program_id(1)

    @pl.when(f == 0)
    def _():
        h = _modnorm(x_ref[...], g_ref[...],
                     mod_ref[0, MOD_SCALE_FFN:MOD_SCALE_FFN + 1, :], mod_ref[0, MOD_SHIFT_FFN:MOD_SHIFT_FFN + 1, :])
        h_ref[...] = h.astype(BF16)
        acc_ref[...] = jnp.zeros_like(acc_ref)

    h = h_ref[...]
    gate = _dot(h, wg_ref[...])
    up = _dot(h, wu_ref[...])
    act = (gate * jax.nn.sigmoid(gate) * up).astype(BF16)
    acc_ref[...] += _dot(act, wd_ref[...])

    @pl.when(f == pl.num_programs(1) - 1)
    def _():
        y = x_ref[...] + mod_ref[0, MOD_GATE_FFN:MOD_GATE_FFN + 1, :] * acc_ref[...]
        if final_norm:
            y = _rms(y) * fg_ref[...]
        o_ref[...] = y


def _ffn(x1, mod3, gain, w_in, w_down, final_g, seq, final_norm):
    t, d = x1.shape
    hidden = w_down.shape[0]
    tm = _tile(seq, 512)
    tf = _tile(hidden, 512)
    nf = hidden // tf
    kern = functools.partial(_ffn_kernel, final_norm=final_norm)
    return pl.pallas_call(
        kern,
        grid=(t // tm, nf),
        in_specs=[pl.BlockSpec((tm, d), lambda i, f: (i, 0)),
                  pl.BlockSpec((1, N_MOD, d), lambda i, f: ((i * tm) // seq, 0, 0)),
                  pl.BlockSpec((1, d), lambda i, f: (0, 0)),
                  pl.BlockSpec((d, tf), lambda i, f: (0, f)),
                  pl.BlockSpec((d, tf), lambda i, f: (0, nf + f)),
                  pl.BlockSpec((tf, d), lambda i, f: (f, 0)),
                  pl.BlockSpec((1, d), lambda i, f: (0, 0))],
        out_specs=pl.BlockSpec((tm, d), lambda i, f: (i, 0)),
        out_shape=jax.ShapeDtypeStruct((t, d), F32),
        scratch_shapes=[pltpu.VMEM((tm, d), BF16), pltpu.VMEM((tm, d), F32)],
        compiler_params=_params("parallel", "arbitrary"),
        name="ffn",
    )(x1, mod3, gain.reshape(1, d), w_in, w_in, w_down, final_g.reshape(1, d))


def _pack_w_in(w, d, dk_all, dv_all, q_rank, kv_rank):
    widths = (dk_all, dk_all, dv_all, dv_all, GLA_GATE_RANK, q_rank, kv_rank, MLA_ROPE, d, d)
    offs = [0]
    for wd in widths:
        offs.append(offs[-1] + wd)
    wq, wk, wv, wg, wgk, wcq, wckv, wkr, wga, wgb = [w[:, offs[i]:offs[i + 1]] for i in range(len(widths))]
    half = MLA_ROPE // 2
    wkr_rot = jnp.concatenate([wkr[:, half:], wkr[:, :half]], axis=1)
    gk_pad = jnp.zeros((d, V7X_LANES - GLA_GATE_RANK), w.dtype)
    cols = [wq, wk, wv, wg, wga, wgb, wcq, wckv, wkr, wkr_rot, wgk, gk_pad]
    return jnp.concatenate(cols, axis=1).astype(BF16)


def _pack_wuq(w):
    r = w.shape[0]
    half = MLA_ROPE // 2
    w3 = w.reshape(r, MLA_HEADS, MLA_NOPE + MLA_ROPE)
    rope = w3[:, :, MLA_NOPE:]
    rot = jnp.concatenate([rope[:, :, half:], rope[:, :, :half]], axis=2)
    return jnp.concatenate([w3[:, :, :MLA_NOPE], rope, rot], axis=2).reshape(r, -1).astype(BF16)


def kernel(x, c, positions, ada_w, ada_b, norm_mix_g, w_in, gla_gk_w, gla_gk_b, gla_onorm_g, gla_wo, mla_q_norm_g, mla_wuq, mla_kv_norm_g, mla_wukv, mla_wo, w_out, norm_ffn_g, ffn_w_in, ffn_w_down, final_norm_g):
    bsz, seq, d = x.shape
    depth = w_in.shape[0]
    dk_all = gla_gk_w.shape[2]
    dv_all = gla_wo.shape[1]
    dk, dv = dk_all // GLA_HEADS, dv_all // GLA_HEADS
    q_rank, kv_rank = mla_wuq.shape[1], mla_wukv.shape[1]
    lat_col = 2 * dk_all + 2 * dv_all + 2 * d
    lat_width = q_rank + kv_rank + 2 * V7X_LANES
    gate_col = 2 * dk_all + 2 * dv_all
    assert lat_col % lat_width == 0 and (lat_col + lat_width - V7X_LANES) % V7X_LANES == 0

    x2 = x.reshape(bsz * seq, d)
    cos, sin = _rope_tab(positions)
    for l in range(depth):
        mod3 = _ada(c, ada_w[l], ada_b[l]).reshape(bsz, N_MOD, d)
        w_all = _pack_w_in(w_in[l], d, dk_all, dv_all, q_rank, kv_rank)
        p = _inproj(x2, mod3, norm_mix_g[l], w_all, seq)
        q, k, v = _mlaproj(p, lat_col // lat_width, lat_width, cos, sin, mla_q_norm_g[l], mla_kv_norm_g[l],
                           _pack_wuq(mla_wuq[l]), mla_wukv[l].astype(BF16))
        gkw = jnp.zeros((V7X_LANES, dk_all), BF16).at[:GLA_GATE_RANK].set(gla_gk_w[l].astype(BF16))
        o_gla = _gla(p, (lat_col + lat_width - V7X_LANES) // V7X_LANES, gkw, gla_gk_b[l], gla_onorm_g[l],
                     bsz, seq, dk, dv)
        o_mla = _flash(q, k, v, bsz, seq)
        merged = _merge(o_gla, o_mla, p, gate_col, gla_wo[l].astype(BF16), mla_wo[l].astype(BF16))
        x1 = _outproj(merged, x2, mod3, w_out[l].astype(BF16), seq)
        x2 = _ffn(x1, mod3, norm_ffn_g[l], ffn_w_in[l].astype(BF16), ffn_w_down[l].astype(BF16), final_norm_g,
                  seq, final_norm=(l == depth - 1))
    return x2.reshape(bsz, seq, d)
```

```python
import functools

import jax
import jax.numpy as jnp
from jax import lax
from jax.experimental import pallas as pl
from jax.experimental.pallas import tpu as pltpu

F32 = jnp.float32
BF16 = jnp.bfloat16

GLA_HEADS = 4
GLA_GATE_RANK = 16
GLA_GATE_NORMALIZER = 16.0
MLA_HEADS = 16
MLA_NOPE = 128
MLA_ROPE = 64
MLA_V = 128
MLA_QW = MLA_NOPE + 2 * MLA_ROPE
MLA_STRIP = 1024
MLA_CHAIN = 512
ROPE_THETA = 10000.0
LOG2_E = 1.4426950408889634
NORM_EPS = 1e-6
N_MOD = 6
MOD_SHIFT_MIX, MOD_SCALE_MIX, MOD_GATE_MIX, MOD_SHIFT_FFN, MOD_SCALE_FFN, MOD_GATE_FFN = range(N_MOD)

V7X_LANES = 128
V7X_VMEM_LIMIT_BYTES = 56 * 1024 * 1024

MASK_VALUE = -1e30


def _params(*semantics):
    return pltpu.CompilerParams(dimension_semantics=semantics, vmem_limit_bytes=V7X_VMEM_LIMIT_BYTES)


def _dot(a, b):
    return jnp.dot(a, b, preferred_element_type=F32)


def _dot_nt(a, b):
    return lax.dot_general(a, b, (((1,), (1,)), ((), ())), preferred_element_type=F32)


def _dot_tn(a, b):
    return lax.dot_general(a, b, (((0,), (0,)), ((), ())), preferred_element_type=F32)


def _rms(x):
    return x * lax.rsqrt(jnp.mean(x * x, axis=-1, keepdims=True) + NORM_EPS)


def _modnorm(x, gain, scale, shift):
    return _rms(x) * gain * (1.0 + scale) + shift


def _tile(n, want):
    t = min(n, want)
    assert n % t == 0, (n, want)
    return t


def _ada_kernel(c_ref, w_ref, b_ref, o_ref):
    c = c_ref[...]
    act = (c * jax.nn.sigmoid(c)).astype(BF16)
    o_ref[...] = _dot(act, w_ref[...].astype(BF16)) + b_ref[...]


def _ada(c, w, b):
    bsz, d = c.shape
    n = w.shape[1]
    tn = _tile(n, 1024)
    return pl.pallas_call(
        _ada_kernel,
        grid=(n // tn,),
        in_specs=[pl.BlockSpec((bsz, d), lambda j: (0, 0)),
                  pl.BlockSpec((d, tn), lambda j: (0, j)),
                  pl.BlockSpec((1, tn), lambda j: (0, j))],
        out_specs=pl.BlockSpec((bsz, tn), lambda j: (0, j)),
        out_shape=jax.ShapeDtypeStruct((bsz, n), F32),
        compiler_params=_params("arbitrary"),
        name="ada",
    )(c, w, b.reshape(1, n))


def _rope_tab_kernel(pos_ref, f_ref, cos_ref, sin_ref):
    ang = pos_ref[...].astype(F32) * f_ref[...]
    cos_ref[...] = jnp.cos(ang)
    sin_ref[...] = jnp.sin(ang)


def _rope_tab(positions):
    t = positions.size
    half = MLA_ROPE // 2
    inv_freq = ROPE_THETA ** (-jnp.arange(0, MLA_ROPE, 2, dtype=F32) / MLA_ROPE)
    tt = _tile(t, 4096)
    spec = pl.BlockSpec((half, tt), lambda i: (0, i))
    return pl.pallas_call(
        _rope_tab_kernel,
        grid=(t // tt,),
        in_specs=[pl.BlockSpec((1, tt), lambda i: (0, i)), pl.BlockSpec((half, 1), lambda i: (0, 0))],
        out_specs=[spec, spec],
        out_shape=[jax.ShapeDtypeStruct((half, t), F32)] * 2,
        compiler_params=_params("arbitrary"),
        name="rope_tab",
    )(positions.reshape(1, t), inv_freq.reshape(half, 1))


def _inproj_kernel(x_ref, mod_ref, g_ref, w_ref, o_ref, h_ref):
    @pl.when(pl.program_id(1) == 0)
    def _():
        h = _modnorm(x_ref[...], g_ref[...],
                     mod_ref[0, MOD_SCALE_MIX:MOD_SCALE_MIX + 1, :], mod_ref[0, MOD_SHIFT_MIX:MOD_SHIFT_MIX + 1, :])
        h_ref[...] = h.astype(BF16)

    o_ref[...] = _dot(h_ref[...], w_ref[...]).astype(o_ref.dtype)


def _inproj(x2, mod3, gain, w, seq):
    t, d = x2.shape
    n = w.shape[1]
    tm = _tile(seq, 1024)
    tn = _tile(n, 1280)
    return pl.pallas_call(
        _inproj_kernel,
        grid=(t // tm, n // tn),
        in_specs=[pl.BlockSpec((tm, d), lambda i, j: (i, 0)),
                  pl.BlockSpec((1, N_MOD, d), lambda i, j: ((i * tm) // seq, 0, 0)),
                  pl.BlockSpec((1, d), lambda i, j: (0, 0)),
                  pl.BlockSpec((d, tn), lambda i, j: (0, j))],
        out_specs=pl.BlockSpec((tm, tn), lambda i, j: (i, j)),
        out_shape=jax.ShapeDtypeStruct((t, n), BF16),
        scratch_shapes=[pltpu.VMEM((tm, d), BF16)],
        compiler_params=_params("parallel", "arbitrary"),
        name="inproj",
    )(x2, mod3, gain.reshape(1, d), w)


def _mlaproj_kernel(lat_ref, cos_ref, sin_ref, gq_ref, gkv_ref, wqt_ref, wk_ref, wvt_ref, qt_ref, k_ref, vt_ref, *,
                    q_rank, kv_rank, scale):
    tm = lat_ref.shape[0]
    cos = cos_ref[...]
    sin = sin_ref[...]
    cs = jnp.concatenate([cos, cos], axis=0)
    sn = jnp.concatenate([-sin, sin], axis=0)

    cqn = _rms(lat_ref[:, 0:q_rank].astype(F32)) * gq_ref[...]
    ckvn = _rms(lat_ref[:, q_rank:q_rank + kv_rank].astype(F32)) * gkv_ref[...]
    cqn_t = cqn.T.astype(BF16)
    ckvn_t = ckvn.T.astype(BF16)
    ckvn = ckvn.astype(BF16)

    kr_t = lat_ref[:, q_rank + kv_rank:q_rank + kv_rank + V7X_LANES].astype(F32).T
    k_rope_t = kr_t[:MLA_ROPE] * cs + kr_t[MLA_ROPE:] * sn
    k_rope = jnp.concatenate([k_rope_t, jnp.zeros_like(k_rope_t)], axis=0).T.astype(BF16)

    for h in range(MLA_HEADS):
        qt = _dot(wqt_ref[h * MLA_QW:(h + 1) * MLA_QW, :], cqn_t)
        q_rope = qt[MLA_NOPE:MLA_NOPE + MLA_ROPE] * cs + qt[MLA_NOPE + MLA_ROPE:] * sn
        qt_ref[h * MLA_QW:h * MLA_QW + MLA_NOPE, :] = (qt[:MLA_NOPE] * scale).astype(BF16)
        qt_ref[h * MLA_QW + MLA_NOPE:h * MLA_QW + MLA_NOPE + MLA_ROPE, :] = (q_rope * scale).astype(BF16)
        qt_ref[h * MLA_QW + MLA_NOPE + MLA_ROPE:(h + 1) * MLA_QW, :] = jnp.zeros((MLA_ROPE, tm), BF16)
        vt = _dot(wvt_ref[h * MLA_V:(h + 1) * MLA_V, :], ckvn_t).astype(BF16)
        vt_ref[h, 0] = vt
    for g in range(MLA_HEADS // 2):
        kk = _dot(ckvn, wk_ref[:, 2 * g * MLA_NOPE:2 * (g + 1) * MLA_NOPE]).astype(BF16)
        for e in range(2):
            h = 2 * g + e
            k_ref[:, h * MLA_QW:h * MLA_QW + MLA_NOPE] = kk[:, e * MLA_NOPE:(e + 1) * MLA_NOPE]
            k_ref[:, h * MLA_QW + MLA_NOPE:(h + 1) * MLA_QW] = k_rope


def _mlaproj(p, lat_block, lat_width, cos_t, sin_t, gq, gkv, wqt, wk, wvt):
    t = p.shape[0]
    q_rank, kv_rank = gq.shape[0], gkv.shape[0]
    tm = _tile(MLA_STRIP, 512)
    per_strip = MLA_STRIP // tm
    half = MLA_ROPE // 2
    scale = (MLA_NOPE + MLA_ROPE) ** -0.5 * LOG2_E
    kern = functools.partial(_mlaproj_kernel, q_rank=q_rank, kv_rank=kv_rank, scale=scale)
    return pl.pallas_call(
        kern,
        grid=(t // tm,),
        in_specs=[pl.BlockSpec((tm, lat_width), lambda i: (i, lat_block)),
                  pl.BlockSpec((half, tm), lambda i: (0, i)),
                  pl.BlockSpec((half, tm), lambda i: (0, i)),
                  pl.BlockSpec((1, q_rank), lambda i: (0, 0)),
                  pl.BlockSpec((1, kv_rank), lambda i: (0, 0)),
                  pl.BlockSpec(wqt.shape, lambda i: (0, 0)),
                  pl.BlockSpec(wk.shape, lambda i: (0, 0)),
                  pl.BlockSpec(wvt.shape, lambda i: (0, 0))],
        out_specs=[pl.BlockSpec((MLA_HEADS * MLA_QW, tm), lambda i: (0, i)),
                   pl.BlockSpec((tm, MLA_HEADS * MLA_QW), lambda i: (i, 0)),
                   pl.BlockSpec((MLA_HEADS, 1, MLA_V, tm), lambda i: (0, i // per_strip, 0, i % per_strip))],
        out_shape=[jax.ShapeDtypeStruct((MLA_HEADS * MLA_QW, t), BF16),
                   jax.ShapeDtypeStruct((t, MLA_HEADS * MLA_QW), BF16),
                   jax.ShapeDtypeStruct((MLA_HEADS, t // MLA_STRIP, MLA_V, MLA_STRIP), BF16)],
        compiler_params=_params("parallel"),
        name="mlaproj",
    )(p, cos_t, sin_t, gq.reshape(1, q_rank), gkv.reshape(1, kv_rank), wqt, wk, wvt)


def _gla_kernel(q_ref, k_ref, v_ref, g_ref, gk_ref, gkw_ref, gkb_ref, on_ref, o_ref, st_ref, *, dk, dv):
    c = q_ref.shape[0]

    @pl.when(pl.program_id(1) == 0)
    def _():
        st_ref[...] = jnp.zeros_like(st_ref)

    z = _dot(gk_ref[...], gkw_ref[...]) + gkb_ref[...]
    log_a = jax.nn.log_sigmoid(z) * (1.0 / GLA_GATE_NORMALIZER)

    halves = [1 << e for e in range(c.bit_length() - 1)]
    row = lax.broadcasted_iota(jnp.int32, (c, dk), 0)
    ai = lax.broadcasted_iota(jnp.int32, (c, c), 0)
    aj = lax.broadcasted_iota(jnp.int32, (c, c), 1)
    diff = ai ^ aj
    causal_diff = jnp.where(ai > aj, diff, 0)
    level_mask = {s: (causal_diff >> (s.bit_length() - 1)) == 1 for s in halves}
    diag_mask = ai == aj

    for h in range(GLA_HEADS):
        g = log_a[:, h * dk:(h + 1) * dk]
        prefix = g
        total = g
        decay = {}
        for s in halves:
            odd = (row & s) != 0
            sibling = jnp.where(odd, pltpu.roll(total, s, axis=0), pltpu.roll(total, c - s, axis=0))
            decay[s] = jnp.exp(jnp.where(odd, prefix, total - prefix))
            prefix = prefix + jnp.where(odd, sibling, 0.0)
            total = total + sibling
        qh = q_ref[:, h * dk:(h + 1) * dk].astype(F32) * (dk ** -0.5)
        kh = k_ref[:, h * dk:(h + 1) * dk].astype(F32)
        vh = v_ref[:, h * dv:(h + 1) * dv]
        att = jnp.where(diag_mask, _dot_nt(qh.astype(BF16), kh.astype(BF16)), 0.0)
        for s in halves:
            a = _dot_nt((qh * decay[s]).astype(BF16), (kh * decay[s]).astype(BF16))
            att = jnp.where(level_mask[s], a, att)
        st = st_ref[h]
        o = _dot_nt((qh * jnp.exp(prefix)).astype(BF16), st.astype(BF16)) + _dot(att.astype(BF16), vh)
        k_dec = (kh * jnp.exp(total - prefix)).astype(BF16)
        st_ref[h] = st * jnp.exp(total[0:1, :]) + _dot_tn(vh, k_dec)
        gate = g_ref[:, h * dv:(h + 1) * dv].astype(F32)
        o_ref[:, h * dv:(h + 1) * dv] = (_rms(o) * on_ref[...] * (gate * jax.nn.sigmoid(gate))).astype(o_ref.dtype)


def _gla(p, gk_block, gkw, gkb, onorm_g, bsz, seq, dk, dv):
    t = p.shape[0]
    c = _tile(seq, 128)
    nc = seq // c
    hk, hv = GLA_HEADS * dk, GLA_HEADS * dv
    kern = functools.partial(_gla_kernel, dk=dk, dv=dv)
    row = lambda b, i: b * nc + i
    return pl.pallas_call(
        kern,
        grid=(bsz, nc),
        in_specs=[pl.BlockSpec((c, hk), lambda b, i: (row(b, i), 0)),
                  pl.BlockSpec((c, hk), lambda b, i: (row(b, i), 1)),
                  pl.BlockSpec((c, hv), lambda b, i: (row(b, i), (2 * hk) // hv)),
                  pl.BlockSpec((c, hv), lambda b, i: (row(b, i), (2 * hk) // hv + 1)),
                  pl.BlockSpec((c, V7X_LANES), lambda b, i: (row(b, i), gk_block)),
                  pl.BlockSpec(gkw.shape, lambda b, i: (0, 0)),
                  pl.BlockSpec((1, hk), lambda b, i: (0, 0)),
                  pl.BlockSpec((1, dv), lambda b, i: (0, 0))],
        out_specs=pl.BlockSpec((c, hv), lambda b, i: (row(b, i), 0)),
        out_shape=jax.ShapeDtypeStruct((t, hv), BF16),
        scratch_shapes=[pltpu.VMEM((GLA_HEADS, dv, dk), F32)],
        compiler_params=_params("parallel", "arbitrary"),
        name="gla",
    )(p, p, p, p, p, gkw, gkb.reshape(1, hk), onorm_g.reshape(1, dv))


def _flash_kernel(qt_ref, k_ref, vt_ref, o_ref, acc_ref):
    strip, cw = MLA_STRIP, MLA_CHAIN
    nch = strip // cw
    i = pl.program_id(2)
    acc_ref[...] = jnp.zeros_like(acc_ref)

    def scores(a, start, rows):
        return _dot(k_ref[pl.ds(start, rows), :], qt_ref[:, a * cw:(a + 1) * cw])

    def update(a, v_t, s, m, l):
        m_new = jnp.maximum(m, jnp.max(s, axis=0, keepdims=True))
        alpha = jnp.exp2(m - m_new)
        p = jnp.exp2(s - m_new)
        l = alpha * l + jnp.sum(p, axis=0, keepdims=True)
        acc_ref[a] = alpha * acc_ref[a] + _dot(v_t, p.astype(BF16))
        return m_new, l

    def body(j, carry):
        start = pl.multiple_of(j * strip, strip)
        s = [scores(a, start, strip) for a in range(nch)]
        return tuple(update(a, vt_ref[0, j], s[a], *carry[a]) for a in range(nch))

    init = tuple((jnp.full((1, cw), MASK_VALUE, F32), jnp.zeros((1, cw), F32)) for _ in range(nch))
    carry = lax.fori_loop(0, i, body, init)
    start = pl.multiple_of(i * strip, strip)
    s = []
    for a in range(nch):
        rows = (a + 1) * cw
        key = lax.broadcasted_iota(jnp.int32, (rows, cw), 0)
        query = lax.broadcasted_iota(jnp.int32, (rows, cw), 1) + a * cw
        s.append(jnp.where(key <= query, scores(a, start, rows), MASK_VALUE))
    for a in range(nch):
        _, l = update(a, vt_ref[0, i][:, :(a + 1) * cw], s[a], *carry[a])
        o_t = acc_ref[a] / l
        o_ref[a * cw:(a + 1) * cw, :] = o_t.T.astype(o_ref.dtype)


def _flash(qt, k, vt, bsz, seq):
    t = k.shape[0]
    assert seq % MLA_STRIP == 0
    nq = seq // MLA_STRIP
    return pl.pallas_call(
        _flash_kernel,
        grid=(bsz, MLA_HEADS, nq),
        in_specs=[pl.BlockSpec((MLA_QW, MLA_STRIP), lambda b, h, i: (h, b * nq + i)),
                  pl.BlockSpec((seq, MLA_QW), lambda b, h, i: (b, h)),
                  pl.BlockSpec((1, nq, MLA_V, MLA_STRIP), lambda b, h, i: (h, b, 0, 0))],
        out_specs=pl.BlockSpec((MLA_STRIP, MLA_V), lambda b, h, i: (b * nq + i, h)),
        out_shape=jax.ShapeDtypeStruct((t, MLA_HEADS * MLA_V), BF16),
        scratch_shapes=[pltpu.VMEM((MLA_STRIP // MLA_CHAIN, MLA_V, MLA_CHAIN), F32)],
        compiler_params=_params("parallel", "parallel", "arbitrary"),
        name="flash",
    )(qt, k, vt)


def _merge_kernel(og_ref, om_ref, ga_ref, gb_ref, wg_ref, wm_ref, o_ref):
    yg = _dot(og_ref[...], wg_ref[...])
    ym = _dot(om_ref[...], wm_ref[...])
    ga = jax.nn.sigmoid(ga_ref[...].astype(F32))
    gb = jax.nn.sigmoid(gb_ref[...].astype(F32))
    o_ref[...] = (ga * yg + gb * ym).astype(o_ref.dtype)


def _merge(o_gla, o_mla, p, gate_col, w_gla, w_mla):
    t, d = o_gla.shape[0], w_gla.shape[1]
    tm = _tile(t, 1024)
    tn = _tile(d, 512)
    ga_blk = gate_col // tn
    gb_blk = (gate_col + d) // tn
    return pl.pallas_call(
        _merge_kernel,
        grid=(t // tm, d // tn),
        in_specs=[pl.BlockSpec((tm, o_gla.shape[1]), lambda i, j: (i, 0)),
                  pl.BlockSpec((tm, o_mla.shape[1]), lambda i, j: (i, 0)),
                  pl.BlockSpec((tm, tn), lambda i, j: (i, ga_blk + j)),
                  pl.BlockSpec((tm, tn), lambda i, j: (i, gb_blk + j)),
                  pl.BlockSpec((w_gla.shape[0], tn), lambda i, j: (0, j)),
                  pl.BlockSpec((w_mla.shape[0], tn), lambda i, j: (0, j))],
        out_specs=pl.BlockSpec((tm, tn), lambda i, j: (i, j)),
        out_shape=jax.ShapeDtypeStruct((t, d), BF16),
        compiler_params=_params("parallel", "arbitrary"),
        name="merge",
    )(o_gla, o_mla, p, p, w_gla, w_mla)


def _outproj_kernel(m_ref, x_ref, mod_ref, w_ref, o_ref):
    y = _dot(m_ref[...], w_ref[...])
    o_ref[...] = x_ref[...] + mod_ref[0, MOD_GATE_MIX:MOD_GATE_MIX + 1, :] * y


def _outproj(merged, x2, mod3, w, seq):
    t, d = x2.shape
    tm = _tile(seq, 1024)
    tn = _tile(d, 512)
    return pl.pallas_call(
        _outproj_kernel,
        grid=(t // tm, d // tn),
        in_specs=[pl.BlockSpec((tm, merged.shape[1]), lambda i, j: (i, 0)),
                  pl.BlockSpec((tm, tn), lambda i, j: (i, j)),
                  pl.BlockSpec((1, N_MOD, tn), lambda i, j: ((i * tm) // seq, 0, j)),
                  pl.BlockSpec((w.shape[0], tn), lambda i, j: (0, j))],
        out_specs=pl.BlockSpec((tm, tn), lambda i, j: (i, j)),
        out_shape=jax.ShapeDtypeStruct((t, d), F32),
        compiler_params=_params("parallel", "arbitrary"),
        name="outproj",
    )(merged, x2, mod3, w)


def _ffn_kernel(x_ref, mod_ref, g_ref, wg_ref, wu_ref, wd_ref, fg_ref, o_ref, h_ref, acc_ref, *, final_norm):
    f = pl.program_id(1)

    @pl.when(f == 0)
    def _():
        h = _modnorm(x_ref[...], g_ref[...],
                     mod_ref[0, MOD_SCALE_FFN:MOD_SCALE_FFN + 1, :], mod_ref[0, MOD_SHIFT_FFN:MOD_SHIFT_FFN + 1, :])
        h_ref[...] = h.astype(BF16)
        acc_ref[...] = jnp.zeros_like(acc_ref)

    h = h_ref[...]
    gate = _dot(h, wg_ref[...])
    up = _dot(h, wu_ref[...])
    act = (gate * jax.nn.sigmoid(gate) * up).astype(BF16)
    acc_ref[...] += _dot(act, wd_ref[...])

    @pl.when(f == pl.num_programs(1) - 1)
    def _():
        y = x_ref[...] + mod_ref[0, MOD_GATE_FFN:MOD_GATE_FFN + 1, :] * acc_ref[...]
        if final_norm:
            y = _rms(y) * fg_ref[...]
        o_ref[...] = y


def _ffn(x1, mod3, gain, w_in, w_down, final_g, seq, final_norm):
    t, d = x1.shape
    hidden = w_down.shape[0]
    tm = _tile(seq, 512)
    tf = _tile(hidden, 512)
    nf = hidden // tf
    kern = functools.partial(_ffn_kernel, final_norm=final_norm)
    return pl.pallas_call(
        kern,
        grid=(t // tm, nf),
        in_specs=[pl.BlockSpec((tm, d), lambda i, f: (i, 0)),
                  pl.BlockSpec((1, N_MOD, d), lambda i, f: ((i * tm) // seq, 0, 0)),
                  pl.BlockSpec((1, d), lambda i, f: (0, 0)),
                  pl.BlockSpec((d, tf), lambda i, f: (0, f)),
                  pl.BlockSpec((d, tf), lambda i, f: (0, nf + f)),
                  pl.BlockSpec((tf, d), lambda i, f: (f, 0)),
                  pl.BlockSpec((1, d), lambda i, f: (0, 0))],
        out_specs=pl.BlockSpec((tm, d), lambda i, f: (i, 0)),
        out_shape=jax.ShapeDtypeStruct((t, d), F32),
        scratch_shapes=[pltpu.VMEM((tm, d), BF16), pltpu.VMEM((tm, d), F32)],
        compiler_params=_params("parallel", "arbitrary"),
        name="ffn",
    )(x1, mod3, gain.reshape(1, d), w_in, w_in, w_down, final_g.reshape(1, d))


def _pack_w_in(w, d, dk_all, dv_all, q_rank, kv_rank):
    widths = (dk_all, dk_all, dv_all, dv_all, GLA_GATE_RANK, q_rank, kv_rank, MLA_ROPE, d, d)
    offs = [0]
    for wd in widths:
        offs.append(offs[-1] + wd)
    wq, wk, wv, wg, wgk, wcq, wckv, wkr, wga, wgb = [w[:, offs[i]:offs[i + 1]] for i in range(len(widths))]
    half = MLA_ROPE // 2
    wkr_rot = jnp.concatenate([wkr[:, half:], wkr[:, :half]], axis=1)
    gk_pad = jnp.zeros((d, V7X_LANES - GLA_GATE_RANK), w.dtype)
    cols = [wq, wk, wv, wg, wga, wgb, wcq, wckv, wkr, wkr_rot, wgk, gk_pad]
    return jnp.concatenate(cols, axis=1).astype(BF16)


def _pack_wuq_t(w):
    r = w.shape[0]
    half = MLA_ROPE // 2
    w3 = w.reshape(r, MLA_HEADS, MLA_NOPE + MLA_ROPE)
    rope = w3[:, :, MLA_NOPE:]
    rot = jnp.concatenate([rope[:, :, half:], rope[:, :, :half]], axis=2)
    return jnp.concatenate([w3[:, :, :MLA_NOPE], rope, rot], axis=2).reshape(r, -1).T.astype(BF16)


def _pack_wukv(w):
    r = w.shape[0]
    w3 = w.reshape(r, MLA_HEADS, MLA_NOPE + MLA_V)
    wk = w3[:, :, :MLA_NOPE].reshape(r, -1).astype(BF16)
    wvt = w3[:, :, MLA_NOPE:].reshape(r, -1).T.astype(BF16)
    return wk, wvt


def kernel(x, c, positions, ada_w, ada_b, norm_mix_g, w_in, gla_gk_w, gla_gk_b, gla_onorm_g, gla_wo, mla_q_norm_g, mla_wuq, mla_kv_norm_g, mla_wukv, mla_wo, w_out, norm_ffn_g, ffn_w_in, ffn_w_down, final_norm_g):
    bsz, seq, d = x.shape
    depth = w_in.shape[0]
    dk_all = gla_gk_w.shape[2]
    dv_all = gla_wo.shape[1]
    dk, dv = dk_all // GLA_HEADS, dv_all // GLA_HEADS
    q_rank, kv_rank = mla_wuq.shape[1], mla_wukv.shape[1]
    lat_col = 2 * dk_all + 2 * dv_all + 2 * d
    lat_width = q_rank + kv_rank + 2 * V7X_LANES
    gate_col = 2 * dk_all + 2 * dv_all
    assert lat_col % lat_width == 0 and (lat_col + lat_width - V7X_LANES) % V7X_LANES == 0

    x2 = x.reshape(bsz * seq, d)
    cos_t, sin_t = _rope_tab(positions)
    for l in range(depth):
        mod3 = _ada(c, ada_w[l], ada_b[l]).reshape(bsz, N_MOD, d)
        w_all = _pack_w_in(w_in[l], d, dk_all, dv_all, q_rank, kv_rank)
        p = _inproj(x2, mod3, norm_mix_g[l], w_all, seq)
        wk, wvt = _pack_wukv(mla_wukv[l])
        qt, k, vt = _mlaproj(p, lat_col // lat_width, lat_width, cos_t, sin_t, mla_q_norm_g[l], mla_kv_norm_g[l],
                             _pack_wuq_t(mla_wuq[l]), wk, wvt)
        gkw = jnp.zeros((V7X_LANES, dk_all), BF16).at[:GLA_GATE_RANK].set(gla_gk_w[l].astype(BF16))
        o_gla = _gla(p, (lat_col + lat_width - V7X_LANES) // V7X_LANES, gkw, gla_gk_b[l], gla_onorm_g[l],
                     bsz, seq, dk, dv)
        o_mla = _flash(qt, k, vt, bsz, seq)
        merged = _merge(o_gla, o_mla, p, gate_col, gla_wo[l].astype(BF16), mla_wo[l].astype(BF16))
        x1 = _outproj(merged, x2, mod3, w_out[l].astype(BF16), seq)
        x2 = _ffn(x1, mod3, norm_ffn_g[l], ffn_w_in[l].astype(BF16), ffn_w_down[l].astype(BF16), final_norm_g,
                  seq, final_norm=(l == depth - 1))
    return x2.reshape(bsz, seq, d)
```

```python
import functools

import jax
import jax.numpy as jnp
from jax import lax
from jax.experimental import pallas as pl
from jax.experimental.pallas import tpu as pltpu

F32 = jnp.float32
BF16 = jnp.bfloat16

GLA_HEADS = 4
GLA_GATE_RANK = 16
GLA_GATE_NORMALIZER = 16.0
MLA_HEADS = 16
MLA_NOPE = 128
MLA_ROPE = 64
MLA_V = 128
MLA_QW = MLA_NOPE + 2 * MLA_ROPE
MLA_STRIP = 1024
MLA_CHAIN = 512
ROPE_THETA = 10000.0
LOG2_E = 1.4426950408889634
NORM_EPS = 1e-6
N_MOD = 6
MOD_SHIFT_MIX, MOD_SCALE_MIX, MOD_GATE_MIX, MOD_SHIFT_FFN, MOD_SCALE_FFN, MOD_GATE_FFN = range(N_MOD)

V7X_LANES = 128
V7X_VMEM_LIMIT_BYTES = 56 * 1024 * 1024

MASK_VALUE = -1e30


def _params(*semantics):
    return pltpu.CompilerParams(dimension_semantics=semantics, vmem_limit_bytes=V7X_VMEM_LIMIT_BYTES)


def _dot(a, b):
    return jnp.dot(a, b, preferred_element_type=F32)


def _dot_nt(a, b):
    return lax.dot_general(a, b, (((1,), (1,)), ((), ())), preferred_element_type=F32)


def _dot_tn(a, b):
    return lax.dot_general(a, b, (((0,), (0,)), ((), ())), preferred_element_type=F32)


def _rms(x):
    return x * lax.rsqrt(jnp.mean(x * x, axis=-1, keepdims=True) + NORM_EPS)


def _modnorm(x, gain, scale, shift):
    return _rms(x) * (gain * (1.0 + scale)) + shift


def _tile(n, want):
    t = min(n, want)
    assert n % t == 0, (n, want)
    return t


def _ada_kernel(c_ref, w_ref, b_ref, o_ref):
    c = c_ref[...]
    act = (c * jax.nn.sigmoid(c)).astype(BF16)
    o_ref[...] = _dot(act, w_ref[...].astype(BF16)) + b_ref[...]


def _ada(c, w, b):
    bsz, d = c.shape
    n = w.shape[1]
    tn = _tile(n, 1024)
    return pl.pallas_call(
        _ada_kernel,
        grid=(n // tn,),
        in_specs=[pl.BlockSpec((bsz, d), lambda j: (0, 0)),
                  pl.BlockSpec((d, tn), lambda j: (0, j)),
                  pl.BlockSpec((1, tn), lambda j: (0, j))],
        out_specs=pl.BlockSpec((bsz, tn), lambda j: (0, j)),
        out_shape=jax.ShapeDtypeStruct((bsz, n), F32),
        compiler_params=_params("arbitrary"),
        name="ada",
    )(c, w, b.reshape(1, n))


def _rope_tab_kernel(pos_ref, f_ref, cos_ref, sin_ref):
    ang = pos_ref[...].astype(F32) * f_ref[...]
    cos_ref[...] = jnp.cos(ang)
    sin_ref[...] = jnp.sin(ang)


def _rope_tab(positions):
    t = positions.size
    half = MLA_ROPE // 2
    inv_freq = ROPE_THETA ** (-jnp.arange(0, MLA_ROPE, 2, dtype=F32) / MLA_ROPE)
    tt = _tile(t, 4096)
    spec = pl.BlockSpec((half, tt), lambda i: (0, i))
    return pl.pallas_call(
        _rope_tab_kernel,
        grid=(t // tt,),
        in_specs=[pl.BlockSpec((1, tt), lambda i: (0, i)), pl.BlockSpec((half, 1), lambda i: (0, 0))],
        out_specs=[spec, spec],
        out_shape=[jax.ShapeDtypeStruct((half, t), F32)] * 2,
        compiler_params=_params("arbitrary"),
        name="rope_tab",
    )(positions.reshape(1, t), inv_freq.reshape(half, 1))


def _inproj_kernel(x_ref, mod_ref, g_ref, w_ref, o_ref, h_ref):
    @pl.when(pl.program_id(1) == 0)
    def _():
        h = _modnorm(x_ref[...], g_ref[...],
                     mod_ref[0, MOD_SCALE_MIX:MOD_SCALE_MIX + 1, :], mod_ref[0, MOD_SHIFT_MIX:MOD_SHIFT_MIX + 1, :])
        h_ref[...] = h.astype(BF16)

    o_ref[...] = _dot(h_ref[...], w_ref[...]).astype(o_ref.dtype)


def _inproj(x2, mod3, gain, w, seq):
    t, d = x2.shape
    n = w.shape[1]
    tm = _tile(seq, 1024)
    tn = _tile(n, 1280)
    return pl.pallas_call(
        _inproj_kernel,
        grid=(t // tm, n // tn),
        in_specs=[pl.BlockSpec((tm, d), lambda i, j: (i, 0)),
                  pl.BlockSpec((1, N_MOD, d), lambda i, j: ((i * tm) // seq, 0, 0)),
                  pl.BlockSpec((1, d), lambda i, j: (0, 0)),
                  pl.BlockSpec((d, tn), lambda i, j: (0, j))],
        out_specs=pl.BlockSpec((tm, tn), lambda i, j: (i, j)),
        out_shape=jax.ShapeDtypeStruct((t, n), BF16),
        scratch_shapes=[pltpu.VMEM((tm, d), BF16)],
        compiler_params=_params("parallel", "arbitrary"),
        name="inproj",
    )(x2, mod3, gain.reshape(1, d), w)


def _mlaproj_kernel(lat_ref, cos_ref, sin_ref, gq_ref, gkv_ref, wqt_ref, wk_ref, wvt_ref, qt_ref, k_ref, vt_ref, *,
                    q_rank, kv_rank, scale):
    tm = lat_ref.shape[0]
    cos = cos_ref[...]
    sin = sin_ref[...]
    cs = jnp.concatenate([cos, cos], axis=0)
    sn = jnp.concatenate([-sin, sin], axis=0)

    cqn = _rms(lat_ref[:, 0:q_rank].astype(F32)) * gq_ref[...]
    ckvn = _rms(lat_ref[:, q_rank:q_rank + kv_rank].astype(F32)) * gkv_ref[...]
    cqn_t = cqn.T.astype(BF16)
    ckvn_t = ckvn.T.astype(BF16)
    ckvn = ckvn.astype(BF16)

    kr_t = lat_ref[:, q_rank + kv_rank:q_rank + kv_rank + V7X_LANES].astype(F32).T
    k_rope_t = kr_t[:MLA_ROPE] * cs + kr_t[MLA_ROPE:] * sn
    k_rope = jnp.concatenate([k_rope_t, jnp.zeros_like(k_rope_t)], axis=0).T.astype(BF16)

    for h in range(MLA_HEADS):
        qt = _dot(wqt_ref[h * MLA_QW:(h + 1) * MLA_QW, :], cqn_t)
        q_rope = qt[MLA_NOPE:MLA_NOPE + MLA_ROPE] * cs + qt[MLA_NOPE + MLA_ROPE:] * sn
        qt_ref[h * MLA_QW:h * MLA_QW + MLA_NOPE, :] = (qt[:MLA_NOPE] * scale).astype(BF16)
        qt_ref[h * MLA_QW + MLA_NOPE:h * MLA_QW + MLA_NOPE + MLA_ROPE, :] = (q_rope * scale).astype(BF16)
        qt_ref[h * MLA_QW + MLA_NOPE + MLA_ROPE:(h + 1) * MLA_QW, :] = jnp.zeros((MLA_ROPE, tm), BF16)
        vt_ref[h, 0] = _dot(wvt_ref[h * MLA_V:(h + 1) * MLA_V, :], ckvn_t).astype(BF16)
    for g in range(MLA_HEADS // 2):
        kk = _dot(ckvn, wk_ref[:, 2 * g * MLA_NOPE:2 * (g + 1) * MLA_NOPE]).astype(BF16)
        for e in range(2):
            h = 2 * g + e
            k_ref[:, h * MLA_QW:h * MLA_QW + MLA_NOPE] = kk[:, e * MLA_NOPE:(e + 1) * MLA_NOPE]
            k_ref[:, h * MLA_QW + MLA_NOPE:(h + 1) * MLA_QW] = k_rope


def _mlaproj(p, lat_block, lat_width, cos_t, sin_t, gq, gkv, wqt, wk, wvt):
    t = p.shape[0]
    q_rank, kv_rank = gq.shape[0], gkv.shape[0]
    tm = _tile(MLA_STRIP, 512)
    per_strip = MLA_STRIP // tm
    half = MLA_ROPE // 2
    scale = (MLA_NOPE + MLA_ROPE) ** -0.5 * LOG2_E
    kern = functools.partial(_mlaproj_kernel, q_rank=q_rank, kv_rank=kv_rank, scale=scale)
    return pl.pallas_call(
        kern,
        grid=(t // tm,),
        in_specs=[pl.BlockSpec((tm, lat_width), lambda i: (i, lat_block)),
                  pl.BlockSpec((half, tm), lambda i: (0, i)),
                  pl.BlockSpec((half, tm), lambda i: (0, i)),
                  pl.BlockSpec((1, q_rank), lambda i: (0, 0)),
                  pl.BlockSpec((1, kv_rank), lambda i: (0, 0)),
                  pl.BlockSpec(wqt.shape, lambda i: (0, 0)),
                  pl.BlockSpec(wk.shape, lambda i: (0, 0)),
                  pl.BlockSpec(wvt.shape, lambda i: (0, 0))],
        out_specs=[pl.BlockSpec((MLA_HEADS * MLA_QW, tm), lambda i: (0, i)),
                   pl.BlockSpec((tm, MLA_HEADS * MLA_QW), lambda i: (i, 0)),
                   pl.BlockSpec((MLA_HEADS, 1, MLA_V, tm), lambda i: (0, i // per_strip, 0, i % per_strip))],
        out_shape=[jax.ShapeDtypeStruct((MLA_HEADS * MLA_QW, t), BF16),
                   jax.ShapeDtypeStruct((t, MLA_HEADS * MLA_QW), BF16),
                   jax.ShapeDtypeStruct((MLA_HEADS, t // MLA_STRIP, MLA_V, MLA_STRIP), BF16)],
        compiler_params=_params("parallel"),
        name="mlaproj",
    )(p, cos_t, sin_t, gq.reshape(1, q_rank), gkv.reshape(1, kv_rank), wqt, wk, wvt)


def _gla_kernel(q_ref, k_ref, v_ref, g_ref, gk_ref, gkw_ref, gkb_ref, on_ref, o_ref, st_ref, *, dk, dv):
    c = q_ref.shape[0]

    @pl.when(pl.program_id(1) == 0)
    def _():
        st_ref[...] = jnp.zeros_like(st_ref)

    z = _dot(gk_ref[...], gkw_ref[...]) + gkb_ref[...]
    log_a = jax.nn.log_sigmoid(z) * (1.0 / GLA_GATE_NORMALIZER)

    halves = [1 << e for e in range(c.bit_length() - 1)]
    row = lax.broadcasted_iota(jnp.int32, (c, dk), 0)
    ai = lax.broadcasted_iota(jnp.int32, (c, c), 0)
    aj = lax.broadcasted_iota(jnp.int32, (c, c), 1)
    diff = ai ^ aj
    causal_diff = jnp.where(ai > aj, diff, 0)
    level_mask = {s: (causal_diff >> (s.bit_length() - 1)) == 1 for s in halves}
    diag_mask = ai == aj

    for h in range(GLA_HEADS):
        g = log_a[:, h * dk:(h + 1) * dk]
        prefix = g
        total = g
        decay = {}
        for s in halves:
            odd = (row & s) != 0
            sibling = jnp.where(odd, pltpu.roll(total, s, axis=0), pltpu.roll(total, c - s, axis=0))
            decay[s] = jnp.exp(jnp.where(odd, prefix, total - prefix))
            prefix = prefix + jnp.where(odd, sibling, 0.0)
            total = total + sibling
        qh = q_ref[:, h * dk:(h + 1) * dk].astype(F32) * (dk ** -0.5)
        kh = k_ref[:, h * dk:(h + 1) * dk].astype(F32)
        vh = v_ref[:, h * dv:(h + 1) * dv]
        att = jnp.where(diag_mask, _dot_nt(qh.astype(BF16), kh.astype(BF16)), 0.0)
        for s in halves:
            a = _dot_nt((qh * decay[s]).astype(BF16), (kh * decay[s]).astype(BF16))
            att = jnp.where(level_mask[s], a, att)
        st = st_ref[h]
        o = _dot_nt((qh * jnp.exp(prefix)).astype(BF16), st.astype(BF16)) + _dot(att.astype(BF16), vh)
        k_dec = (kh * jnp.exp(total - prefix)).astype(BF16)
        st_ref[h] = st * jnp.exp(total[0:1, :]) + _dot_tn(vh, k_dec)
        gate = g_ref[:, h * dv:(h + 1) * dv].astype(F32)
        o_ref[:, h * dv:(h + 1) * dv] = (_rms(o) * on_ref[...] * (gate * jax.nn.sigmoid(gate))).astype(o_ref.dtype)


def _gla(p, gk_block, gkw, gkb, onorm_g, bsz, seq, dk, dv):
    t = p.shape[0]
    c = _tile(seq, 128)
    nc = seq // c
    hk, hv = GLA_HEADS * dk, GLA_HEADS * dv
    kern = functools.partial(_gla_kernel, dk=dk, dv=dv)
    row = lambda b, i: b * nc + i
    return pl.pallas_call(
        kern,
        grid=(bsz, nc),
        in_specs=[pl.BlockSpec((c, hk), lambda b, i: (row(b, i), 0)),
                  pl.BlockSpec((c, hk), lambda b, i: (row(b, i), 1)),
                  pl.BlockSpec((c, hv), lambda b, i: (row(b, i), (2 * hk) // hv)),
                  pl.BlockSpec((c, hv), lambda b, i: (row(b, i), (2 * hk) // hv + 1)),
                  pl.BlockSpec((c, V7X_LANES), lambda b, i: (row(b, i), gk_block)),
                  pl.BlockSpec(gkw.shape, lambda b, i: (0, 0)),
                  pl.BlockSpec((1, hk), lambda b, i: (0, 0)),
                  pl.BlockSpec((1, dv), lambda b, i: (0, 0))],
        out_specs=pl.BlockSpec((c, hv), lambda b, i: (row(b, i), 0)),
        out_shape=jax.ShapeDtypeStruct((t, hv), BF16),
        scratch_shapes=[pltpu.VMEM((GLA_HEADS, dv, dk), F32)],
        compiler_params=_params("parallel", "arbitrary"),
        name="gla",
    )(p, p, p, p, p, gkw, gkb.reshape(1, hk), onorm_g.reshape(1, dv))


def _flash_kernel(qt_ref, k_ref, vt_ref, o_ref, acc_ref, s0_ref):
    strip, cw = MLA_STRIP, MLA_CHAIN
    nch = strip // cw
    i = pl.program_id(2)
    acc_ref[...] = jnp.zeros_like(acc_ref)

    def scores(a, start, rows):
        return _dot(k_ref[pl.ds(start, rows), :], qt_ref[:, a * cw:(a + 1) * cw])

    def update(a, v_t, s, s_max, m, l):
        m_new = jnp.maximum(m, s_max)
        alpha = jnp.exp2(m - m_new)
        p = jnp.exp2(s - m_new)
        l = alpha * l + jnp.sum(p, axis=0, keepdims=True)
        acc_ref[a] = alpha * acc_ref[a] + _dot(v_t, p.astype(BF16))
        return m_new, l

    def body(j, carry):
        c0, c1, s0_max = carry
        start = pl.multiple_of(j * strip, strip)
        s1 = scores(1, start, strip)
        c0 = update(0, vt_ref[0, j], s0_ref[...], s0_max, *c0)
        s0 = scores(0, start + strip, strip)
        s0_ref[...] = s0
        c1 = update(1, vt_ref[0, j], s1, jnp.max(s1, axis=0, keepdims=True), *c1)
        return c0, c1, jnp.max(s0, axis=0, keepdims=True)

    assert nch == 2
    s0 = scores(0, 0, strip)
    s0_ref[...] = s0
    stats = (jnp.full((1, cw), MASK_VALUE, F32), jnp.zeros((1, cw), F32))
    carry = lax.fori_loop(0, i, body, (stats, stats, jnp.max(s0, axis=0, keepdims=True)))
    start = pl.multiple_of(i * strip, strip)
    s = [s0_ref[:cw, :], scores(1, start, strip)]
    for a in range(nch):
        rows = (a + 1) * cw
        key = lax.broadcasted_iota(jnp.int32, (rows, cw), 0)
        query = lax.broadcasted_iota(jnp.int32, (rows, cw), 1) + a * cw
        sm = jnp.where(key <= query, s[a], MASK_VALUE)
        _, l = update(a, vt_ref[0, i][:, :rows], sm, jnp.max(sm, axis=0, keepdims=True), *carry[a])
        o_t = acc_ref[a] / l
        o_ref[a * cw:(a + 1) * cw, :] = o_t.T.astype(o_ref.dtype)


def _flash(qt, k, vt, bsz, seq):
    t = k.shape[0]
    assert seq % MLA_STRIP == 0
    nq = seq // MLA_STRIP
    return pl.pallas_call(
        _flash_kernel,
        grid=(bsz, MLA_HEADS, nq),
        in_specs=[pl.BlockSpec((MLA_QW, MLA_STRIP), lambda b, h, i: (h, b * nq + i)),
                  pl.BlockSpec((seq, MLA_QW), lambda b, h, i: (b, h)),
                  pl.BlockSpec((1, nq, MLA_V, MLA_STRIP), lambda b, h, i: (h, b, 0, 0))],
        out_specs=pl.BlockSpec((MLA_STRIP, MLA_V), lambda b, h, i: (b * nq + i, h)),
        out_shape=jax.ShapeDtypeStruct((t, MLA_HEADS * MLA_V), BF16),
        scratch_shapes=[pltpu.VMEM((MLA_STRIP // MLA_CHAIN, MLA_V, MLA_CHAIN), F32),
                        pltpu.VMEM((MLA_STRIP, MLA_CHAIN), F32)],
        compiler_params=_params("parallel", "parallel", "arbitrary"),
        name="flash",
    )(qt, k, vt)


def _mixout_kernel(og_ref, om_ref, ga_ref, gb_ref, x_ref, mod_ref, wg_ref, wm_ref, wo_ref, o_ref):
    yg = _dot(og_ref[...], wg_ref[...])
    ym = _dot(om_ref[...], wm_ref[...])
    ga = jax.nn.sigmoid(ga_ref[...].astype(F32))
    gb = jax.nn.sigmoid(gb_ref[...].astype(F32))
    merged = (ga * yg + gb * ym).astype(BF16)
    o_ref[...] = x_ref[...] + mod_ref[0, MOD_GATE_MIX:MOD_GATE_MIX + 1, :] * _dot(merged, wo_ref[...])


def _mixout(o_gla, o_mla, p, gate_col, x2, mod3, w_gla, w_mla, w_out, seq):
    t, d = x2.shape
    tm = _tile(seq, 256)
    assert gate_col % d == 0
    row = lambda i: (i, 0)
    resident = lambda w: pl.BlockSpec(w.shape, lambda i: (0, 0), pipeline_mode=pl.Buffered(1))
    return pl.pallas_call(
        _mixout_kernel,
        grid=(t // tm,),
        in_specs=[pl.BlockSpec((tm, o_gla.shape[1]), row),
                  pl.BlockSpec((tm, o_mla.shape[1]), row),
                  pl.BlockSpec((tm, d), lambda i: (i, gate_col // d)),
                  pl.BlockSpec((tm, d), lambda i: (i, gate_col // d + 1)),
                  pl.BlockSpec((tm, d), row),
                  pl.BlockSpec((1, N_MOD, d), lambda i: ((i * tm) // seq, 0, 0)),
                  resident(w_gla), resident(w_mla), resident(w_out)],
        out_specs=pl.BlockSpec((tm, d), row),
        out_shape=jax.ShapeDtypeStruct((t, d), F32),
        compiler_params=_params("parallel"),
        name="mixout",
    )(o_gla, o_mla, p, p, x2, mod3, w_gla, w_mla, w_out)


def _ffn_kernel(x_ref, mod_ref, g_ref, wg_ref, wu_ref, wd_ref, fg_ref, o_ref, h_ref, acc_ref, *, final_norm):
    f = pl.program_id(1)

    @pl.when(f == 0)
    def _():
        h = _modnorm(x_ref[...], g_ref[...],
                     mod_ref[0, MOD_SCALE_FFN:MOD_SCALE_FFN + 1, :], mod_ref[0, MOD_SHIFT_FFN:MOD_SHIFT_FFN + 1, :])
        h_ref[...] = h.astype(BF16)
        acc_ref[...] = jnp.zeros_like(acc_ref)

    h = h_ref[...]
    gate = _dot(h, wg_ref[...])
    up = _dot(h, wu_ref[...])
    act = (gate * jax.nn.sigmoid(gate) * up).astype(BF16)
    acc_ref[...] += _dot(act, wd_ref[...])

    @pl.when(f == pl.num_programs(1) - 1)
    def _():
        y = x_ref[...] + mod_ref[0, MOD_GATE_FFN:MOD_GATE_FFN + 1, :] * acc_ref[...]
        if final_norm:
            y = _rms(y) * fg_ref[...]
        o_ref[...] = y


def _ffn(x1, mod3, gain, w_in, w_down, final_g, seq, final_norm):
    t, d = x1.shape
    hidden = w_down.shape[0]
    tm = _tile(seq, 512)
    tf = _tile(hidden, 512)
    nf = hidden // tf
    kern = functools.partial(_ffn_kernel, final_norm=final_norm)
    return pl.pallas_call(
        kern,
        grid=(t // tm, nf),
        in_specs=[pl.BlockSpec((tm, d), lambda i, f: (i, 0)),
                  pl.BlockSpec((1, N_MOD, d), lambda i, f: ((i * tm) // seq, 0, 0)),
                  pl.BlockSpec((1, d), lambda i, f: (0, 0)),
                  pl.BlockSpec((d, tf), lambda i, f: (0, f)),
                  pl.BlockSpec((d, tf), lambda i, f: (0, nf + f)),
                  pl.BlockSpec((tf, d), lambda i, f: (f, 0)),
                  pl.BlockSpec((1, d), lambda i, f: (0, 0))],
        out_specs=pl.BlockSpec((tm, d), lambda i, f: (i, 0)),
        out_shape=jax.ShapeDtypeStruct((t, d), F32),
        scratch_shapes=[pltpu.VMEM((tm, d), BF16), pltpu.VMEM((tm, d), F32)],
        compiler_params=_params("parallel", "arbitrary"),
        name="ffn",
    )(x1, mod3, gain.reshape(1, d), w_in, w_in, w_down, final_g.reshape(1, d))


def _pack_w_in(w, d, dk_all, dv_all, q_rank, kv_rank):
    widths = (dk_all, dk_all, dv_all, dv_all, GLA_GATE_RANK, q_rank, kv_rank, MLA_ROPE, d, d)
    offs = [0]
    for wd in widths:
        offs.append(offs[-1] + wd)
    w = w.astype(BF16)
    wq, wk, wv, wg, wgk, wcq, wckv, wkr, wga, wgb = [w[:, offs[i]:offs[i + 1]] for i in range(len(widths))]
    half = MLA_ROPE // 2
    wkr_rot = jnp.concatenate([wkr[:, half:], wkr[:, :half]], axis=1)
    gk_pad = jnp.zeros((d, V7X_LANES - GLA_GATE_RANK), BF16)
    cols = [wq, wk, wv, wg, wga, wgb, wcq, wckv, wkr, wkr_rot, wgk, gk_pad]
    return jnp.concatenate(cols, axis=1)


def _pack_wuq_t(w):
    r = w.shape[0]
    half = MLA_ROPE // 2
    w3 = w.reshape(r, MLA_HEADS, MLA_NOPE + MLA_ROPE)
    rope = w3[:, :, MLA_NOPE:]
    rot = jnp.concatenate([rope[:, :, half:], rope[:, :, :half]], axis=2)
    return jnp.concatenate([w3[:, :, :MLA_NOPE], rope, rot], axis=2).reshape(r, -1).T.astype(BF16)


def _pack_wukv(w):
    r = w.shape[0]
    w3 = w.reshape(r, MLA_HEADS, MLA_NOPE + MLA_V)
    wk = w3[:, :, :MLA_NOPE].reshape(r, -1).astype(BF16)
    wvt = w3[:, :, MLA_NOPE:].reshape(r, -1).T.astype(BF16)
    return wk, wvt


def kernel(x, c, positions, ada_w, ada_b, norm_mix_g, w_in, gla_gk_w, gla_gk_b, gla_onorm_g, gla_wo, mla_q_norm_g, mla_wuq, mla_kv_norm_g, mla_wukv, mla_wo, w_out, norm_ffn_g, ffn_w_in, ffn_w_down, final_norm_g):
    bsz, seq, d = x.shape
    depth = w_in.shape[0]
    dk_all = gla_gk_w.shape[2]
    dv_all = gla_wo.shape[1]
    dk, dv = dk_all // GLA_HEADS, dv_all // GLA_HEADS
    q_rank, kv_rank = mla_wuq.shape[1], mla_wukv.shape[1]
    lat_col = 2 * dk_all + 2 * dv_all + 2 * d
    lat_width = q_rank + kv_rank + 2 * V7X_LANES
    gate_col = 2 * dk_all + 2 * dv_all
    assert lat_col % lat_width == 0 and (lat_col + lat_width - V7X_LANES) % V7X_LANES == 0

    x2 = x.reshape(bsz * seq, d)
    cos_t, sin_t = _rope_tab(positions)
    for l in range(depth):
        mod3 = _ada(c, ada_w[l], ada_b[l]).reshape(bsz, N_MOD, d)
        w_all = _pack_w_in(w_in[l], d, dk_all, dv_all, q_rank, kv_rank)
        p = _inproj(x2, mod3, norm_mix_g[l], w_all, seq)
        wk, wvt = _pack_wukv(mla_wukv[l])
        qt, k, vt = _mlaproj(p, lat_col // lat_width, lat_width, cos_t, sin_t, mla_q_norm_g[l], mla_kv_norm_g[l],
                             _pack_wuq_t(mla_wuq[l]), wk, wvt)
        gkw = jnp.zeros((V7X_LANES, dk_all), BF16).at[:GLA_GATE_RANK].set(gla_gk_w[l].astype(BF16))
        o_gla = _gla(p, (lat_col + lat_width - V7X_LANES) // V7X_LANES, gkw, gla_gk_b[l], gla_onorm_g[l],
                     bsz, seq, dk, dv)
        o_mla = _flash(qt, k, vt, bsz, seq)
        x1 = _mixout(o_gla, o_mla, p, gate_col, x2, mod3, gla_wo[l].astype(BF16), mla_wo[l].astype(BF16),
                     w_out[l].astype(BF16), seq)
        x2 = _ffn(x1, mod3, norm_ffn_g[l], ffn_w_in[l].astype(BF16), ffn_w_down[l].astype(BF16), final_norm_g,
                  seq, final_norm=(l == depth - 1))
    return x2.reshape(bsz, seq, d)
```

```python
import functools

import jax
import jax.numpy as jnp
from jax import lax
from jax.experimental import pallas as pl
from jax.experimental.pallas import tpu as pltpu

F32 = jnp.float32
BF16 = jnp.bfloat16

GLA_HEADS = 4
GLA_GATE_RANK = 16
GLA_GATE_NORMALIZER = 16.0
MLA_HEADS = 16
MLA_NOPE = 128
MLA_ROPE = 64
MLA_V = 128
MLA_QW = MLA_NOPE + 2 * MLA_ROPE
MLA_STRIP = 1024
MLA_CHAIN = 512
MLA_HEAD_GROUP = 2
ROPE_THETA = 10000.0
LOG2_E = 1.4426950408889634
NORM_EPS = 1e-6
N_MOD = 6
MOD_SHIFT_MIX, MOD_SCALE_MIX, MOD_GATE_MIX, MOD_SHIFT_FFN, MOD_SCALE_FFN, MOD_GATE_FFN = range(N_MOD)

V7X_LANES = 128
V7X_VMEM_LIMIT_BYTES = 56 * 1024 * 1024

MASK_VALUE = -1e30


def _params(*semantics):
    return pltpu.CompilerParams(dimension_semantics=semantics, vmem_limit_bytes=V7X_VMEM_LIMIT_BYTES)


def _dot(a, b):
    return jnp.dot(a, b, preferred_element_type=F32)


def _dot_nt(a, b):
    return lax.dot_general(a, b, (((1,), (1,)), ((), ())), preferred_element_type=F32)


def _dot_tn(a, b):
    return lax.dot_general(a, b, (((0,), (0,)), ((), ())), preferred_element_type=F32)


def _rms(x):
    return x * lax.rsqrt(jnp.mean(x * x, axis=-1, keepdims=True) + NORM_EPS)


def _modnorm(x, gain, scale, shift):
    return _rms(x) * (gain * (1.0 + scale)) + shift


def _tile(n, want):
    t = min(n, want)
    assert n % t == 0, (n, want)
    return t


def _ada_kernel(c_ref, w_ref, b_ref, o_ref):
    c = c_ref[...]
    act = (c * jax.nn.sigmoid(c)).astype(BF16)
    o_ref[...] = _dot(act, w_ref[...].astype(BF16)) + b_ref[...]


def _ada(c, w, b):
    bsz, d = c.shape
    n = w.shape[1]
    tn = _tile(n, 1024)
    return pl.pallas_call(
        _ada_kernel,
        grid=(n // tn,),
        in_specs=[pl.BlockSpec((bsz, d), lambda j: (0, 0)),
                  pl.BlockSpec((d, tn), lambda j: (0, j)),
                  pl.BlockSpec((1, tn), lambda j: (0, j))],
        out_specs=pl.BlockSpec((bsz, tn), lambda j: (0, j)),
        out_shape=jax.ShapeDtypeStruct((bsz, n), F32),
        compiler_params=_params("arbitrary"),
        name="ada",
    )(c, w, b.reshape(1, n))


def _rope_tab_kernel(pos_ref, f_ref, cos_ref, sin_ref):
    ang = pos_ref[...].astype(F32) * f_ref[...]
    cos_ref[...] = jnp.cos(ang)
    sin_ref[...] = jnp.sin(ang)


def _rope_tab(positions):
    t = positions.size
    half = MLA_ROPE // 2
    inv_freq = ROPE_THETA ** (-jnp.arange(0, MLA_ROPE, 2, dtype=F32) / MLA_ROPE)
    tt = _tile(t, 4096)
    spec = pl.BlockSpec((half, tt), lambda i: (0, i))
    return pl.pallas_call(
        _rope_tab_kernel,
        grid=(t // tt,),
        in_specs=[pl.BlockSpec((1, tt), lambda i: (0, i)), pl.BlockSpec((half, 1), lambda i: (0, 0))],
        out_specs=[spec, spec],
        out_shape=[jax.ShapeDtypeStruct((half, t), F32)] * 2,
        compiler_params=_params("arbitrary"),
        name="rope_tab",
    )(positions.reshape(1, t), inv_freq.reshape(half, 1))


def _inproj_kernel(x_ref, mod_ref, g_ref, w_ref, o_ref, ha_ref, hb_ref, *, chunks):
    i, j = pl.program_id(0), pl.program_id(1)
    tm = x_ref.shape[0]
    rows_per = tm // chunks

    def norm_rows(h_ref, rows):
        h = _modnorm(x_ref[rows, :], g_ref[...],
                     mod_ref[0, MOD_SCALE_MIX:MOD_SCALE_MIX + 1, :], mod_ref[0, MOD_SHIFT_MIX:MOD_SHIFT_MIX + 1, :])
        h_ref[rows, :] = h.astype(BF16)

    @pl.when((i == 0) & (j == 0))
    def _():
        norm_rows(ha_ref, slice(None))

    def step(h_cur, h_next):
        chunk = jnp.minimum(jnp.maximum(j - 1, 0), chunks - 1)
        norm_rows(h_next, pl.ds(pl.multiple_of(chunk * rows_per, rows_per), rows_per))
        o_ref[...] = _dot(h_cur[...], w_ref[0]).astype(o_ref.dtype)

    @pl.when(i % 2 == 0)
    def _():
        step(ha_ref, hb_ref)

    @pl.when(i % 2 == 1)
    def _():
        step(hb_ref, ha_ref)


def _inproj(x2, mod3, gain, w, seq):
    t, d = x2.shape
    nj, _, tn = w.shape
    n = nj * tn
    tm = _tile(seq, 1024)
    nt = t // tm
    chunks = 8
    assert nj - 1 >= chunks and tm % (chunks * 16) == 0

    def ahead(i, j):
        return jnp.where((i == 0) & (j == 0), 0, jnp.minimum(i + 1, nt - 1))

    return pl.pallas_call(
        functools.partial(_inproj_kernel, chunks=chunks),
        grid=(nt, nj),
        in_specs=[pl.BlockSpec((tm, d), lambda i, j: (ahead(i, j), 0)),
                  pl.BlockSpec((1, N_MOD, d), lambda i, j: ((ahead(i, j) * tm) // seq, 0, 0)),
                  pl.BlockSpec((1, d), lambda i, j: (0, 0)),
                  pl.BlockSpec((1, d, tn), lambda i, j: (j, 0, 0))],
        out_specs=pl.BlockSpec((tm, tn), lambda i, j: (i, j)),
        out_shape=jax.ShapeDtypeStruct((t, n), BF16),
        scratch_shapes=[pltpu.VMEM((tm, d), BF16), pltpu.VMEM((tm, d), BF16)],
        compiler_params=_params("arbitrary", "arbitrary"),
        name="inproj",
    )(x2, mod3, gain.reshape(1, d), w)


def _mlaproj_kernel(lat_ref, cos_ref, sin_ref, gq_ref, gkv_ref, wqt_ref, wk_ref, wvt_ref, qt_ref, k_ref, vt_ref, *,
                    q_rank, kv_rank, scale):
    tm = lat_ref.shape[0]
    cos = cos_ref[...]
    sin = sin_ref[...]
    cs = jnp.concatenate([cos, cos], axis=0)
    sn = jnp.concatenate([-sin, sin], axis=0)

    cqn = _rms(lat_ref[:, 0:q_rank].astype(F32)) * gq_ref[...]
    ckvn = _rms(lat_ref[:, q_rank:q_rank + kv_rank].astype(F32)) * gkv_ref[...]
    cqn_t = cqn.T.astype(BF16)
    ckvn_t = ckvn.T.astype(BF16)
    ckvn = ckvn.astype(BF16)

    kr_t = lat_ref[:, q_rank + kv_rank:q_rank + kv_rank + V7X_LANES].astype(F32).T
    k_rope_t = kr_t[:MLA_ROPE] * cs + kr_t[MLA_ROPE:] * sn
    k_rope = jnp.concatenate([k_rope_t, jnp.zeros_like(k_rope_t)], axis=0).T.astype(BF16)

    for h in range(MLA_HEADS):
        qt = _dot(wqt_ref[h * MLA_QW:(h + 1) * MLA_QW, :], cqn_t)
        q_rope = qt[MLA_NOPE:MLA_NOPE + MLA_ROPE] * cs + qt[MLA_NOPE + MLA_ROPE:] * sn
        qt_ref[h * MLA_QW:h * MLA_QW + MLA_NOPE, :] = (qt[:MLA_NOPE] * scale).astype(BF16)
        qt_ref[h * MLA_QW + MLA_NOPE:h * MLA_QW + MLA_NOPE + MLA_ROPE, :] = (q_rope * scale).astype(BF16)
        qt_ref[h * MLA_QW + MLA_NOPE + MLA_ROPE:(h + 1) * MLA_QW, :] = jnp.zeros((MLA_ROPE, tm), BF16)
        vt_ref[h, 0] = _dot(wvt_ref[h * MLA_V:(h + 1) * MLA_V, :], ckvn_t).astype(BF16)
    for g in range(MLA_HEADS // 2):
        kk = _dot(ckvn, wk_ref[:, 2 * g * MLA_NOPE:2 * (g + 1) * MLA_NOPE]).astype(BF16)
        for e in range(2):
            h = 2 * g + e
            k_ref[:, h * MLA_QW:h * MLA_QW + MLA_NOPE] = kk[:, e * MLA_NOPE:(e + 1) * MLA_NOPE]
            k_ref[:, h * MLA_QW + MLA_NOPE:(h + 1) * MLA_QW] = k_rope


def _mlaproj(p, lat_block, lat_width, cos_t, sin_t, gq, gkv, wqt, wk, wvt):
    t = p.shape[0]
    q_rank, kv_rank = gq.shape[0], gkv.shape[0]
    tm = _tile(MLA_STRIP, 512)
    per_strip = MLA_STRIP // tm
    half = MLA_ROPE // 2
    scale = (MLA_NOPE + MLA_ROPE) ** -0.5 * LOG2_E
    kern = functools.partial(_mlaproj_kernel, q_rank=q_rank, kv_rank=kv_rank, scale=scale)
    return pl.pallas_call(
        kern,
        grid=(t // tm,),
        in_specs=[pl.BlockSpec((tm, lat_width), lambda i: (i, lat_block)),
                  pl.BlockSpec((half, tm), lambda i: (0, i)),
                  pl.BlockSpec((half, tm), lambda i: (0, i)),
                  pl.BlockSpec((1, q_rank), lambda i: (0, 0)),
                  pl.BlockSpec((1, kv_rank), lambda i: (0, 0)),
                  pl.BlockSpec(wqt.shape, lambda i: (0, 0)),
                  pl.BlockSpec(wk.shape, lambda i: (0, 0)),
                  pl.BlockSpec(wvt.shape, lambda i: (0, 0))],
        out_specs=[pl.BlockSpec((MLA_HEADS * MLA_QW, tm), lambda i: (0, i)),
                   pl.BlockSpec((tm, MLA_HEADS * MLA_QW), lambda i: (i, 0)),
                   pl.BlockSpec((MLA_HEADS, 1, MLA_V, tm), lambda i: (0, i // per_strip, 0, i % per_strip))],
        out_shape=[jax.ShapeDtypeStruct((MLA_HEADS * MLA_QW, t), BF16),
                   jax.ShapeDtypeStruct((t, MLA_HEADS * MLA_QW), BF16),
                   jax.ShapeDtypeStruct((MLA_HEADS, t // MLA_STRIP, MLA_V, MLA_STRIP), BF16)],
        compiler_params=_params("parallel"),
        name="mlaproj",
    )(p, cos_t, sin_t, gq.reshape(1, q_rank), gkv.reshape(1, kv_rank), wqt, wk, wvt)


def _gla_kernel(q_ref, k_ref, v_ref, g_ref, gk_ref, gkw_ref, gkb_ref, on_ref, o_ref, st_ref, *, dk, dv):
    c = q_ref.shape[0]

    @pl.when(pl.program_id(1) == 0)
    def _():
        st_ref[...] = jnp.zeros_like(st_ref)

    z = _dot(gk_ref[...], gkw_ref[...]) + gkb_ref[...]
    log_a = jax.nn.log_sigmoid(z) * (LOG2_E / GLA_GATE_NORMALIZER)

    halves = [1 << e for e in range(c.bit_length() - 1)]
    row = lax.broadcasted_iota(jnp.int32, (c, dk), 0)
    ai = lax.broadcasted_iota(jnp.int32, (c, c), 0)
    aj = lax.broadcasted_iota(jnp.int32, (c, c), 1)
    diff = ai ^ aj
    causal_diff = jnp.where(ai > aj, diff, 0)
    level_mask = {s: (causal_diff >> (s.bit_length() - 1)) == 1 for s in halves}
    diag_mask = ai == aj

    for h in range(GLA_HEADS):
        g = log_a[:, h * dk:(h + 1) * dk]
        prefix = g
        total = g
        decay = {}
        for s in halves:
            odd = (row & s) != 0
            sibling = jnp.where(odd, pltpu.roll(total, s, axis=0), pltpu.roll(total, c - s, axis=0))
            decay[s] = jnp.exp2(jnp.where(odd, prefix, total - prefix))
            prefix = prefix + jnp.where(odd, sibling, 0.0)
            total = total + sibling
        qh = q_ref[:, h * dk:(h + 1) * dk].astype(F32) * (dk ** -0.5)
        kh = k_ref[:, h * dk:(h + 1) * dk].astype(F32)
        vh = v_ref[:, h * dv:(h + 1) * dv]
        att = jnp.where(diag_mask, _dot_nt(qh.astype(BF16), kh.astype(BF16)), 0.0)
        for s in halves:
            a = _dot_nt((qh * decay[s]).astype(BF16), (kh * decay[s]).astype(BF16))
            att = jnp.where(level_mask[s], a, att)
        st = st_ref[h]
        o = _dot_nt((qh * jnp.exp2(prefix)).astype(BF16), st.astype(BF16)) + _dot(att.astype(BF16), vh)
        k_dec = (kh * jnp.exp2(total - prefix)).astype(BF16)
        st_ref[h] = st * jnp.exp2(total[0:1, :]) + _dot_tn(vh, k_dec)
        gate = g_ref[:, h * dv:(h + 1) * dv].astype(F32)
        o_ref[:, h * dv:(h + 1) * dv] = (_rms(o) * on_ref[...] * (gate * jax.nn.sigmoid(gate))).astype(o_ref.dtype)


def _gla(p, gk_block, gkw, gkb, onorm_g, bsz, seq, dk, dv):
    t = p.shape[0]
    c = _tile(seq, 128)
    nc = seq // c
    hk, hv = GLA_HEADS * dk, GLA_HEADS * dv
    kern = functools.partial(_gla_kernel, dk=dk, dv=dv)
    row = lambda b, i: b * nc + i
    return pl.pallas_call(
        kern,
        grid=(bsz, nc),
        in_specs=[pl.BlockSpec((c, hk), lambda b, i: (row(b, i), 0)),
                  pl.BlockSpec((c, hk), lambda b, i: (row(b, i), 1)),
                  pl.BlockSpec((c, hv), lambda b, i: (row(b, i), (2 * hk) // hv)),
                  pl.BlockSpec((c, hv), lambda b, i: (row(b, i), (2 * hk) // hv + 1)),
                  pl.BlockSpec((c, V7X_LANES), lambda b, i: (row(b, i), gk_block)),
                  pl.BlockSpec(gkw.shape, lambda b, i: (0, 0)),
                  pl.BlockSpec((1, hk), lambda b, i: (0, 0)),
                  pl.BlockSpec((1, dv), lambda b, i: (0, 0))],
        out_specs=pl.BlockSpec((c, hv), lambda b, i: (row(b, i), 0)),
        out_shape=jax.ShapeDtypeStruct((t, hv), BF16),
        scratch_shapes=[pltpu.VMEM((GLA_HEADS, dv, dk), F32)],
        compiler_params=_params("parallel", "arbitrary"),
        name="gla",
    )(p, p, p, p, p, gkw, gkb.reshape(1, hk), onorm_g.reshape(1, dv))


def _flash_kernel(qt_ref, k_ref, vt_ref, o_ref, acc_ref, s0_ref):
    strip, cw = MLA_STRIP, MLA_CHAIN
    nch = strip // cw
    heads = range(acc_ref.shape[0])
    i = pl.program_id(2)
    acc_ref[...] = jnp.zeros_like(acc_ref)

    def scores(h, a, start, rows):
        return _dot(k_ref[pl.ds(start, rows), h * MLA_QW:(h + 1) * MLA_QW],
                    qt_ref[h * MLA_QW:(h + 1) * MLA_QW, a * cw:(a + 1) * cw])

    def col_max(s):
        return jnp.max(s, axis=0, keepdims=True)

    def update(h, a, v_t, s, s_max, m, l):
        m_new = jnp.maximum(m, s_max)
        alpha = jnp.exp2(m - m_new)
        p = jnp.exp2(s - m_new)
        l = alpha * l + jnp.sum(p, axis=0, keepdims=True)
        acc_ref[h, a] = alpha * acc_ref[h, a] + _dot(v_t, p.astype(BF16))
        return m_new, l

    def body(j, carry):
        c0, c1, s0_max = carry
        start = pl.multiple_of(j * strip, strip)
        s1 = [scores(h, 1, start, strip) for h in heads]
        c0 = tuple(update(h, 0, vt_ref[h, j], s0_ref[h], s0_max[h], *c0[h]) for h in heads)
        s0 = [scores(h, 0, start + strip, strip) for h in heads]
        for h in heads:
            s0_ref[h] = s0[h]
        c1 = tuple(update(h, 1, vt_ref[h, j], s1[h], col_max(s1[h]), *c1[h]) for h in heads)
        return c0, c1, tuple(col_max(s0[h]) for h in heads)

    assert nch == 2
    s0 = [scores(h, 0, 0, strip) for h in heads]
    for h in heads:
        s0_ref[h] = s0[h]
    stats = tuple((jnp.full((1, cw), MASK_VALUE, F32), jnp.zeros((1, cw), F32)) for _ in heads)
    carry = lax.fori_loop(0, i, body, (stats, stats, tuple(col_max(s0[h]) for h in heads)))
    start = pl.multiple_of(i * strip, strip)
    s = [[s0_ref[h, :cw, :] for h in heads], [scores(h, 1, start, strip) for h in heads]]
    for a in range(nch):
        rows = (a + 1) * cw
        key = lax.broadcasted_iota(jnp.int32, (rows, cw), 0)
        query = lax.broadcasted_iota(jnp.int32, (rows, cw), 1) + a * cw
        for h in heads:
            sm = jnp.where(key <= query, s[a][h], MASK_VALUE)
            _, l = update(h, a, vt_ref[h, i][:, :rows], sm, col_max(sm), *carry[a][h])
            o_t = acc_ref[h, a] / l
            o_ref[a * cw:(a + 1) * cw, h * MLA_V:(h + 1) * MLA_V] = o_t.T.astype(o_ref.dtype)


def _flash(qt, k, vt, bsz, seq):
    t = k.shape[0]
    assert seq % MLA_STRIP == 0
    nq = seq // MLA_STRIP
    hp = MLA_HEAD_GROUP
    return pl.pallas_call(
        _flash_kernel,
        grid=(bsz, MLA_HEADS // hp, nq),
        in_specs=[pl.BlockSpec((hp * MLA_QW, MLA_STRIP), lambda b, g, i: (g, b * nq + i)),
                  pl.BlockSpec((seq, hp * MLA_QW), lambda b, g, i: (b, g)),
                  pl.BlockSpec((hp, nq, MLA_V, MLA_STRIP), lambda b, g, i: (g, b, 0, 0))],
        out_specs=pl.BlockSpec((MLA_STRIP, hp * MLA_V), lambda b, g, i: (b * nq + i, g)),
        out_shape=jax.ShapeDtypeStruct((t, MLA_HEADS * MLA_V), BF16),
        scratch_shapes=[pltpu.VMEM((hp, MLA_STRIP // MLA_CHAIN, MLA_V, MLA_CHAIN), F32),
                        pltpu.VMEM((hp, MLA_STRIP, MLA_CHAIN), F32)],
        compiler_params=_params("parallel", "parallel", "arbitrary"),
        name="flash",
    )(qt, k, vt)


def _mixout_kernel(og_ref, om_ref, ga_ref, gb_ref, x_ref, mod_ref, wg_ref, wm_ref, wo_ref, o_ref):
    yg = _dot(og_ref[...], wg_ref[...])
    ym = _dot(om_ref[...], wm_ref[...])
    ga = jax.nn.sigmoid(ga_ref[...].astype(F32))
    gb = jax.nn.sigmoid(gb_ref[...].astype(F32))
    merged = (ga * yg + gb * ym).astype(BF16)
    o_ref[...] = x_ref[...] + mod_ref[0, MOD_GATE_MIX:MOD_GATE_MIX + 1, :] * _dot(merged, wo_ref[...])


def _mixout(o_gla, o_mla, p, gate_col, x2, mod3, w_gla, w_mla, w_out, seq):
    t, d = x2.shape
    tm = _tile(seq, 256)
    assert gate_col % d == 0
    row = lambda i: (i, 0)
    resident = lambda w: pl.BlockSpec(w.shape, lambda i: (0, 0), pipeline_mode=pl.Buffered(1))
    return pl.pallas_call(
        _mixout_kernel,
        grid=(t // tm,),
        in_specs=[pl.BlockSpec((tm, o_gla.shape[1]), row),
                  pl.BlockSpec((tm, o_mla.shape[1]), row),
                  pl.BlockSpec((tm, d), lambda i: (i, gate_col // d)),
                  pl.BlockSpec((tm, d), lambda i: (i, gate_col // d + 1)),
                  pl.BlockSpec((tm, d), row),
                  pl.BlockSpec((1, N_MOD, d), lambda i: ((i * tm) // seq, 0, 0)),
                  resident(w_gla), resident(w_mla), resident(w_out)],
        out_specs=pl.BlockSpec((tm, d), row),
        out_shape=jax.ShapeDtypeStruct((t, d), F32),
        compiler_params=_params("parallel"),
        name="mixout",
    )(o_gla, o_mla, p, p, x2, mod3, w_gla, w_mla, w_out)


def _ffn_kernel(x_ref, xn_ref, mod_ref, modn_ref, g_ref, wgu_ref, wd_ref, fg_ref, o_ref, ha_ref, hb_ref, acc_ref,
                *, final_norm, chunks):
    i, f = pl.program_id(0), pl.program_id(1)
    tm = x_ref.shape[0]
    rows_per = tm // chunks

    def norm_rows(h_ref, src_ref, m_ref, rows):
        h = _modnorm(src_ref[rows, :], g_ref[...],
                     m_ref[0, MOD_SCALE_FFN:MOD_SCALE_FFN + 1, :], m_ref[0, MOD_SHIFT_FFN:MOD_SHIFT_FFN + 1, :])
        h_ref[rows, :] = h.astype(BF16)

    @pl.when((i == 0) & (f == 0))
    def _():
        norm_rows(ha_ref, x_ref, mod_ref, slice(None))

    @pl.when(f == 0)
    def _():
        acc_ref[...] = jnp.zeros_like(acc_ref)

    def step(h_cur, h_next):
        chunk = jnp.minimum(f, chunks - 1)
        norm_rows(h_next, xn_ref, modn_ref, pl.ds(pl.multiple_of(chunk * rows_per, rows_per), rows_per))
        tf = wd_ref.shape[0]
        gate_up = _dot(h_cur[...], wgu_ref[0])
        gate, up = gate_up[:, :tf], gate_up[:, tf:]
        act = (gate * jax.nn.sigmoid(gate) * up).astype(BF16)
        acc_ref[...] += _dot(act, wd_ref[...])

    @pl.when(i % 2 == 0)
    def _():
        step(ha_ref, hb_ref)

    @pl.when(i % 2 == 1)
    def _():
        step(hb_ref, ha_ref)

    @pl.when(f == pl.num_programs(1) - 1)
    def _():
        y = x_ref[...] + mod_ref[0, MOD_GATE_FFN:MOD_GATE_FFN + 1, :] * acc_ref[...]
        if final_norm:
            y = _rms(y) * fg_ref[...]
        o_ref[...] = y


FFN_HIDDEN_BLOCK = 512


def _pack_ffn_w_in(w):
    d, hidden = w.shape[0], w.shape[1] // 2
    tf = _tile(hidden, FFN_HIDDEN_BLOCK)
    return w.reshape(d, 2, hidden // tf, tf).transpose(2, 0, 1, 3).reshape(hidden // tf, d, 2 * tf).astype(BF16)


def _ffn(x1, mod3, gain, w_gu, w_down, final_g, seq, final_norm):
    t, d = x1.shape
    hidden = w_down.shape[0]
    tm = _tile(seq, 512)
    nf, tf = w_gu.shape[0], w_gu.shape[2] // 2
    nt = t // tm
    chunks = 8
    assert nf >= chunks and tm % (chunks * 16) == 0
    nxt = lambda i: jnp.minimum(i + 1, nt - 1)
    kern = functools.partial(_ffn_kernel, final_norm=final_norm, chunks=chunks)
    return pl.pallas_call(
        kern,
        grid=(nt, nf),
        in_specs=[pl.BlockSpec((tm, d), lambda i, f: (i, 0)),
                  pl.BlockSpec((tm, d), lambda i, f: (nxt(i), 0)),
                  pl.BlockSpec((1, N_MOD, d), lambda i, f: ((i * tm) // seq, 0, 0)),
                  pl.BlockSpec((1, N_MOD, d), lambda i, f: ((nxt(i) * tm) // seq, 0, 0)),
                  pl.BlockSpec((1, d), lambda i, f: (0, 0)),
                  pl.BlockSpec((1, d, 2 * tf), lambda i, f: (f, 0, 0)),
                  pl.BlockSpec((tf, d), lambda i, f: (f, 0)),
                  pl.BlockSpec((1, d), lambda i, f: (0, 0))],
        out_specs=pl.BlockSpec((tm, d), lambda i, f: (i, 0)),
        out_shape=jax.ShapeDtypeStruct((t, d), F32),
        scratch_shapes=[pltpu.VMEM((tm, d), BF16), pltpu.VMEM((tm, d), BF16), pltpu.VMEM((tm, d), F32)],
        compiler_params=_params("arbitrary", "arbitrary"),
        name="ffn",
    )(x1, x1, mod3, mod3, gain.reshape(1, d), w_gu, w_down, final_g.reshape(1, d))


def _pack_w_in_kernel(wt_ref, o_ref, *, moves, kr_row, gk_row, tail_col):
    tn = o_ref.shape[2]

    def put(dst, val_t):
        done = 0
        while done < val_t.shape[0]:
            blk, off = divmod(dst + done, tn)
            n = min(val_t.shape[0] - done, tn - off)
            o_ref[blk, :, off:off + n] = val_t[done:done + n].T.astype(BF16)
            done += n

    for dst, src, width in moves:
        put(dst, wt_ref[src:src + width, :])
    half = MLA_ROPE // 2
    kr = wt_ref[kr_row:kr_row + MLA_ROPE, :]
    gk = wt_ref[gk_row:gk_row + GLA_GATE_RANK, :]
    pad = jnp.zeros((V7X_LANES - GLA_GATE_RANK, kr.shape[1]), F32)
    put(tail_col, jnp.concatenate([kr, kr[half:], kr[:half], gk, pad], axis=0))


INPROJ_COL_BLOCK = 1280


def _pack_w_in(w, d, dk_all, dv_all, q_rank, kv_rank):
    widths = (dk_all, dk_all, dv_all, dv_all, GLA_GATE_RANK, q_rank, kv_rank, MLA_ROPE, d, d)
    offs = [0]
    for wd in widths:
        offs.append(offs[-1] + wd)
    gla_w = offs[4]
    moves = ((0, 0, gla_w), (gla_w, offs[8], 2 * d), (gla_w + 2 * d, offs[5], q_rank + kv_rank))
    tail_col = gla_w + 2 * d + q_rank + kv_rank
    n_out = tail_col + 2 * V7X_LANES
    tn = INPROJ_COL_BLOCK
    assert all(dst % V7X_LANES == 0 for dst, _, _ in moves) and tail_col % V7X_LANES == 0 and n_out % tn == 0
    rows = w.shape[0]
    tr = _tile(rows, 128)
    kern = functools.partial(_pack_w_in_kernel, moves=moves, kr_row=offs[7], gk_row=offs[4], tail_col=tail_col)
    return pl.pallas_call(
        kern,
        grid=(rows // tr,),
        in_specs=[pl.BlockSpec((w.shape[1], tr), lambda i: (0, i))],
        out_specs=pl.BlockSpec((n_out // tn, tr, tn), lambda i: (0, i, 0)),
        out_shape=jax.ShapeDtypeStruct((n_out // tn, rows, tn), BF16),
        compiler_params=_params("parallel"),
        name="pack_w_in",
    )(w.T)


def _pack_wuq_t(w):
    r = w.shape[0]
    half = MLA_ROPE // 2
    w3 = w.reshape(r, MLA_HEADS, MLA_NOPE + MLA_ROPE)
    rope = w3[:, :, MLA_NOPE:]
    rot = jnp.concatenate([rope[:, :, half:], rope[:, :, :half]], axis=2)
    return jnp.concatenate([w3[:, :, :MLA_NOPE], rope, rot], axis=2).reshape(r, -1).T.astype(BF16)


def _pack_wukv(w):
    r = w.shape[0]
    w3 = w.reshape(r, MLA_HEADS, MLA_NOPE + MLA_V)
    wk = w3[:, :, :MLA_NOPE].reshape(r, -1).astype(BF16)
    wvt = w3[:, :, MLA_NOPE:].reshape(r, -1).T.astype(BF16)
    return wk, wvt


def kernel(x, c, positions, ada_w, ada_b, norm_mix_g, w_in, gla_gk_w, gla_gk_b, gla_onorm_g, gla_wo, mla_q_norm_g, mla_wuq, mla_kv_norm_g, mla_wukv, mla_wo, w_out, norm_ffn_g, ffn_w_in, ffn_w_down, final_norm_g):
    bsz, seq, d = x.shape
    depth = w_in.shape[0]
    dk_all = gla_gk_w.shape[2]
    dv_all = gla_wo.shape[1]
    dk, dv = dk_all // GLA_HEADS, dv_all // GLA_HEADS
    q_rank, kv_rank = mla_wuq.shape[1], mla_wukv.shape[1]
    lat_col = 2 * dk_all + 2 * dv_all + 2 * d
    lat_width = q_rank + kv_rank + 2 * V7X_LANES
    gate_col = 2 * dk_all + 2 * dv_all
    assert lat_col % lat_width == 0 and (lat_col + lat_width - V7X_LANES) % V7X_LANES == 0

    x2 = x.reshape(bsz * seq, d)
    cos_t, sin_t = _rope_tab(positions)
    for l in range(depth):
        mod3 = _ada(c, ada_w[l], ada_b[l]).reshape(bsz, N_MOD, d)
        w_all = _pack_w_in(w_in[l], d, dk_all, dv_all, q_rank, kv_rank)
        p = _inproj(x2, mod3, norm_mix_g[l], w_all, seq)
        wk, wvt = _pack_wukv(mla_wukv[l])
        qt, k, vt = _mlaproj(p, lat_col // lat_width, lat_width, cos_t, sin_t, mla_q_norm_g[l], mla_kv_norm_g[l],
                             _pack_wuq_t(mla_wuq[l]), wk, wvt)
        gkw = jnp.zeros((V7X_LANES, dk_all), BF16).at[:GLA_GATE_RANK].set(gla_gk_w[l].astype(BF16))
        o_gla = _gla(p, (lat_col + lat_width - V7X_LANES) // V7X_LANES, gkw, gla_gk_b[l], gla_onorm_g[l],
                     bsz, seq, dk, dv)
        o_mla = _flash(qt, k, vt, bsz, seq)
        x1 = _mixout(o_gla, o_mla, p, gate_col, x2, mod3, gla_wo[l].astype(BF16), mla_wo[l].astype(BF16),
                     w_out[l].astype(BF16), seq)
        x2 = _ffn(x1, mod3, norm_ffn_g[l], _pack_ffn_w_in(ffn_w_in[l]), ffn_w_down[l].astype(BF16), final_norm_g,
                  seq, final_norm=(l == depth - 1))
    return x2.reshape(bsz, seq, d)
```

```python
import functools

import jax
import jax.numpy as jnp
from jax import lax
from jax.experimental import pallas as pl
from jax.experimental.pallas import tpu as pltpu

F32 = jnp.float32
BF16 = jnp.bfloat16

GLA_HEADS = 4
GLA_GATE_RANK = 16
GLA_GATE_NORMALIZER = 16.0
MLA_HEADS = 16
MLA_NOPE = 128
MLA_ROPE = 64
MLA_V = 128
MLA_QW = MLA_NOPE + 2 * MLA_ROPE
MLA_QBLOCK = 1024
MLA_STRIP = 512
MLA_CHAIN = 512
MLA_HEAD_GROUP = 2
ROPE_THETA = 10000.0
LOG2_E = 1.4426950408889634
NORM_EPS = 1e-6
N_MOD = 6
MOD_SHIFT_MIX, MOD_SCALE_MIX, MOD_GATE_MIX, MOD_SHIFT_FFN, MOD_SCALE_FFN, MOD_GATE_FFN = range(N_MOD)

V7X_LANES = 128
V7X_VMEM_LIMIT_BYTES = 56 * 1024 * 1024

MASK_VALUE = -1e30


def _params(*semantics):
    return pltpu.CompilerParams(dimension_semantics=semantics, vmem_limit_bytes=V7X_VMEM_LIMIT_BYTES)


def _dot(a, b):
    return jnp.dot(a, b, preferred_element_type=F32)


def _dot_nt(a, b):
    return lax.dot_general(a, b, (((1,), (1,)), ((), ())), preferred_element_type=F32)


def _dot_tn(a, b):
    return lax.dot_general(a, b, (((0,), (0,)), ((), ())), preferred_element_type=F32)


def _rms(x):
    return x * lax.rsqrt(jnp.mean(x * x, axis=-1, keepdims=True) + NORM_EPS)


def _modnorm(x, gain, scale, shift):
    return _rms(x) * (gain * (1.0 + scale)) + shift


def _tile(n, want):
    t = min(n, want)
    assert n % t == 0, (n, want)
    return t


def _ada_kernel(c_ref, w_ref, b_ref, o_ref):
    c = c_ref[...]
    act = (c * jax.nn.sigmoid(c)).astype(BF16)
    o_ref[...] = _dot(act, w_ref[...].astype(BF16)) + b_ref[...]


def _ada(c, w, b):
    bsz, d = c.shape
    n = w.shape[1]
    tn = _tile(n, 1024)
    return pl.pallas_call(
        _ada_kernel,
        grid=(n // tn,),
        in_specs=[pl.BlockSpec((bsz, d), lambda j: (0, 0)),
                  pl.BlockSpec((d, tn), lambda j: (0, j)),
                  pl.BlockSpec((1, tn), lambda j: (0, j))],
        out_specs=pl.BlockSpec((bsz, tn), lambda j: (0, j)),
        out_shape=jax.ShapeDtypeStruct((bsz, n), F32),
        compiler_params=_params("arbitrary"),
        name="ada",
    )(c, w, b.reshape(1, n))


def _rope_tab_kernel(pos_ref, f_ref, cos_ref, sin_ref):
    ang = pos_ref[...].astype(F32) * f_ref[...]
    cos_ref[...] = jnp.cos(ang)
    sin_ref[...] = jnp.sin(ang)


def _rope_tab(positions):
    t = positions.size
    half = MLA_ROPE // 2
    inv_freq = ROPE_THETA ** (-jnp.arange(0, MLA_ROPE, 2, dtype=F32) / MLA_ROPE)
    tt = _tile(t, 4096)
    spec = pl.BlockSpec((half, tt), lambda i: (0, i))
    return pl.pallas_call(
        _rope_tab_kernel,
        grid=(t // tt,),
        in_specs=[pl.BlockSpec((1, tt), lambda i: (0, i)), pl.BlockSpec((half, 1), lambda i: (0, 0))],
        out_specs=[spec, spec],
        out_shape=[jax.ShapeDtypeStruct((half, t), F32)] * 2,
        compiler_params=_params("arbitrary"),
        name="rope_tab",
    )(positions.reshape(1, t), inv_freq.reshape(half, 1))


def _inproj_kernel(x_ref, mod_ref, g_ref, w_ref, o_ref, h_ref):
    @pl.when(pl.program_id(1) == 0)
    def _():
        h = _modnorm(x_ref[...], g_ref[...],
                     mod_ref[0, MOD_SCALE_MIX:MOD_SCALE_MIX + 1, :], mod_ref[0, MOD_SHIFT_MIX:MOD_SHIFT_MIX + 1, :])
        h_ref[...] = h.astype(BF16)

    o_ref[...] = _dot(h_ref[...], w_ref[...]).astype(o_ref.dtype)


def _inproj(x2, mod3, gain, w, seq):
    t, d = x2.shape
    n = w.shape[1]
    tm = _tile(seq, 1024)
    tn = _tile(n, 1280)
    return pl.pallas_call(
        _inproj_kernel,
        grid=(t // tm, n // tn),
        in_specs=[pl.BlockSpec((tm, d), lambda i, j: (i, 0)),
                  pl.BlockSpec((1, N_MOD, d), lambda i, j: ((i * tm) // seq, 0, 0)),
                  pl.BlockSpec((1, d), lambda i, j: (0, 0)),
                  pl.BlockSpec((d, tn), lambda i, j: (0, j))],
        out_specs=pl.BlockSpec((tm, tn), lambda i, j: (i, j)),
        out_shape=jax.ShapeDtypeStruct((t, n), BF16),
        scratch_shapes=[pltpu.VMEM((tm, d), BF16)],
        compiler_params=_params("parallel", "arbitrary"),
        name="inproj",
    )(x2, mod3, gain.reshape(1, d), w)


def _mlaproj_kernel(lat_ref, cos_ref, sin_ref, gq_ref, gkv_ref, wqt_ref, wk_ref, wvt_ref, qt_ref, k_ref, vt_ref, *,
                    q_rank, kv_rank, scale):
    tm = lat_ref.shape[0]
    cos = cos_ref[...]
    sin = sin_ref[...]
    cs = jnp.concatenate([cos, cos], axis=0)
    sn = jnp.concatenate([-sin, sin], axis=0)

    cqn = _rms(lat_ref[:, 0:q_rank].astype(F32)) * gq_ref[...]
    ckvn = _rms(lat_ref[:, q_rank:q_rank + kv_rank].astype(F32)) * gkv_ref[...]
    cqn_t = cqn.T.astype(BF16)
    ckvn_t = ckvn.T.astype(BF16)
    ckvn = ckvn.astype(BF16)

    kr_t = lat_ref[:, q_rank + kv_rank:q_rank + kv_rank + V7X_LANES].astype(F32).T
    k_rope_t = kr_t[:MLA_ROPE] * cs + kr_t[MLA_ROPE:] * sn
    k_rope = jnp.concatenate([k_rope_t, jnp.zeros_like(k_rope_t)], axis=0).T.astype(BF16)

    for h in range(MLA_HEADS):
        qt = _dot(wqt_ref[h * MLA_QW:(h + 1) * MLA_QW, :], cqn_t)
        q_rope = qt[MLA_NOPE:MLA_NOPE + MLA_ROPE] * cs + qt[MLA_NOPE + MLA_ROPE:] * sn
        qt_ref[h * MLA_QW:h * MLA_QW + MLA_NOPE, :] = (qt[:MLA_NOPE] * scale).astype(BF16)
        qt_ref[h * MLA_QW + MLA_NOPE:h * MLA_QW + MLA_NOPE + MLA_ROPE, :] = (q_rope * scale).astype(BF16)
        qt_ref[h * MLA_QW + MLA_NOPE + MLA_ROPE:(h + 1) * MLA_QW, :] = jnp.zeros((MLA_ROPE, tm), BF16)
        vt_ref[h, 0] = _dot(wvt_ref[h * MLA_V:(h + 1) * MLA_V, :], ckvn_t).astype(BF16)
    for g in range(MLA_HEADS // 2):
        kk = _dot(ckvn, wk_ref[:, 2 * g * MLA_NOPE:2 * (g + 1) * MLA_NOPE]).astype(BF16)
        for e in range(2):
            h = 2 * g + e
            k_ref[:, h * MLA_QW:h * MLA_QW + MLA_NOPE] = kk[:, e * MLA_NOPE:(e + 1) * MLA_NOPE]
            k_ref[:, h * MLA_QW + MLA_NOPE:(h + 1) * MLA_QW] = k_rope


def _mlaproj(p, lat_block, lat_width, cos_t, sin_t, gq, gkv, wqt, wk, wvt):
    t = p.shape[0]
    q_rank, kv_rank = gq.shape[0], gkv.shape[0]
    tm = _tile(MLA_STRIP, 512)
    per_strip = MLA_STRIP // tm
    half = MLA_ROPE // 2
    scale = (MLA_NOPE + MLA_ROPE) ** -0.5 * LOG2_E
    kern = functools.partial(_mlaproj_kernel, q_rank=q_rank, kv_rank=kv_rank, scale=scale)
    return pl.pallas_call(
        kern,
        grid=(t // tm,),
        in_specs=[pl.BlockSpec((tm, lat_width), lambda i: (i, lat_block)),
                  pl.BlockSpec((half, tm), lambda i: (0, i)),
                  pl.BlockSpec((half, tm), lambda i: (0, i)),
                  pl.BlockSpec((1, q_rank), lambda i: (0, 0)),
                  pl.BlockSpec((1, kv_rank), lambda i: (0, 0)),
                  pl.BlockSpec(wqt.shape, lambda i: (0, 0)),
                  pl.BlockSpec(wk.shape, lambda i: (0, 0)),
                  pl.BlockSpec(wvt.shape, lambda i: (0, 0))],
        out_specs=[pl.BlockSpec((MLA_HEADS * MLA_QW, tm), lambda i: (0, i)),
                   pl.BlockSpec((tm, MLA_HEADS * MLA_QW), lambda i: (i, 0)),
                   pl.BlockSpec((MLA_HEADS, 1, MLA_V, tm), lambda i: (0, i // per_strip, 0, i % per_strip))],
        out_shape=[jax.ShapeDtypeStruct((MLA_HEADS * MLA_QW, t), BF16),
                   jax.ShapeDtypeStruct((t, MLA_HEADS * MLA_QW), BF16),
                   jax.ShapeDtypeStruct((MLA_HEADS, t // MLA_STRIP, MLA_V, MLA_STRIP), BF16)],
        compiler_params=_params("parallel"),
        name="mlaproj",
    )(p, cos_t, sin_t, gq.reshape(1, q_rank), gkv.reshape(1, kv_rank), wqt, wk, wvt)


def _gla_kernel(q_ref, k_ref, v_ref, g_ref, gk_ref, gkw_ref, gkb_ref, on_ref, o_ref, st_ref, *, dk, dv):
    c = q_ref.shape[0]

    @pl.when(pl.program_id(1) == 0)
    def _():
        st_ref[...] = jnp.zeros_like(st_ref)

    z = _dot(gk_ref[...], gkw_ref[...]) + gkb_ref[...]
    log_a = jax.nn.log_sigmoid(z) * (LOG2_E / GLA_GATE_NORMALIZER)

    halves = [1 << e for e in range(c.bit_length() - 1)]
    row = lax.broadcasted_iota(jnp.int32, (c, dk), 0)
    ai = lax.broadcasted_iota(jnp.int32, (c, c), 0)
    aj = lax.broadcasted_iota(jnp.int32, (c, c), 1)
    diff = ai ^ aj
    causal_diff = jnp.where(ai > aj, diff, 0)
    level_mask = {s: (causal_diff >> (s.bit_length() - 1)) == 1 for s in halves}
    diag_mask = ai == aj

    for h in range(GLA_HEADS):
        g = log_a[:, h * dk:(h + 1) * dk]
        prefix = g
        total = g
        decay = {}
        for s in halves:
            odd = (row & s) != 0
            sibling = jnp.where(odd, pltpu.roll(total, s, axis=0), pltpu.roll(total, c - s, axis=0))
            decay[s] = jnp.exp2(jnp.where(odd, prefix, total - prefix))
            prefix = prefix + jnp.where(odd, sibling, 0.0)
            total = total + sibling
        qh = q_ref[:, h * dk:(h + 1) * dk].astype(F32) * (dk ** -0.5)
        kh = k_ref[:, h * dk:(h + 1) * dk].astype(F32)
        vh = v_ref[:, h * dv:(h + 1) * dv]
        att = jnp.where(diag_mask, _dot_nt(qh.astype(BF16), kh.astype(BF16)), 0.0)
        for s in halves:
            a = _dot_nt((qh * decay[s]).astype(BF16), (kh * decay[s]).astype(BF16))
            att = jnp.where(level_mask[s], a, att)
        st = st_ref[h]
        o = _dot_nt((qh * jnp.exp2(prefix)).astype(BF16), st.astype(BF16)) + _dot(att.astype(BF16), vh)
        k_dec = (kh * jnp.exp2(total - prefix)).astype(BF16)
        st_ref[h] = st * jnp.exp2(total[0:1, :]) + _dot_tn(vh, k_dec)
        gate = g_ref[:, h * dv:(h + 1) * dv].astype(F32)
        o_ref[:, h * dv:(h + 1) * dv] = (_rms(o) * on_ref[...] * (gate * jax.nn.sigmoid(gate))).astype(o_ref.dtype)


def _gla(p, gk_block, gkw, gkb, onorm_g, bsz, seq, dk, dv):
    t = p.shape[0]
    c = _tile(seq, 128)
    nc = seq // c
    hk, hv = GLA_HEADS * dk, GLA_HEADS * dv
    kern = functools.partial(_gla_kernel, dk=dk, dv=dv)
    row = lambda b, i: b * nc + i
    return pl.pallas_call(
        kern,
        grid=(bsz, nc),
        in_specs=[pl.BlockSpec((c, hk), lambda b, i: (row(b, i), 0)),
                  pl.BlockSpec((c, hk), lambda b, i: (row(b, i), 1)),
                  pl.BlockSpec((c, hv), lambda b, i: (row(b, i), (2 * hk) // hv)),
                  pl.BlockSpec((c, hv), lambda b, i: (row(b, i), (2 * hk) // hv + 1)),
                  pl.BlockSpec((c, V7X_LANES), lambda b, i: (row(b, i), gk_block)),
                  pl.BlockSpec(gkw.shape, lambda b, i: (0, 0)),
                  pl.BlockSpec((1, hk), lambda b, i: (0, 0)),
                  pl.BlockSpec((1, dv), lambda b, i: (0, 0))],
        out_specs=pl.BlockSpec((c, hv), lambda b, i: (row(b, i), 0)),
        out_shape=jax.ShapeDtypeStruct((t, hv), BF16),
        scratch_shapes=[pltpu.VMEM((GLA_HEADS, dv, dk), F32)],
        compiler_params=_params("parallel", "arbitrary"),
        name="gla",
    )(p, p, p, p, p, gkw, gkb.reshape(1, hk), onorm_g.reshape(1, dv))


def _flash_kernel(qt_ref, k_ref, vt_ref, o_ref, acc_ref, sa_ref, sb_ref):
    strip, cw = MLA_STRIP, MLA_CHAIN
    hp, nch = acc_ref.shape[0], acc_ref.shape[1]
    chains = [(h, a) for h in range(hp) for a in range(nch)]
    per_block = MLA_QBLOCK // strip
    assert per_block == 2 and nch == 2 and strip == cw
    i = pl.program_id(2)
    acc_ref[...] = jnp.zeros_like(acc_ref)

    def scores(h, a, strip_idx):
        start = pl.multiple_of(strip_idx * strip, strip)
        return _dot(k_ref[pl.ds(start, strip), h * MLA_QW:(h + 1) * MLA_QW],
                    qt_ref[h * MLA_QW:(h + 1) * MLA_QW, a * cw:(a + 1) * cw])

    def col_max(s):
        return jnp.max(s, axis=0, keepdims=True)

    def update(h, a, strip_idx, s, s_max, m, l):
        m_new = jnp.maximum(m, s_max)
        alpha = jnp.exp2(m - m_new)
        p = jnp.exp2(s - m_new)
        l = alpha * l + jnp.sum(p, axis=0, keepdims=True)
        acc_ref[h, a] = alpha * acc_ref[h, a] + _dot(vt_ref[h, strip_idx], p.astype(BF16))
        return m_new, l

    def half_step(cur_ref, nxt_ref, strip_idx, carry):
        nxt_max = []
        for h, a in chains:
            s = scores(h, a, strip_idx + 1)
            nxt_ref[h, a] = s
            nxt_max.append(col_max(s))
        out = []
        for c, (h, a) in enumerate(chains):
            m, l, cur_max = carry[c]
            m, l = update(h, a, strip_idx, cur_ref[h, a], cur_max, m, l)
            out.append((m, l, nxt_max[c]))
        return tuple(out)

    def body(jj, carry):
        carry = half_step(sa_ref, sb_ref, 2 * jj, carry)
        return half_step(sb_ref, sa_ref, 2 * jj + 1, carry)

    init = []
    for h, a in chains:
        s = scores(h, a, 0)
        sa_ref[h, a] = s
        init.append((jnp.full((1, cw), MASK_VALUE, F32), jnp.zeros((1, cw), F32), col_max(s)))
    carry = lax.fori_loop(0, i, body, tuple(init))
    first = per_block * i
    causal = lax.broadcasted_iota(jnp.int32, (strip, cw), 0) <= lax.broadcasted_iota(jnp.int32, (strip, cw), 1)
    last = [scores(h, 1, first + 1) for h in range(hp)]
    for c, (h, a) in enumerate(chains):
        m, l, s_max = carry[c]
        if a == 0:
            s = jnp.where(causal, sa_ref[h, a], MASK_VALUE)
            m, l = update(h, a, first, s, col_max(s), m, l)
        else:
            m, l = update(h, a, first, sa_ref[h, a], s_max, m, l)
            s = jnp.where(causal, last[h], MASK_VALUE)
            m, l = update(h, a, first + 1, s, col_max(s), m, l)
        o_t = acc_ref[h, a] / l
        o_ref[a * cw:(a + 1) * cw, h * MLA_V:(h + 1) * MLA_V] = o_t.T.astype(o_ref.dtype)


def _flash(qt, k, vt, bsz, seq):
    t = k.shape[0]
    assert seq % MLA_QBLOCK == 0
    nq = seq // MLA_QBLOCK
    hp = MLA_HEAD_GROUP
    nch = MLA_QBLOCK // MLA_CHAIN
    return pl.pallas_call(
        _flash_kernel,
        grid=(bsz, MLA_HEADS // hp, nq),
        in_specs=[pl.BlockSpec((hp * MLA_QW, MLA_QBLOCK), lambda b, g, i: (g, b * nq + i)),
                  pl.BlockSpec((seq, hp * MLA_QW), lambda b, g, i: (b, g)),
                  pl.BlockSpec((hp, seq // MLA_STRIP, MLA_V, MLA_STRIP), lambda b, g, i: (g, b, 0, 0))],
        out_specs=pl.BlockSpec((MLA_QBLOCK, hp * MLA_V), lambda b, g, i: (b * nq + i, g)),
        out_shape=jax.ShapeDtypeStruct((t, MLA_HEADS * MLA_V), BF16),
        scratch_shapes=[pltpu.VMEM((hp, nch, MLA_V, MLA_CHAIN), F32),
                        pltpu.VMEM((hp, nch, MLA_STRIP, MLA_CHAIN), F32),
                        pltpu.VMEM((hp, nch, MLA_STRIP, MLA_CHAIN), F32)],
        compiler_params=_params("parallel", "parallel", "arbitrary"),
        name="flash",
    )(qt, k, vt)


def _mixout_kernel(og_ref, om_ref, ga_ref, gb_ref, x_ref, mod_ref, wg_ref, wm_ref, wo_ref, o_ref):
    yg = _dot(og_ref[...], wg_ref[...])
    ym = _dot(om_ref[...], wm_ref[...])
    ga = jax.nn.sigmoid(ga_ref[...].astype(F32))
    gb = jax.nn.sigmoid(gb_ref[...].astype(F32))
    merged = (ga * yg + gb * ym).astype(BF16)
    o_ref[...] = x_ref[...] + mod_ref[0, MOD_GATE_MIX:MOD_GATE_MIX + 1, :] * _dot(merged, wo_ref[...])


def _mixout(o_gla, o_mla, p, gate_col, x2, mod3, w_gla, w_mla, w_out, seq):
    t, d = x2.shape
    tm = _tile(seq, 256)
    assert gate_col % d == 0
    row = lambda i: (i, 0)
    resident = lambda w: pl.BlockSpec(w.shape, lambda i: (0, 0), pipeline_mode=pl.Buffered(1))
    return pl.pallas_call(
        _mixout_kernel,
        grid=(t // tm,),
        in_specs=[pl.BlockSpec((tm, o_gla.shape[1]), row),
                  pl.BlockSpec((tm, o_mla.shape[1]), row),
                  pl.BlockSpec((tm, d), lambda i: (i, gate_col // d)),
                  pl.BlockSpec((tm, d), lambda i: (i, gate_col // d + 1)),
                  pl.BlockSpec((tm, d), row),
                  pl.BlockSpec((1, N_MOD, d), lambda i: ((i * tm) // seq, 0, 0)),
                  resident(w_gla), resident(w_mla), resident(w_out)],
        out_specs=pl.BlockSpec((tm, d), row),
        out_shape=jax.ShapeDtypeStruct((t, d), F32),
        compiler_params=_params("parallel"),
        name="mixout",
    )(o_gla, o_mla, p, p, x2, mod3, w_gla, w_mla, w_out)


def _ffn_kernel(x_ref, mod_ref, g_ref, wg_ref, wu_ref, wd_ref, fg_ref, o_ref, h_ref, acc_ref, *, final_norm):
    f = pl.program_id(1)

    @pl.when(f == 0)
    def _():
        h = _modnorm(x_ref[...], g_ref[...],
                     mod_ref[0, MOD_SCALE_FFN:MOD_SCALE_FFN + 1, :], mod_ref[0, MOD_SHIFT_FFN:MOD_SHIFT_FFN + 1, :])
        h_ref[...] = h.astype(BF16)
        acc_ref[...] = jnp.zeros_like(acc_ref)

    h = h_ref[...]
    gate = _dot(h, wg_ref[...])
    up = _dot(h, wu_ref[...])
    act = (gate * jax.nn.sigmoid(gate) * up).astype(BF16)
    acc_ref[...] += _dot(act, wd_ref[...])

    @pl.when(f == pl.num_programs(1) - 1)
    def _():
        y = x_ref[...] + mod_ref[0, MOD_GATE_FFN:MOD_GATE_FFN + 1, :] * acc_ref[...]
        if final_norm:
            y = _rms(y) * fg_ref[...]
        o_ref[...] = y


def _ffn(x1, mod3, gain, w_in, w_down, final_g, seq, final_norm):
    t, d = x1.shape
    hidden = w_down.shape[0]
    tm = _tile(seq, 1024)
    tf = _tile(hidden, 256)
    nf = hidden // tf
    kern = functools.partial(_ffn_kernel, final_norm=final_norm)
    return pl.pallas_call(
        kern,
        grid=(t // tm, nf),
        in_specs=[pl.BlockSpec((tm, d), lambda i, f: (i, 0)),
                  pl.BlockSpec((1, N_MOD, d), lambda i, f: ((i * tm) // seq, 0, 0)),
                  pl.BlockSpec((1, d), lambda i, f: (0, 0)),
                  pl.BlockSpec((d, tf), lambda i, f: (0, f)),
                  pl.BlockSpec((d, tf), lambda i, f: (0, nf + f)),
                  pl.BlockSpec((tf, d), lambda i, f: (f, 0)),
                  pl.BlockSpec((1, d), lambda i, f: (0, 0))],
        out_specs=pl.BlockSpec((tm, d), lambda i, f: (i, 0), pipeline_mode=pl.Buffered(1)),
        out_shape=jax.ShapeDtypeStruct((t, d), F32),
        scratch_shapes=[pltpu.VMEM((tm, d), BF16), pltpu.VMEM((tm, d), F32)],
        compiler_params=_params("parallel", "arbitrary"),
        name="ffn",
    )(x1, mod3, gain.reshape(1, d), w_in, w_in, w_down, final_g.reshape(1, d))


def _pack_w_in_kernel(wt_ref, o_ref, *, moves, kr_row, gk_row, tail_col):
    for dst, src, width in moves:
        o_ref[:, dst:dst + width] = wt_ref[src:src + width, :].T.astype(BF16)
    half = MLA_ROPE // 2
    kr = wt_ref[kr_row:kr_row + MLA_ROPE, :]
    gk = wt_ref[gk_row:gk_row + GLA_GATE_RANK, :]
    pad = jnp.zeros((V7X_LANES - GLA_GATE_RANK, kr.shape[1]), F32)
    o_ref[:, tail_col:] = jnp.concatenate([kr, kr[half:], kr[:half], gk, pad], axis=0).T.astype(BF16)


def _pack_w_in(w, d, dk_all, dv_all, q_rank, kv_rank):
    widths = (dk_all, dk_all, dv_all, dv_all, GLA_GATE_RANK, q_rank, kv_rank, MLA_ROPE, d, d)
    offs = [0]
    for wd in widths:
        offs.append(offs[-1] + wd)
    gla_w = offs[4]
    moves = ((0, 0, gla_w), (gla_w, offs[8], 2 * d), (gla_w + 2 * d, offs[5], q_rank + kv_rank))
    tail_col = gla_w + 2 * d + q_rank + kv_rank
    n_out = tail_col + 2 * V7X_LANES
    assert all(dst % V7X_LANES == 0 for dst, _, _ in moves) and tail_col % V7X_LANES == 0
    rows = w.shape[0]
    tr = _tile(rows, 128)
    kern = functools.partial(_pack_w_in_kernel, moves=moves, kr_row=offs[7], gk_row=offs[4], tail_col=tail_col)
    return pl.pallas_call(
        kern,
        grid=(rows // tr,),
        in_specs=[pl.BlockSpec((w.shape[1], tr), lambda i: (0, i))],
        out_specs=pl.BlockSpec((tr, n_out), lambda i: (i, 0)),
        out_shape=jax.ShapeDtypeStruct((rows, n_out), BF16),
        compiler_params=_params("parallel"),
        name="pack_w_in",
    )(w.T)


def _pack_wuq_t(w):
    r = w.shape[0]
    half = MLA_ROPE // 2
    w3 = w.reshape(r, MLA_HEADS, MLA_NOPE + MLA_ROPE)
    rope = w3[:, :, MLA_NOPE:]
    rot = jnp.concatenate([rope[:, :, half:], rope[:, :, :half]], axis=2)
    return jnp.concatenate([w3[:, :, :MLA_NOPE], rope, rot], axis=2).reshape(r, -1).T.astype(BF16)


def _pack_wukv(w):
    r = w.shape[0]
    w3 = w.reshape(r, MLA_HEADS, MLA_NOPE + MLA_V)
    wk = w3[:, :, :MLA_NOPE].reshape(r, -1).astype(BF16)
    wvt = w3[:, :, MLA_NOPE:].reshape(r, -1).T.astype(BF16)
    return wk, wvt


def kernel(x, c, positions, ada_w, ada_b, norm_mix_g, w_in, gla_gk_w, gla_gk_b, gla_onorm_g, gla_wo, mla_q_norm_g, mla_wuq, mla_kv_norm_g, mla_wukv, mla_wo, w_out, norm_ffn_g, ffn_w_in, ffn_w_down, final_norm_g):
    bsz, seq, d = x.shape
    depth = w_in.shape[0]
    dk_all = gla_gk_w.shape[2]
    dv_all = gla_wo.shape[1]
    dk, dv = dk_all // GLA_HEADS, dv_all // GLA_HEADS
    q_rank, kv_rank = mla_wuq.shape[1], mla_wukv.shape[1]
    lat_col = 2 * dk_all + 2 * dv_all + 2 * d
    lat_width = q_rank + kv_rank + 2 * V7X_LANES
    gate_col = 2 * dk_all + 2 * dv_all
    assert lat_col % lat_width == 0 and (lat_col + lat_width - V7X_LANES) % V7X_LANES == 0

    x2 = x.reshape(bsz * seq, d)
    cos_t, sin_t = _rope_tab(positions)
    for l in range(depth):
        mod3 = _ada(c, ada_w[l], ada_b[l]).reshape(bsz, N_MOD, d)
        w_all = _pack_w_in(w_in[l], d, dk_all, dv_all, q_rank, kv_rank)
        p = _inproj(x2, mod3, norm_mix_g[l], w_all, seq)
        wk, wvt = _pack_wukv(mla_wukv[l])
        qt, k, vt = _mlaproj(p, lat_col // lat_width, lat_width, cos_t, sin_t, mla_q_norm_g[l], mla_kv_norm_g[l],
                             _pack_wuq_t(mla_wuq[l]), wk, wvt)
        gkw = jnp.zeros((V7X_LANES, dk_all), BF16).at[:GLA_GATE_RANK].set(gla_gk_w[l].astype(BF16))
        o_gla = _gla(p, (lat_col + lat_width - V7X_LANES) // V7X_LANES, gkw, gla_gk_b[l], gla_onorm_g[l],
                     bsz, seq, dk, dv)
        o_mla = _flash(qt, k, vt, bsz, seq)
        x1 = _mixout(o_gla, o_mla, p, gate_col, x2, mod3, gla_wo[l].astype(BF16), mla_wo[l].astype(BF16),
                     w_out[l].astype(BF16), seq)
        x2 = _ffn(x1, mod3, norm_ffn_g[l], ffn_w_in[l].astype(BF16), ffn_w_down[l].astype(BF16), final_norm_g,
                  seq, final_norm=(l == depth - 1))
    return x2.reshape(bsz, seq, d)
```

```python
import functools

import jax
import jax.numpy as jnp
from jax import lax
from jax.experimental import pallas as pl
from jax.experimental.pallas import tpu as pltpu

F32 = jnp.float32
BF16 = jnp.bfloat16

GLA_HEADS = 4
GLA_GATE_RANK = 16
GLA_GATE_NORMALIZER = 16.0
MLA_HEADS = 16
MLA_NOPE = 128
MLA_ROPE = 64
MLA_V = 128
MLA_QW = MLA_NOPE + 2 * MLA_ROPE
MLA_QBLOCK = 1024
MLA_STRIP = 512
MLA_CHAIN = 512
MLA_HEAD_GROUP = 2
ROPE_THETA = 10000.0
LOG2_E = 1.4426950408889634
NORM_EPS = 1e-6
N_MOD = 6
MOD_SHIFT_MIX, MOD_SCALE_MIX, MOD_GATE_MIX, MOD_SHIFT_FFN, MOD_SCALE_FFN, MOD_GATE_FFN = range(N_MOD)

V7X_LANES = 128
V7X_VMEM_LIMIT_BYTES = 56 * 1024 * 1024

MASK_VALUE = -1e30


def _params(*semantics):
    return pltpu.CompilerParams(dimension_semantics=semantics, vmem_limit_bytes=V7X_VMEM_LIMIT_BYTES)


def _dot(a, b):
    return jnp.dot(a, b, preferred_element_type=F32)


def _dot_nt(a, b):
    return lax.dot_general(a, b, (((1,), (1,)), ((), ())), preferred_element_type=F32)


def _dot_tn(a, b):
    return lax.dot_general(a, b, (((0,), (0,)), ((), ())), preferred_element_type=F32)


def _rms(x):
    return x * lax.rsqrt(jnp.mean(x * x, axis=-1, keepdims=True) + NORM_EPS)


def _modnorm(x, gain, scale, shift):
    return _rms(x) * (gain * (1.0 + scale)) + shift


def _tile(n, want):
    t = min(n, want)
    assert n % t == 0, (n, want)
    return t


def _ada_kernel(c_ref, w_ref, b_ref, o_ref):
    c = c_ref[...]
    act = (c * jax.nn.sigmoid(c)).astype(BF16)
    o_ref[...] = _dot(act, w_ref[...].astype(BF16)) + b_ref[...]


def _ada(c, w, b):
    bsz, d = c.shape
    n = w.shape[1]
    tn = _tile(n, 1024)
    return pl.pallas_call(
        _ada_kernel,
        grid=(n // tn,),
        in_specs=[pl.BlockSpec((bsz, d), lambda j: (0, 0)),
                  pl.BlockSpec((d, tn), lambda j: (0, j)),
                  pl.BlockSpec((1, tn), lambda j: (0, j))],
        out_specs=pl.BlockSpec((bsz, tn), lambda j: (0, j)),
        out_shape=jax.ShapeDtypeStruct((bsz, n), F32),
        compiler_params=_params("arbitrary"),
        name="ada",
    )(c, w, b.reshape(1, n))


def _rope_tab_kernel(pos_ref, f_ref, cos_ref, sin_ref):
    ang = pos_ref[...].astype(F32) * f_ref[...]
    cos_ref[...] = jnp.cos(ang)
    sin_ref[...] = jnp.sin(ang)


def _rope_tab(positions):
    t = positions.size
    half = MLA_ROPE // 2
    inv_freq = ROPE_THETA ** (-jnp.arange(0, MLA_ROPE, 2, dtype=F32) / MLA_ROPE)
    tt = _tile(t, 4096)
    spec = pl.BlockSpec((half, tt), lambda i: (0, i))
    return pl.pallas_call(
        _rope_tab_kernel,
        grid=(t // tt,),
        in_specs=[pl.BlockSpec((1, tt), lambda i: (0, i)), pl.BlockSpec((half, 1), lambda i: (0, 0))],
        out_specs=[spec, spec],
        out_shape=[jax.ShapeDtypeStruct((half, t), F32)] * 2,
        compiler_params=_params("arbitrary"),
        name="rope_tab",
    )(positions.reshape(1, t), inv_freq.reshape(half, 1))


def _inproj_kernel(x_ref, mod_ref, g_ref, w_ref, o_ref, h_ref):
    @pl.when(pl.program_id(1) == 0)
    def _():
        h = _modnorm(x_ref[...], g_ref[...],
                     mod_ref[0, MOD_SCALE_MIX:MOD_SCALE_MIX + 1, :], mod_ref[0, MOD_SHIFT_MIX:MOD_SHIFT_MIX + 1, :])
        h_ref[...] = h.astype(BF16)

    o_ref[...] = _dot(h_ref[...], w_ref[...]).astype(o_ref.dtype)


def _inproj(x2, mod3, gain, w, seq):
    t, d = x2.shape
    n = w.shape[1]
    tm = _tile(seq, 1024)
    tn = _tile(n, 1280)
    return pl.pallas_call(
        _inproj_kernel,
        grid=(t // tm, n // tn),
        in_specs=[pl.BlockSpec((tm, d), lambda i, j: (i, 0)),
                  pl.BlockSpec((1, N_MOD, d), lambda i, j: ((i * tm) // seq, 0, 0)),
                  pl.BlockSpec((1, d), lambda i, j: (0, 0)),
                  pl.BlockSpec((d, tn), lambda i, j: (0, j))],
        out_specs=pl.BlockSpec((tm, tn), lambda i, j: (i, j)),
        out_shape=jax.ShapeDtypeStruct((t, n), BF16),
        scratch_shapes=[pltpu.VMEM((tm, d), BF16)],
        compiler_params=_params("parallel", "arbitrary"),
        name="inproj",
    )(x2, mod3, gain.reshape(1, d), w)


def _mlaproj_kernel(lat_ref, cos_ref, sin_ref, gq_ref, gkv_ref, wqt_ref, wk_ref, wvt_ref, qt_ref, k_ref, vt_ref, *,
                    q_rank, kv_rank, scale):
    tm = lat_ref.shape[0]
    cos = cos_ref[...]
    sin = sin_ref[...]
    cs = jnp.concatenate([cos, cos], axis=0)
    sn = jnp.concatenate([-sin, sin], axis=0)

    cqn = _rms(lat_ref[:, 0:q_rank].astype(F32)) * gq_ref[...]
    ckvn = _rms(lat_ref[:, q_rank:q_rank + kv_rank].astype(F32)) * gkv_ref[...]
    cqn_t = cqn.T.astype(BF16)
    ckvn_t = ckvn.T.astype(BF16)
    ckvn = ckvn.astype(BF16)

    kr_t = lat_ref[:, q_rank + kv_rank:q_rank + kv_rank + V7X_LANES].astype(F32).T
    k_rope_t = kr_t[:MLA_ROPE] * cs + kr_t[MLA_ROPE:] * sn
    k_rope = jnp.concatenate([k_rope_t, jnp.zeros_like(k_rope_t)], axis=0).T.astype(BF16)

    for h in range(MLA_HEADS):
        qt = _dot(wqt_ref[h * MLA_QW:(h + 1) * MLA_QW, :], cqn_t)
        q_rope = qt[MLA_NOPE:MLA_NOPE + MLA_ROPE] * cs + qt[MLA_NOPE + MLA_ROPE:] * sn
        qt_ref[h * MLA_QW:h * MLA_QW + MLA_NOPE, :] = (qt[:MLA_NOPE] * scale).astype(BF16)
        qt_ref[h * MLA_QW + MLA_NOPE:h * MLA_QW + MLA_NOPE + MLA_ROPE, :] = (q_rope * scale).astype(BF16)
        qt_ref[h * MLA_QW + MLA_NOPE + MLA_ROPE:(h + 1) * MLA_QW, :] = jnp.zeros((MLA_ROPE, tm), BF16)
        vt_ref[h, 0] = _dot(wvt_ref[h * MLA_V:(h + 1) * MLA_V, :], ckvn_t).astype(BF16)
    for g in range(MLA_HEADS // 2):
        kk = _dot(ckvn, wk_ref[:, 2 * g * MLA_NOPE:2 * (g + 1) * MLA_NOPE]).astype(BF16)
        for e in range(2):
            h = 2 * g + e
            k_ref[:, h * MLA_QW:h * MLA_QW + MLA_NOPE] = kk[:, e * MLA_NOPE:(e + 1) * MLA_NOPE]
            k_ref[:, h * MLA_QW + MLA_NOPE:(h + 1) * MLA_QW] = k_rope


def _mlaproj(p, lat_block, lat_width, cos_t, sin_t, gq, gkv, wqt, wk, wvt):
    t = p.shape[0]
    q_rank, kv_rank = gq.shape[0], gkv.shape[0]
    tm = _tile(MLA_STRIP, 512)
    per_strip = MLA_STRIP // tm
    half = MLA_ROPE // 2
    scale = (MLA_NOPE + MLA_ROPE) ** -0.5 * LOG2_E
    kern = functools.partial(_mlaproj_kernel, q_rank=q_rank, kv_rank=kv_rank, scale=scale)
    return pl.pallas_call(
        kern,
        grid=(t // tm,),
        in_specs=[pl.BlockSpec((tm, lat_width), lambda i: (i, lat_block)),
                  pl.BlockSpec((half, tm), lambda i: (0, i)),
                  pl.BlockSpec((half, tm), lambda i: (0, i)),
                  pl.BlockSpec((1, q_rank), lambda i: (0, 0)),
                  pl.BlockSpec((1, kv_rank), lambda i: (0, 0)),
                  pl.BlockSpec(wqt.shape, lambda i: (0, 0)),
                  pl.BlockSpec(wk.shape, lambda i: (0, 0)),
                  pl.BlockSpec(wvt.shape, lambda i: (0, 0))],
        out_specs=[pl.BlockSpec((MLA_HEADS * MLA_QW, tm), lambda i: (0, i)),
                   pl.BlockSpec((tm, MLA_HEADS * MLA_QW), lambda i: (i, 0)),
                   pl.BlockSpec((MLA_HEADS, 1, MLA_V, tm), lambda i: (0, i // per_strip, 0, i % per_strip))],
        out_shape=[jax.ShapeDtypeStruct((MLA_HEADS * MLA_QW, t), BF16),
                   jax.ShapeDtypeStruct((t, MLA_HEADS * MLA_QW), BF16),
                   jax.ShapeDtypeStruct((MLA_HEADS, t // MLA_STRIP, MLA_V, MLA_STRIP), BF16)],
        compiler_params=_params("parallel"),
        name="mlaproj",
    )(p, cos_t, sin_t, gq.reshape(1, q_rank), gkv.reshape(1, kv_rank), wqt, wk, wvt)


def _gla_kernel(q_ref, k_ref, v_ref, g_ref, gk_ref, gkw_ref, gkb_ref, on_ref, o_ref, st_ref, *, dk, dv):
    c = q_ref.shape[0]

    @pl.when(pl.program_id(1) == 0)
    def _():
        st_ref[...] = jnp.zeros_like(st_ref)

    z = _dot(gk_ref[...], gkw_ref[...]) + gkb_ref[...]
    log_a = jax.nn.log_sigmoid(z) * (LOG2_E / GLA_GATE_NORMALIZER)

    halves = [1 << e for e in range(c.bit_length() - 1)]
    row = lax.broadcasted_iota(jnp.int32, (c, dk), 0)
    ai = lax.broadcasted_iota(jnp.int32, (c, c), 0)
    aj = lax.broadcasted_iota(jnp.int32, (c, c), 1)
    diff = ai ^ aj
    causal_diff = jnp.where(ai > aj, diff, 0)
    level_mask = {s: (causal_diff >> (s.bit_length() - 1)) == 1 for s in halves}
    diag_mask = ai == aj

    for h in range(GLA_HEADS):
        g = log_a[:, h * dk:(h + 1) * dk]
        prefix = g
        total = g
        decay = {}
        for s in halves:
            odd = (row & s) != 0
            sibling = jnp.where(odd, pltpu.roll(total, s, axis=0), pltpu.roll(total, c - s, axis=0))
            decay[s] = jnp.exp2(jnp.where(odd, prefix, total - prefix))
            prefix = prefix + jnp.where(odd, sibling, 0.0)
            total = total + sibling
        qh = q_ref[:, h * dk:(h + 1) * dk].astype(F32) * (dk ** -0.5)
        kh = k_ref[:, h * dk:(h + 1) * dk].astype(F32)
        vh = v_ref[:, h * dv:(h + 1) * dv]
        att = jnp.where(diag_mask, _dot_nt(qh.astype(BF16), kh.astype(BF16)), 0.0)
        for s in halves:
            a = _dot_nt((qh * decay[s]).astype(BF16), (kh * decay[s]).astype(BF16))
            att = jnp.where(level_mask[s], a, att)
        st = st_ref[h]
        o = _dot_nt((qh * jnp.exp2(prefix)).astype(BF16), st.astype(BF16)) + _dot(att.astype(BF16), vh)
        k_dec = (kh * jnp.exp2(total - prefix)).astype(BF16)
        st_ref[h] = st * jnp.exp2(total[0:1, :]) + _dot_tn(vh, k_dec)
        gate = g_ref[:, h * dv:(h + 1) * dv].astype(F32)
        o_ref[:, h * dv:(h + 1) * dv] = (_rms(o) * on_ref[...] * (gate * jax.nn.sigmoid(gate))).astype(o_ref.dtype)


def _gla(p, gk_block, gkw, gkb, onorm_g, bsz, seq, dk, dv):
    t = p.shape[0]
    c = _tile(seq, 128)
    nc = seq // c
    hk, hv = GLA_HEADS * dk, GLA_HEADS * dv
    kern = functools.partial(_gla_kernel, dk=dk, dv=dv)
    row = lambda b, i: b * nc + i
    return pl.pallas_call(
        kern,
        grid=(bsz, nc),
        in_specs=[pl.BlockSpec((c, hk), lambda b, i: (row(b, i), 0)),
                  pl.BlockSpec((c, hk), lambda b, i: (row(b, i), 1)),
                  pl.BlockSpec((c, hv), lambda b, i: (row(b, i), (2 * hk) // hv)),
                  pl.BlockSpec((c, hv), lambda b, i: (row(b, i), (2 * hk) // hv + 1)),
                  pl.BlockSpec((c, V7X_LANES), lambda b, i: (row(b, i), gk_block)),
                  pl.BlockSpec(gkw.shape, lambda b, i: (0, 0)),
                  pl.BlockSpec((1, hk), lambda b, i: (0, 0)),
                  pl.BlockSpec((1, dv), lambda b, i: (0, 0))],
        out_specs=pl.BlockSpec((c, hv), lambda b, i: (row(b, i), 0)),
        out_shape=jax.ShapeDtypeStruct((t, hv), BF16),
        scratch_shapes=[pltpu.VMEM((GLA_HEADS, dv, dk), F32)],
        compiler_params=_params("parallel", "arbitrary"),
        name="gla",
    )(p, p, p, p, p, gkw, gkb.reshape(1, hk), onorm_g.reshape(1, dv))


def _flash_kernel(qt_ref, k_ref, vt_ref, o_ref, acc_ref, sa_ref, sb_ref, pa_ref, pb_ref):
    strip, cw = MLA_STRIP, MLA_CHAIN
    hp, nch = acc_ref.shape[0], acc_ref.shape[1]
    chains = [(h, a) for h in range(hp) for a in range(nch)]
    per_block = MLA_QBLOCK // strip
    assert per_block == 2 and nch == 2 and strip == cw
    i = pl.program_id(2)
    acc_ref[...] = jnp.zeros_like(acc_ref)
    pb_ref[...] = jnp.zeros_like(pb_ref)

    def scores(h, a, strip_idx):
        start = pl.multiple_of(strip_idx * strip, strip)
        return _dot(k_ref[pl.ds(start, strip), h * MLA_QW:(h + 1) * MLA_QW],
                    qt_ref[h * MLA_QW:(h + 1) * MLA_QW, a * cw:(a + 1) * cw])

    def col_max(s):
        return jnp.max(s, axis=0, keepdims=True)

    def softmax(s, s_max, m, l):
        m_new = jnp.maximum(m, s_max)
        alpha = jnp.exp2(m - m_new)
        p = jnp.exp2(s - m_new)
        return p.astype(BF16), alpha, m_new, alpha * l + jnp.sum(p, axis=0, keepdims=True)

    def accumulate(h, a, strip_idx, p, alpha):
        acc_ref[h, a] = alpha * acc_ref[h, a] + _dot(vt_ref[h, strip_idx], p)

    def half_step(s_cur, s_nxt, p_cur, p_prev, t, carry):
        out = []
        for c, (h, a) in enumerate(chains):
            m, l, cur_max, alpha_prev = carry[c]
            s = scores(h, a, t + 1)
            s_nxt[h, a] = s
            accumulate(h, a, jnp.maximum(t - 1, 0), p_prev[h, a], alpha_prev)
            p, alpha, m, l = softmax(s_cur[h, a], cur_max, m, l)
            p_cur[h, a] = p
            out.append((m, l, col_max(s), alpha))
        return tuple(out)

    def body(jj, carry):
        carry = half_step(sa_ref, sb_ref, pa_ref, pb_ref, 2 * jj, carry)
        return half_step(sb_ref, sa_ref, pb_ref, pa_ref, 2 * jj + 1, carry)

    init = []
    for h, a in chains:
        s = scores(h, a, 0)
        sa_ref[h, a] = s
        init.append((jnp.full((1, cw), MASK_VALUE, F32), jnp.zeros((1, cw), F32), col_max(s), jnp.ones((1, cw), F32)))
    carry = lax.fori_loop(0, i, body, tuple(init))
    first = per_block * i
    causal = lax.broadcasted_iota(jnp.int32, (strip, cw), 0) <= lax.broadcasted_iota(jnp.int32, (strip, cw), 1)
    last = [scores(h, 1, first + 1) for h in range(hp)]

    def update(h, a, strip_idx, s, s_max, m, l):
        p, alpha, m, l = softmax(s, s_max, m, l)
        accumulate(h, a, strip_idx, p, alpha)
        return m, l

    for c, (h, a) in enumerate(chains):
        m, l, s_max, alpha_prev = carry[c]
        accumulate(h, a, jnp.maximum(first - 1, 0), pb_ref[h, a], alpha_prev)
        if a == 0:
            s = jnp.where(causal, sa_ref[h, a], MASK_VALUE)
            m, l = update(h, a, first, s, col_max(s), m, l)
        else:
            m, l = update(h, a, first, sa_ref[h, a], s_max, m, l)
            s = jnp.where(causal, last[h], MASK_VALUE)
            m, l = update(h, a, first + 1, s, col_max(s), m, l)
        o_t = acc_ref[h, a] / l
        o_ref[a * cw:(a + 1) * cw, h * MLA_V:(h + 1) * MLA_V] = o_t.T.astype(o_ref.dtype)


def _flash(qt, k, vt, bsz, seq):
    t = k.shape[0]
    assert seq % MLA_QBLOCK == 0
    nq = seq // MLA_QBLOCK
    hp = MLA_HEAD_GROUP
    nch = MLA_QBLOCK // MLA_CHAIN
    return pl.pallas_call(
        _flash_kernel,
        grid=(bsz, MLA_HEADS // hp, nq),
        in_specs=[pl.BlockSpec((hp * MLA_QW, MLA_QBLOCK), lambda b, g, i: (g, b * nq + i)),
                  pl.BlockSpec((seq, hp * MLA_QW), lambda b, g, i: (b, g)),
                  pl.BlockSpec((hp, seq // MLA_STRIP, MLA_V, MLA_STRIP), lambda b, g, i: (g, b, 0, 0))],
        out_specs=pl.BlockSpec((MLA_QBLOCK, hp * MLA_V), lambda b, g, i: (b * nq + i, g)),
        out_shape=jax.ShapeDtypeStruct((t, MLA_HEADS * MLA_V), BF16),
        scratch_shapes=[pltpu.VMEM((hp, nch, MLA_V, MLA_CHAIN), F32),
                        pltpu.VMEM((hp, nch, MLA_STRIP, MLA_CHAIN), F32),
                        pltpu.VMEM((hp, nch, MLA_STRIP, MLA_CHAIN), F32),
                        pltpu.VMEM((hp, nch, MLA_STRIP, MLA_CHAIN), BF16),
                        pltpu.VMEM((hp, nch, MLA_STRIP, MLA_CHAIN), BF16)],
        compiler_params=_params("parallel", "parallel", "arbitrary"),
        name="flash",
    )(qt, k, vt)


def _mixout_kernel(og_ref, om_ref, ga_ref, gb_ref, x_ref, mod_ref, wg_ref, wm_ref, wo_ref, o_ref):
    yg = _dot(og_ref[...], wg_ref[...])
    ym = _dot(om_ref[...], wm_ref[...])
    ga = jax.nn.sigmoid(ga_ref[...].astype(F32))
    gb = jax.nn.sigmoid(gb_ref[...].astype(F32))
    merged = (ga * yg + gb * ym).astype(BF16)
    o_ref[...] = x_ref[...] + mod_ref[0, MOD_GATE_MIX:MOD_GATE_MIX + 1, :] * _dot(merged, wo_ref[...])


def _mixout(o_gla, o_mla, p, gate_col, x2, mod3, w_gla, w_mla, w_out, seq):
    t, d = x2.shape
    tm = _tile(seq, 256)
    assert gate_col % d == 0
    row = lambda i: (i, 0)
    resident = lambda w: pl.BlockSpec(w.shape, lambda i: (0, 0), pipeline_mode=pl.Buffered(1))
    return pl.pallas_call(
        _mixout_kernel,
        grid=(t // tm,),
        in_specs=[pl.BlockSpec((tm, o_gla.shape[1]), row),
                  pl.BlockSpec((tm, o_mla.shape[1]), row),
                  pl.BlockSpec((tm, d), lambda i: (i, gate_col // d)),
                  pl.BlockSpec((tm, d), lambda i: (i, gate_col // d + 1)),
                  pl.BlockSpec((tm, d), row),
                  pl.BlockSpec((1, N_MOD, d), lambda i: ((i * tm) // seq, 0, 0)),
                  resident(w_gla), resident(w_mla), resident(w_out)],
        out_specs=pl.BlockSpec((tm, d), row),
        out_shape=jax.ShapeDtypeStruct((t, d), F32),
        compiler_params=_params("parallel"),
        name="mixout",
    )(o_gla, o_mla, p, p, x2, mod3, w_gla, w_mla, w_out)


def _ffn_kernel(x_ref, mod_ref, g_ref, wg_ref, wu_ref, wd_ref, fg_ref, o_ref, h_ref, acc_ref, *, final_norm):
    f = pl.program_id(1)

    @pl.when(f == 0)
    def _():
        h = _modnorm(x_ref[...], g_ref[...],
                     mod_ref[0, MOD_SCALE_FFN:MOD_SCALE_FFN + 1, :], mod_ref[0, MOD_SHIFT_FFN:MOD_SHIFT_FFN + 1, :])
        h_ref[...] = h.astype(BF16)
        acc_ref[...] = jnp.zeros_like(acc_ref)

    h = h_ref[...]
    gate = _dot(h, wg_ref[...])
    up = _dot(h, wu_ref[...])
    act = (gate * jax.nn.sigmoid(gate) * up).astype(BF16)
    acc_ref[...] += _dot(act, wd_ref[...])

    @pl.when(f == pl.num_programs(1) - 1)
    def _():
        y = x_ref[...] + mod_ref[0, MOD_GATE_FFN:MOD_GATE_FFN + 1, :] * acc_ref[...]
        if final_norm:
            y = _rms(y) * fg_ref[...]
        o_ref[...] = y


def _ffn(x1, mod3, gain, w_in, w_down, final_g, seq, final_norm):
    t, d = x1.shape
    hidden = w_down.shape[0]
    tm = _tile(seq, 512)
    tf = _tile(hidden, 512)
    nf = hidden // tf
    kern = functools.partial(_ffn_kernel, final_norm=final_norm)
    return pl.pallas_call(
        kern,
        grid=(t // tm, nf),
        in_specs=[pl.BlockSpec((tm, d), lambda i, f: (i, 0)),
                  pl.BlockSpec((1, N_MOD, d), lambda i, f: ((i * tm) // seq, 0, 0)),
                  pl.BlockSpec((1, d), lambda i, f: (0, 0)),
                  pl.BlockSpec((d, tf), lambda i, f: (0, f)),
                  pl.BlockSpec((d, tf), lambda i, f: (0, nf + f)),
                  pl.BlockSpec((tf, d), lambda i, f: (f, 0)),
                  pl.BlockSpec((1, d), lambda i, f: (0, 0))],
        out_specs=pl.BlockSpec((tm, d), lambda i, f: (i, 0)),
        out_shape=jax.ShapeDtypeStruct((t, d), F32),
        scratch_shapes=[pltpu.VMEM((tm, d), BF16), pltpu.VMEM((tm, d), F32)],
        compiler_params=_params("parallel", "arbitrary"),
        name="ffn",
    )(x1, mod3, gain.reshape(1, d), w_in, w_in, w_down, final_g.reshape(1, d))


def _pack_w_in_kernel(wt_ref, o_ref, *, moves, kr_row, gk_row, tail_col):
    for dst, src, width in moves:
        o_ref[:, dst:dst + width] = wt_ref[src:src + width, :].T.astype(BF16)
    half = MLA_ROPE // 2
    kr = wt_ref[kr_row:kr_row + MLA_ROPE, :]
    gk = wt_ref[gk_row:gk_row + GLA_GATE_RANK, :]
    pad = jnp.zeros((V7X_LANES - GLA_GATE_RANK, kr.shape[1]), F32)
    o_ref[:, tail_col:] = jnp.concatenate([kr, kr[half:], kr[:half], gk, pad], axis=0).T.astype(BF16)


def _pack_w_in(w, d, dk_all, dv_all, q_rank, kv_rank):
    widths = (dk_all, dk_all, dv_all, dv_all, GLA_GATE_RANK, q_rank, kv_rank, MLA_ROPE, d, d)
    offs = [0]
    for wd in widths:
        offs.append(offs[-1] + wd)
    gla_w = offs[4]
    moves = ((0, 0, gla_w), (gla_w, offs[8], 2 * d), (gla_w + 2 * d, offs[5], q_rank + kv_rank))
    tail_col = gla_w + 2 * d + q_rank + kv_rank
    n_out = tail_col + 2 * V7X_LANES
    assert all(dst % V7X_LANES == 0 for dst, _, _ in moves) and tail_col % V7X_LANES == 0
    rows = w.shape[0]
    tr = _tile(rows, 128)
    kern = functools.partial(_pack_w_in_kernel, moves=moves, kr_row=offs[7], gk_row=offs[4], tail_col=tail_col)
    return pl.pallas_call(
        kern,
        grid=(rows // tr,),
        in_specs=[pl.BlockSpec((w.shape[1], tr), lambda i: (0, i))],
        out_specs=pl.BlockSpec((tr, n_out), lambda i: (i, 0)),
        out_shape=jax.ShapeDtypeStruct((rows, n_out), BF16),
        compiler_params=_params("parallel"),
        name="pack_w_in",
    )(w.T)


def _pack_wuq_t(w):
    r = w.shape[0]
    half = MLA_ROPE // 2
    w3 = w.reshape(r, MLA_HEADS, MLA_NOPE + MLA_ROPE)
    rope = w3[:, :, MLA_NOPE:]
    rot = jnp.concatenate([rope[:, :, half:], rope[:, :, :half]], axis=2)
    return jnp.concatenate([w3[:, :, :MLA_NOPE], rope, rot], axis=2).reshape(r, -1).T.astype(BF16)


def _pack_wukv(w):
    r = w.shape[0]
    w3 = w.reshape(r, MLA_HEADS, MLA_NOPE + MLA_V)
    wk = w3[:, :, :MLA_NOPE].reshape(r, -1).astype(BF16)
    wvt = w3[:, :, MLA_NOPE:].reshape(r, -1).T.astype(BF16)
    return wk, wvt


def kernel(x, c, positions, ada_w, ada_b, norm_mix_g, w_in, gla_gk_w, gla_gk_b, gla_onorm_g, gla_wo, mla_q_norm_g, mla_wuq, mla_kv_norm_g, mla_wukv, mla_wo, w_out, norm_ffn_g, ffn_w_in, ffn_w_down, final_norm_g):
    bsz, seq, d = x.shape
    depth = w_in.shape[0]
    dk_all = gla_gk_w.shape[2]
    dv_all = gla_wo.shape[1]
    dk, dv = dk_all // GLA_HEADS, dv_all // GLA_HEADS
    q_rank, kv_rank = mla_wuq.shape[1], mla_wukv.shape[1]
    lat_col = 2 * dk_all + 2 * dv_all + 2 * d
    lat_width = q_rank + kv_rank + 2 * V7X_LANES
    gate_col = 2 * dk_all + 2 * dv_all
    assert lat_col % lat_width == 0 and (lat_col + lat_width - V7X_LANES) % V7X_LANES == 0

    x2 = x.reshape(bsz * seq, d)
    cos_t, sin_t = _rope_tab(positions)
    for l in range(depth):
        mod3 = _ada(c, ada_w[l], ada_b[l]).reshape(bsz, N_MOD, d)
        w_all = _pack_w_in(w_in[l], d, dk_all, dv_all, q_rank, kv_rank)
        p = _inproj(x2, mod3, norm_mix_g[l], w_all, seq)
        wk, wvt = _pack_wukv(mla_wukv[l])
        qt, k, vt = _mlaproj(p, lat_col // lat_width, lat_width, cos_t, sin_t, mla_q_norm_g[l], mla_kv_norm_g[l],
                             _pack_wuq_t(mla_wuq[l]), wk, wvt)
        gkw = jnp.zeros((V7X_LANES, dk_all), BF16).at[:GLA_GATE_RANK].set(gla_gk_w[l].astype(BF16))
        o_gla = _gla(p, (lat_col + lat_width - V7X_LANES) // V7X_LANES, gkw, gla_gk_b[l], gla_onorm_g[l],
                     bsz, seq, dk, dv)
        o_mla = _flash(qt, k, vt, bsz, seq)
        x1 = _mixout(o_gla, o_mla, p, gate_col, x2, mod3, gla_wo[l].astype(BF16), mla_wo[l].astype(BF16),
                     w_out[l].astype(BF16), seq)
        x2 = _ffn(x1, mod3, norm_ffn_g[l], ffn_w_in[l].astype(BF16), ffn_w_down[l].astype(BF16), final_norm_g,
                  seq, final_norm=(l == depth - 1))
    return x2.reshape(bsz, seq, d)
```

```python
import functools

import jax
import jax.numpy as jnp
from jax import lax
from jax.experimental import pallas as pl
from jax.experimental.pallas import tpu as pltpu

F32 = jnp.float32
BF16 = jnp.bfloat16

GLA_HEADS = 4
GLA_GATE_RANK = 16
GLA_GATE_NORMALIZER = 16.0
MLA_HEADS = 16
MLA_NOPE = 128
MLA_ROPE = 64
MLA_V = 128
MLA_QW = MLA_NOPE + 2 * MLA_ROPE
MLA_QBLOCK = 1024
MLA_STRIP = 512
MLA_CHAIN = 512
MLA_VP = MLA_V + 16
MLA_HEAD_GROUP = 2
ROPE_THETA = 10000.0
LOG2_E = 1.4426950408889634
NORM_EPS = 1e-6
N_MOD = 6
MOD_SHIFT_MIX, MOD_SCALE_MIX, MOD_GATE_MIX, MOD_SHIFT_FFN, MOD_SCALE_FFN, MOD_GATE_FFN = range(N_MOD)

V7X_LANES = 128
V7X_VMEM_LIMIT_BYTES = 56 * 1024 * 1024

MASK_VALUE = -1e30


def _params(*semantics):
    return pltpu.CompilerParams(dimension_semantics=semantics, vmem_limit_bytes=V7X_VMEM_LIMIT_BYTES)


def _dot(a, b):
    return jnp.dot(a, b, preferred_element_type=F32)


def _dot_nt(a, b):
    return lax.dot_general(a, b, (((1,), (1,)), ((), ())), preferred_element_type=F32)


def _dot_tn(a, b):
    return lax.dot_general(a, b, (((0,), (0,)), ((), ())), preferred_element_type=F32)


def _rms(x):
    return x * lax.rsqrt(jnp.mean(x * x, axis=-1, keepdims=True) + NORM_EPS)


def _modnorm(x, gain, scale, shift):
    return _rms(x) * (gain * (1.0 + scale)) + shift


def _tile(n, want):
    t = min(n, want)
    assert n % t == 0, (n, want)
    return t


def _ada_kernel(c_ref, w_ref, b_ref, o_ref):
    c = c_ref[...]
    act = (c * jax.nn.sigmoid(c)).astype(BF16)
    o_ref[...] = _dot(act, w_ref[...].astype(BF16)) + b_ref[...]


def _ada(c, w, b):
    bsz, d = c.shape
    n = w.shape[1]
    tn = _tile(n, 1024)
    return pl.pallas_call(
        _ada_kernel,
        grid=(n // tn,),
        in_specs=[pl.BlockSpec((bsz, d), lambda j: (0, 0)),
                  pl.BlockSpec((d, tn), lambda j: (0, j)),
                  pl.BlockSpec((1, tn), lambda j: (0, j))],
        out_specs=pl.BlockSpec((bsz, tn), lambda j: (0, j)),
        out_shape=jax.ShapeDtypeStruct((bsz, n), F32),
        compiler_params=_params("arbitrary"),
        name="ada",
    )(c, w, b.reshape(1, n))


def _rope_tab_kernel(pos_ref, f_ref, cos_ref, sin_ref):
    ang = pos_ref[...].astype(F32) * f_ref[...]
    cos_ref[...] = jnp.cos(ang)
    sin_ref[...] = jnp.sin(ang)


def _rope_tab(positions):
    t = positions.size
    half = MLA_ROPE // 2
    inv_freq = ROPE_THETA ** (-jnp.arange(0, MLA_ROPE, 2, dtype=F32) / MLA_ROPE)
    tt = _tile(t, 4096)
    spec = pl.BlockSpec((half, tt), lambda i: (0, i))
    return pl.pallas_call(
        _rope_tab_kernel,
        grid=(t // tt,),
        in_specs=[pl.BlockSpec((1, tt), lambda i: (0, i)), pl.BlockSpec((half, 1), lambda i: (0, 0))],
        out_specs=[spec, spec],
        out_shape=[jax.ShapeDtypeStruct((half, t), F32)] * 2,
        compiler_params=_params("arbitrary"),
        name="rope_tab",
    )(positions.reshape(1, t), inv_freq.reshape(half, 1))


def _inproj_kernel(x_ref, mod_ref, g_ref, w_ref, o_ref, h_ref):
    @pl.when(pl.program_id(1) == 0)
    def _():
        h = _modnorm(x_ref[...], g_ref[...],
                     mod_ref[0, MOD_SCALE_MIX:MOD_SCALE_MIX + 1, :], mod_ref[0, MOD_SHIFT_MIX:MOD_SHIFT_MIX + 1, :])
        h_ref[...] = h.astype(BF16)

    o_ref[...] = _dot(h_ref[...], w_ref[...]).astype(o_ref.dtype)


def _inproj(x2, mod3, gain, w, seq):
    t, d = x2.shape
    n = w.shape[1]
    tm = _tile(seq, 1024)
    tn = _tile(n, 1280)
    return pl.pallas_call(
        _inproj_kernel,
        grid=(t // tm, n // tn),
        in_specs=[pl.BlockSpec((tm, d), lambda i, j: (i, 0)),
                  pl.BlockSpec((1, N_MOD, d), lambda i, j: ((i * tm) // seq, 0, 0)),
                  pl.BlockSpec((1, d), lambda i, j: (0, 0)),
                  pl.BlockSpec((d, tn), lambda i, j: (0, j))],
        out_specs=pl.BlockSpec((tm, tn), lambda i, j: (i, j)),
        out_shape=jax.ShapeDtypeStruct((t, n), BF16),
        scratch_shapes=[pltpu.VMEM((tm, d), BF16)],
        compiler_params=_params("parallel", "arbitrary"),
        name="inproj",
    )(x2, mod3, gain.reshape(1, d), w)


def _mlaproj_kernel(lat_ref, cos_ref, sin_ref, gq_ref, gkv_ref, wqt_ref, wk_ref, wvt_ref, qt_ref, k_ref, vt_ref, *,
                    q_rank, kv_rank, scale):
    tm = lat_ref.shape[0]
    cos = cos_ref[...]
    sin = sin_ref[...]
    cs = jnp.concatenate([cos, cos], axis=0)
    sn = jnp.concatenate([-sin, sin], axis=0)

    cqn = _rms(lat_ref[:, 0:q_rank].astype(F32)) * gq_ref[...]
    ckvn = _rms(lat_ref[:, q_rank:q_rank + kv_rank].astype(F32)) * gkv_ref[...]
    cqn_t = cqn.T.astype(BF16)
    ckvn_t = ckvn.T.astype(BF16)
    ckvn = ckvn.astype(BF16)

    kr_t = lat_ref[:, q_rank + kv_rank:q_rank + kv_rank + V7X_LANES].astype(F32).T
    k_rope_t = kr_t[:MLA_ROPE] * cs + kr_t[MLA_ROPE:] * sn
    k_rope = jnp.concatenate([k_rope_t, jnp.zeros_like(k_rope_t)], axis=0).T.astype(BF16)

    for h in range(MLA_HEADS):
        qt = _dot(wqt_ref[h * MLA_QW:(h + 1) * MLA_QW, :], cqn_t)
        q_rope = qt[MLA_NOPE:MLA_NOPE + MLA_ROPE] * cs + qt[MLA_NOPE + MLA_ROPE:] * sn
        qt_ref[h * MLA_QW:h * MLA_QW + MLA_NOPE, :] = (qt[:MLA_NOPE] * scale).astype(BF16)
        qt_ref[h * MLA_QW + MLA_NOPE:h * MLA_QW + MLA_NOPE + MLA_ROPE, :] = (q_rope * scale).astype(BF16)
        qt_ref[h * MLA_QW + MLA_NOPE + MLA_ROPE:(h + 1) * MLA_QW, :] = jnp.zeros((MLA_ROPE, tm), BF16)
        vt_ref[h, 0, :MLA_V, :] = _dot(wvt_ref[h * MLA_V:(h + 1) * MLA_V, :], ckvn_t).astype(BF16)
        vt_ref[h, 0, MLA_V:, :] = jnp.ones((MLA_VP - MLA_V, tm), BF16)
    for g in range(MLA_HEADS // 2):
        kk = _dot(ckvn, wk_ref[:, 2 * g * MLA_NOPE:2 * (g + 1) * MLA_NOPE]).astype(BF16)
        for e in range(2):
            h = 2 * g + e
            k_ref[:, h * MLA_QW:h * MLA_QW + MLA_NOPE] = kk[:, e * MLA_NOPE:(e + 1) * MLA_NOPE]
            k_ref[:, h * MLA_QW + MLA_NOPE:(h + 1) * MLA_QW] = k_rope


def _mlaproj(p, lat_block, lat_width, cos_t, sin_t, gq, gkv, wqt, wk, wvt):
    t = p.shape[0]
    q_rank, kv_rank = gq.shape[0], gkv.shape[0]
    tm = _tile(MLA_STRIP, 512)
    per_strip = MLA_STRIP // tm
    half = MLA_ROPE // 2
    scale = (MLA_NOPE + MLA_ROPE) ** -0.5 * LOG2_E
    kern = functools.partial(_mlaproj_kernel, q_rank=q_rank, kv_rank=kv_rank, scale=scale)
    return pl.pallas_call(
        kern,
        grid=(t // tm,),
        in_specs=[pl.BlockSpec((tm, lat_width), lambda i: (i, lat_block)),
                  pl.BlockSpec((half, tm), lambda i: (0, i)),
                  pl.BlockSpec((half, tm), lambda i: (0, i)),
                  pl.BlockSpec((1, q_rank), lambda i: (0, 0)),
                  pl.BlockSpec((1, kv_rank), lambda i: (0, 0)),
                  pl.BlockSpec(wqt.shape, lambda i: (0, 0)),
                  pl.BlockSpec(wk.shape, lambda i: (0, 0)),
                  pl.BlockSpec(wvt.shape, lambda i: (0, 0))],
        out_specs=[pl.BlockSpec((MLA_HEADS * MLA_QW, tm), lambda i: (0, i)),
                   pl.BlockSpec((tm, MLA_HEADS * MLA_QW), lambda i: (i, 0)),
                   pl.BlockSpec((MLA_HEADS, 1, MLA_VP, tm), lambda i: (0, i // per_strip, 0, i % per_strip))],
        out_shape=[jax.ShapeDtypeStruct((MLA_HEADS * MLA_QW, t), BF16),
                   jax.ShapeDtypeStruct((t, MLA_HEADS * MLA_QW), BF16),
                   jax.ShapeDtypeStruct((MLA_HEADS, t // MLA_STRIP, MLA_VP, MLA_STRIP), BF16)],
        compiler_params=_params("parallel"),
        name="mlaproj",
    )(p, cos_t, sin_t, gq.reshape(1, q_rank), gkv.reshape(1, kv_rank), wqt, wk, wvt)


def _gla_kernel(q_ref, k_ref, v_ref, g_ref, gk_ref, gkw_ref, gkb_ref, on_ref, o_ref, st_ref, *, dk, dv):
    c = q_ref.shape[0]

    @pl.when(pl.program_id(1) == 0)
    def _():
        st_ref[...] = jnp.zeros_like(st_ref)

    z = _dot(gk_ref[...], gkw_ref[...]) + gkb_ref[...]
    log_a = jax.nn.log_sigmoid(z) * (LOG2_E / GLA_GATE_NORMALIZER)

    halves = [1 << e for e in range(c.bit_length() - 1)]
    row = lax.broadcasted_iota(jnp.int32, (c, dk), 0)
    ai = lax.broadcasted_iota(jnp.int32, (c, c), 0)
    aj = lax.broadcasted_iota(jnp.int32, (c, c), 1)
    diff = ai ^ aj
    causal_diff = jnp.where(ai > aj, diff, 0)
    level_mask = {s: (causal_diff >> (s.bit_length() - 1)) == 1 for s in halves}
    diag_mask = ai == aj

    for h in range(GLA_HEADS):
        g = log_a[:, h * dk:(h + 1) * dk]
        prefix = g
        total = g
        decay = {}
        for s in halves:
            odd = (row & s) != 0
            sibling = jnp.where(odd, pltpu.roll(total, s, axis=0), pltpu.roll(total, c - s, axis=0))
            decay[s] = jnp.exp2(jnp.where(odd, prefix, total - prefix))
            prefix = prefix + jnp.where(odd, sibling, 0.0)
            total = total + sibling
        qh = q_ref[:, h * dk:(h + 1) * dk].astype(F32) * (dk ** -0.5)
        kh = k_ref[:, h * dk:(h + 1) * dk].astype(F32)
        vh = v_ref[:, h * dv:(h + 1) * dv]
        att = jnp.where(diag_mask, _dot_nt(qh.astype(BF16), kh.astype(BF16)), 0.0)
        for s in halves:
            a = _dot_nt((qh * decay[s]).astype(BF16), (kh * decay[s]).astype(BF16))
            att = jnp.where(level_mask[s], a, att)
        st = st_ref[h]
        o = _dot_nt((qh * jnp.exp2(prefix)).astype(BF16), st.astype(BF16)) + _dot(att.astype(BF16), vh)
        k_dec = (kh * jnp.exp2(total - prefix)).astype(BF16)
        st_ref[h] = st * jnp.exp2(total[0:1, :]) + _dot_tn(vh, k_dec)
        gate = g_ref[:, h * dv:(h + 1) * dv].astype(F32)
        o_ref[:, h * dv:(h + 1) * dv] = (_rms(o) * on_ref[...] * (gate * jax.nn.sigmoid(gate))).astype(o_ref.dtype)


def _gla(p, gk_block, gkw, gkb, onorm_g, bsz, seq, dk, dv):
    t = p.shape[0]
    c = _tile(seq, 128)
    nc = seq // c
    hk, hv = GLA_HEADS * dk, GLA_HEADS * dv
    kern = functools.partial(_gla_kernel, dk=dk, dv=dv)
    row = lambda b, i: b * nc + i
    return pl.pallas_call(
        kern,
        grid=(bsz, nc),
        in_specs=[pl.BlockSpec((c, hk), lambda b, i: (row(b, i), 0)),
                  pl.BlockSpec((c, hk), lambda b, i: (row(b, i), 1)),
                  pl.BlockSpec((c, hv), lambda b, i: (row(b, i), (2 * hk) // hv)),
                  pl.BlockSpec((c, hv), lambda b, i: (row(b, i), (2 * hk) // hv + 1)),
                  pl.BlockSpec((c, V7X_LANES), lambda b, i: (row(b, i), gk_block)),
                  pl.BlockSpec(gkw.shape, lambda b, i: (0, 0)),
                  pl.BlockSpec((1, hk), lambda b, i: (0, 0)),
                  pl.BlockSpec((1, dv), lambda b, i: (0, 0))],
        out_specs=pl.BlockSpec((c, hv), lambda b, i: (row(b, i), 0)),
        out_shape=jax.ShapeDtypeStruct((t, hv), BF16),
        scratch_shapes=[pltpu.VMEM((GLA_HEADS, dv, dk), F32)],
        compiler_params=_params("parallel", "arbitrary"),
        name="gla",
    )(p, p, p, p, p, gkw, gkb.reshape(1, hk), onorm_g.reshape(1, dv))


def _flash_kernel(qt_ref, k_ref, vt_ref, o_ref, acc_ref, sa_ref, sb_ref, pa_ref, pb_ref):
    strip, cw = MLA_STRIP, MLA_CHAIN
    hp, nch = acc_ref.shape[0], acc_ref.shape[1]
    chains = [(h, a) for h in range(hp) for a in range(nch)]
    per_block = MLA_QBLOCK // strip
    assert per_block == 2 and nch == 2 and strip == cw
    i = pl.program_id(2)
    acc_ref[...] = jnp.zeros_like(acc_ref)
    pb_ref[...] = jnp.zeros_like(pb_ref)

    def scores(h, a, strip_idx):
        start = pl.multiple_of(strip_idx * strip, strip)
        return _dot(k_ref[pl.ds(start, strip), h * MLA_QW:(h + 1) * MLA_QW],
                    qt_ref[h * MLA_QW:(h + 1) * MLA_QW, a * cw:(a + 1) * cw])

    def col_max(s):
        return jnp.max(s, axis=0, keepdims=True)

    def softmax(s, s_max, m):
        m_new = jnp.maximum(m, s_max)
        return jnp.exp2((s - m_new).astype(BF16)), jnp.exp2(m - m_new), m_new

    def accumulate(h, a, strip_idx, p, alpha):
        acc_ref[h, a] = alpha * acc_ref[h, a] + _dot(vt_ref[h, strip_idx], p)

    def half_step(s_cur, s_nxt, p_cur, p_prev, t, carry):
        out = []
        for c, (h, a) in enumerate(chains):
            m, cur_max, alpha_prev = carry[c]
            s = scores(h, a, t + 1)
            s_nxt[h, a] = s
            accumulate(h, a, jnp.maximum(t - 1, 0), p_prev[h, a], alpha_prev)
            p, alpha, m = softmax(s_cur[h, a], cur_max, m)
            p_cur[h, a] = p
            out.append((m, col_max(s), alpha))
        return tuple(out)

    def body(jj, carry):
        carry = half_step(sa_ref, sb_ref, pa_ref, pb_ref, 2 * jj, carry)
        return half_step(sb_ref, sa_ref, pb_ref, pa_ref, 2 * jj + 1, carry)

    init = []
    for h, a in chains:
        s = scores(h, a, 0)
        sa_ref[h, a] = s
        init.append((jnp.full((1, cw), MASK_VALUE, F32), col_max(s), jnp.ones((1, cw), F32)))
    carry = lax.fori_loop(0, i, body, tuple(init))
    first = per_block * i
    causal = lax.broadcasted_iota(jnp.int32, (strip, cw), 0) <= lax.broadcasted_iota(jnp.int32, (strip, cw), 1)
    last = [scores(h, 1, first + 1) for h in range(hp)]

    def update(h, a, strip_idx, s, s_max, m):
        p, alpha, m = softmax(s, s_max, m)
        accumulate(h, a, strip_idx, p, alpha)
        return m

    for c, (h, a) in enumerate(chains):
        m, s_max, alpha_prev = carry[c]
        accumulate(h, a, jnp.maximum(first - 1, 0), pb_ref[h, a], alpha_prev)
        if a == 0:
            s = jnp.where(causal, sa_ref[h, a], MASK_VALUE)
            update(h, a, first, s, col_max(s), m)
        else:
            m = update(h, a, first, sa_ref[h, a], s_max, m)
            s = jnp.where(causal, last[h], MASK_VALUE)
            update(h, a, first + 1, s, col_max(s), m)
        o_t = acc_ref[h, a, :MLA_V, :] / acc_ref[h, a, MLA_V:MLA_V + 1, :]
        o_ref[a * cw:(a + 1) * cw, h * MLA_V:(h + 1) * MLA_V] = o_t.T.astype(o_ref.dtype)


def _flash(qt, k, vt, bsz, seq):
    t = k.shape[0]
    assert seq % MLA_QBLOCK == 0
    nq = seq // MLA_QBLOCK
    hp = MLA_HEAD_GROUP
    nch = MLA_QBLOCK // MLA_CHAIN
    return pl.pallas_call(
        _flash_kernel,
        grid=(bsz, MLA_HEADS // hp, nq),
        in_specs=[pl.BlockSpec((hp * MLA_QW, MLA_QBLOCK), lambda b, g, i: (g, b * nq + i)),
                  pl.BlockSpec((seq, hp * MLA_QW), lambda b, g, i: (b, g)),
                  pl.BlockSpec((hp, seq // MLA_STRIP, MLA_VP, MLA_STRIP), lambda b, g, i: (g, b, 0, 0))],
        out_specs=pl.BlockSpec((MLA_QBLOCK, hp * MLA_V), lambda b, g, i: (b * nq + i, g)),
        out_shape=jax.ShapeDtypeStruct((t, MLA_HEADS * MLA_V), BF16),
        scratch_shapes=[pltpu.VMEM((hp, nch, MLA_VP, MLA_CHAIN), F32),
                        pltpu.VMEM((hp, nch, MLA_STRIP, MLA_CHAIN), F32),
                        pltpu.VMEM((hp, nch, MLA_STRIP, MLA_CHAIN), F32),
                        pltpu.VMEM((hp, nch, MLA_STRIP, MLA_CHAIN), BF16),
                        pltpu.VMEM((hp, nch, MLA_STRIP, MLA_CHAIN), BF16)],
        compiler_params=_params("parallel", "parallel", "arbitrary"),
        name="flash",
    )(qt, k, vt)


def _mixout_kernel(og_ref, om_ref, ga_ref, gb_ref, x_ref, mod_ref, wg_ref, wm_ref, wo_ref, o_ref):
    yg = _dot(og_ref[...], wg_ref[...])
    ym = _dot(om_ref[...], wm_ref[...])
    ga = jax.nn.sigmoid(ga_ref[...].astype(F32))
    gb = jax.nn.sigmoid(gb_ref[...].astype(F32))
    merged = (ga * yg + gb * ym).astype(BF16)
    o_ref[...] = x_ref[...] + mod_ref[0, MOD_GATE_MIX:MOD_GATE_MIX + 1, :] * _dot(merged, wo_ref[...])


def _mixout(o_gla, o_mla, p, gate_col, x2, mod3, w_gla, w_mla, w_out, seq):
    t, d = x2.shape
    tm = _tile(seq, 256)
    assert gate_col % d == 0
    row = lambda i: (i, 0)
    resident = lambda w: pl.BlockSpec(w.shape, lambda i: (0, 0), pipeline_mode=pl.Buffered(1))
    return pl.pallas_call(
        _mixout_kernel,
        grid=(t // tm,),
        in_specs=[pl.BlockSpec((tm, o_gla.shape[1]), row),
                  pl.BlockSpec((tm, o_mla.shape[1]), row),
                  pl.BlockSpec((tm, d), lambda i: (i, gate_col // d)),
                  pl.BlockSpec((tm, d), lambda i: (i, gate_col // d + 1)),
                  pl.BlockSpec((tm, d), row),
                  pl.BlockSpec((1, N_MOD, d), lambda i: ((i * tm) // seq, 0, 0)),
                  resident(w_gla), resident(w_mla), resident(w_out)],
        out_specs=pl.BlockSpec((tm, d), row),
        out_shape=jax.ShapeDtypeStruct((t, d), F32),
        compiler_params=_params("parallel"),
        name="mixout",
    )(o_gla, o_mla, p, p, x2, mod3, w_gla, w_mla, w_out)


def _ffn_kernel(x_ref, mod_ref, g_ref, wg_ref, wu_ref, wd_ref, fg_ref, o_ref, h_ref, acc_ref, *, final_norm):
    f = pl.program_id(1)

    @pl.when(f == 0)
    def _():
        h = _modnorm(x_ref[...], g_ref[...],
                     mod_ref[0, MOD_SCALE_FFN:MOD_SCALE_FFN + 1, :], mod_ref[0, MOD_SHIFT_FFN:MOD_SHIFT_FFN + 1, :])
        h_ref[...] = h.astype(BF16)
        acc_ref[...] = jnp.zeros_like(acc_ref)

    h = h_ref[...]
    gate = _dot(h, wg_ref[...])
    up = _dot(h, wu_ref[...])
    act = (gate * jax.nn.sigmoid(gate) * up).astype(BF16)
    acc_ref[...] += _dot(act, wd_ref[...])

    @pl.when(f == pl.num_programs(1) - 1)
    def _():
        y = x_ref[...] + mod_ref[0, MOD_GATE_FFN:MOD_GATE_FFN + 1, :] * acc_ref[...]
        if final_norm:
            y = _rms(y) * fg_ref[...]
        o_ref[...] = y


def _ffn(x1, mod3, gain, w_in, w_down, final_g, seq, final_norm):
    t, d = x1.shape
    hidden = w_down.shape[0]
    tm = _tile(seq, 512)
    tf = _tile(hidden, 512)
    nf = hidden // tf
    kern = functools.partial(_ffn_kernel, final_norm=final_norm)
    return pl.pallas_call(
        kern,
        grid=(t // tm, nf),
        in_specs=[pl.BlockSpec((tm, d), lambda i, f: (i, 0)),
                  pl.BlockSpec((1, N_MOD, d), lambda i, f: ((i * tm) // seq, 0, 0)),
                  pl.BlockSpec((1, d), lambda i, f: (0, 0)),
                  pl.BlockSpec((d, tf), lambda i, f: (0, f)),
                  pl.BlockSpec((d, tf), lambda i, f: (0, nf + f)),
                  pl.BlockSpec((tf, d), lambda i, f: (f, 0)),
                  pl.BlockSpec((1, d), lambda i, f: (0, 0))],
        out_specs=pl.BlockSpec((tm, d), lambda i, f: (i, 0)),
        out_shape=jax.ShapeDtypeStruct((t, d), F32),
        scratch_shapes=[pltpu.VMEM((tm, d), BF16), pltpu.VMEM((tm, d), F32)],
        compiler_params=_params("parallel", "arbitrary"),
        name="ffn",
    )(x1, mod3, gain.reshape(1, d), w_in, w_in, w_down, final_g.reshape(1, d))


def _pack_w_in_kernel(wt_ref, o_ref, *, moves, kr_row, gk_row, tail_col):
    for dst, src, width in moves:
        o_ref[:, dst:dst + width] = wt_ref[src:src + width, :].T.astype(BF16)
    half = MLA_ROPE // 2
    kr = wt_ref[kr_row:kr_row + MLA_ROPE, :]
    gk = wt_ref[gk_row:gk_row + GLA_GATE_RANK, :]
    pad = jnp.zeros((V7X_LANES - GLA_GATE_RANK, kr.shape[1]), F32)
    o_ref[:, tail_col:] = jnp.concatenate([kr, kr[half:], kr[:half], gk, pad], axis=0).T.astype(BF16)


def _pack_w_in(w, d, dk_all, dv_all, q_rank, kv_rank):
    widths = (dk_all, dk_all, dv_all, dv_all, GLA_GATE_RANK, q_rank, kv_rank, MLA_ROPE, d, d)
    offs = [0]
    for wd in widths:
        offs.append(offs[-1] + wd)
    gla_w = offs[4]
    moves = ((0, 0, gla_w), (gla_w, offs[8], 2 * d), (gla_w + 2 * d, offs[5], q_rank + kv_rank))
    tail_col = gla_w + 2 * d + q_rank + kv_rank
    n_out = tail_col + 2 * V7X_LANES
    assert all(dst % V7X_LANES == 0 for dst, _, _ in moves) and tail_col % V7X_LANES == 0
    rows = w.shape[0]
    tr = _tile(rows, 128)
    kern = functools.partial(_pack_w_in_kernel, moves=moves, kr_row=offs[7], gk_row=offs[4], tail_col=tail_col)
    return pl.pallas_call(
        kern,
        grid=(rows // tr,),
        in_specs=[pl.BlockSpec((w.shape[1], tr), lambda i: (0, i))],
        out_specs=pl.BlockSpec((tr, n_out), lambda i: (i, 0)),
        out_shape=jax.ShapeDtypeStruct((rows, n_out), BF16),
        compiler_params=_params("parallel"),
        name="pack_w_in",
    )(w.T)


def _pack_wuq_t(w):
    r = w.shape[0]
    half = MLA_ROPE // 2
    w3 = w.reshape(r, MLA_HEADS, MLA_NOPE + MLA_ROPE)
    rope = w3[:, :, MLA_NOPE:]
    rot = jnp.concatenate([rope[:, :, half:], rope[:, :, :half]], axis=2)
    return jnp.concatenate([w3[:, :, :MLA_NOPE], rope, rot], axis=2).reshape(r, -1).T.astype(BF16)


def _pack_wukv(w):
    r = w.shape[0]
    w3 = w.reshape(r, MLA_HEADS, MLA_NOPE + MLA_V)
    wk = w3[:, :, :MLA_NOPE].reshape(r, -1).astype(BF16)
    wvt = w3[:, :, MLA_NOPE:].reshape(r, -1).T.astype(BF16)
    return wk, wvt


def kernel(x, c, positions, ada_w, ada_b, norm_mix_g, w_in, gla_gk_w, gla_gk_b, gla_onorm_g, gla_wo, mla_q_norm_g, mla_wuq, mla_kv_norm_g, mla_wukv, mla_wo, w_out, norm_ffn_g, ffn_w_in, ffn_w_down, final_norm_g):
    bsz, seq, d = x.shape
    depth = w_in.shape[0]
    dk_all = gla_gk_w.shape[2]
    dv_all = gla_wo.shape[1]
    dk, dv = dk_all // GLA_HEADS, dv_all // GLA_HEADS
    q_rank, kv_rank = mla_wuq.shape[1], mla_wukv.shape[1]
    lat_col = 2 * dk_all + 2 * dv_all + 2 * d
    lat_width = q_rank + kv_rank + 2 * V7X_LANES
    gate_col = 2 * dk_all + 2 * dv_all
    assert lat_col % lat_width == 0 and (lat_col + lat_width - V7X_LANES) % V7X_LANES == 0

    x2 = x.reshape(bsz * seq, d)
    cos_t, sin_t = _rope_tab(positions)
    for l in range(depth):
        mod3 = _ada(c, ada_w[l], ada_b[l]).reshape(bsz, N_MOD, d)
        w_all = _pack_w_in(w_in[l], d, dk_all, dv_all, q_rank, kv_rank)
        p = _inproj(x2, mod3, norm_mix_g[l], w_all, seq)
        wk, wvt = _pack_wukv(mla_wukv[l])
        qt, k, vt = _mlaproj(p, lat_col // lat_width, lat_width, cos_t, sin_t, mla_q_norm_g[l], mla_kv_norm_g[l],
                             _pack_wuq_t(mla_wuq[l]), wk, wvt)
        gkw = jnp.zeros((V7X_LANES, dk_all), BF16).at[:GLA_GATE_RANK].set(gla_gk_w[l].astype(BF16))
        o_gla = _gla(p, (lat_col + lat_width - V7X_LANES) // V7X_LANES, gkw, gla_gk_b[l], gla_onorm_g[l],
                     bsz, seq, dk, dv)
        o_mla = _flash(qt, k, vt, bsz, seq)
        x1 = _mixout(o_gla, o_mla, p, gate_col, x2, mod3, gla_wo[l].astype(BF16), mla_wo[l].astype(BF16),
                     w_out[l].astype(BF16), seq)
        x2 = _ffn(x1, mod3, norm_ffn_g[l], ffn_w_in[l].astype(BF16), ffn_w_down[l].astype(BF16), final_norm_g,
                  seq, final_norm=(l == depth - 1))
    return x2.reshape(bsz, seq, d)
```

```python
import functools

import jax
import jax.numpy as jnp
from jax import lax
from jax.experimental import pallas as pl
from jax.experimental.pallas import tpu as pltpu

F32 = jnp.float32
BF16 = jnp.bfloat16

GLA_HEADS = 4
GLA_GATE_RANK = 16
GLA_GATE_NORMALIZER = 16.0
MLA_HEADS = 16
MLA_NOPE = 128
MLA_ROPE = 64
MLA_V = 128
MLA_QW = MLA_NOPE + 2 * MLA_ROPE
MLA_QBLOCK = 1024
MLA_STRIP = 512
MLA_CHAIN = 512
MLA_VP = MLA_V + 16
MLA_HEAD_GROUP = 2
ROPE_THETA = 10000.0
LOG2_E = 1.4426950408889634
NORM_EPS = 1e-6
N_MOD = 6
MOD_SHIFT_MIX, MOD_SCALE_MIX, MOD_GATE_MIX, MOD_SHIFT_FFN, MOD_SCALE_FFN, MOD_GATE_FFN = range(N_MOD)

V7X_LANES = 128
V7X_VMEM_LIMIT_BYTES = 56 * 1024 * 1024

MASK_VALUE = -1e30


def _params(*semantics):
    return pltpu.CompilerParams(dimension_semantics=semantics, vmem_limit_bytes=V7X_VMEM_LIMIT_BYTES)


def _dot(a, b):
    return jnp.dot(a, b, preferred_element_type=F32)


def _dot_nt(a, b):
    return lax.dot_general(a, b, (((1,), (1,)), ((), ())), preferred_element_type=F32)


def _dot_tn(a, b):
    return lax.dot_general(a, b, (((0,), (0,)), ((), ())), preferred_element_type=F32)


def _rms(x):
    return x * lax.rsqrt(jnp.mean(x * x, axis=-1, keepdims=True) + NORM_EPS)


def _modnorm(x, gain, scale, shift):
    return _rms(x) * (gain * (1.0 + scale)) + shift


def _tile(n, want):
    t = min(n, want)
    assert n % t == 0, (n, want)
    return t


def _ada_kernel(c_ref, w_ref, b_ref, o_ref):
    c = c_ref[...]
    act = (c * jax.nn.sigmoid(c)).astype(BF16)
    o_ref[...] = _dot(act, w_ref[...].astype(BF16)) + b_ref[...]


def _ada(c, w, b):
    bsz, d = c.shape
    n = w.shape[1]
    tn = _tile(n, 1024)
    return pl.pallas_call(
        _ada_kernel,
        grid=(n // tn,),
        in_specs=[pl.BlockSpec((bsz, d), lambda j: (0, 0)),
                  pl.BlockSpec((d, tn), lambda j: (0, j)),
                  pl.BlockSpec((1, tn), lambda j: (0, j))],
        out_specs=pl.BlockSpec((bsz, tn), lambda j: (0, j)),
        out_shape=jax.ShapeDtypeStruct((bsz, n), F32),
        compiler_params=_params("arbitrary"),
        name="ada",
    )(c, w, b.reshape(1, n))


def _rope_tab_kernel(pos_ref, f_ref, cos_ref, sin_ref):
    ang = pos_ref[...].astype(F32) * f_ref[...]
    cos_ref[...] = jnp.cos(ang)
    sin_ref[...] = jnp.sin(ang)


def _rope_tab(positions):
    t = positions.size
    half = MLA_ROPE // 2
    inv_freq = ROPE_THETA ** (-jnp.arange(0, MLA_ROPE, 2, dtype=F32) / MLA_ROPE)
    tt = _tile(t, 4096)
    spec = pl.BlockSpec((half, tt), lambda i: (0, i))
    return pl.pallas_call(
        _rope_tab_kernel,
        grid=(t // tt,),
        in_specs=[pl.BlockSpec((1, tt), lambda i: (0, i)), pl.BlockSpec((half, 1), lambda i: (0, 0))],
        out_specs=[spec, spec],
        out_shape=[jax.ShapeDtypeStruct((half, t), F32)] * 2,
        compiler_params=_params("arbitrary"),
        name="rope_tab",
    )(positions.reshape(1, t), inv_freq.reshape(half, 1))


def _inproj_kernel(x_ref, mod_ref, g_ref, w_ref, o_ref, h_ref):
    @pl.when(pl.program_id(1) == 0)
    def _():
        h = _modnorm(x_ref[...], g_ref[...],
                     mod_ref[0, MOD_SCALE_MIX:MOD_SCALE_MIX + 1, :], mod_ref[0, MOD_SHIFT_MIX:MOD_SHIFT_MIX + 1, :])
        h_ref[...] = h.astype(BF16)

    o_ref[...] = _dot(h_ref[...], w_ref[...]).astype(o_ref.dtype)


def _inproj(x2, mod3, gain, w, seq):
    t, d = x2.shape
    n = w.shape[1]
    tm = _tile(seq, 1024)
    tn = _tile(n, 1280)
    return pl.pallas_call(
        _inproj_kernel,
        grid=(t // tm, n // tn),
        in_specs=[pl.BlockSpec((tm, d), lambda i, j: (i, 0)),
                  pl.BlockSpec((1, N_MOD, d), lambda i, j: ((i * tm) // seq, 0, 0)),
                  pl.BlockSpec((1, d), lambda i, j: (0, 0)),
                  pl.BlockSpec((d, tn), lambda i, j: (0, j))],
        out_specs=pl.BlockSpec((tm, tn), lambda i, j: (i, j)),
        out_shape=jax.ShapeDtypeStruct((t, n), BF16),
        scratch_shapes=[pltpu.VMEM((tm, d), BF16)],
        compiler_params=_params("parallel", "arbitrary"),
        name="inproj",
    )(x2, mod3, gain.reshape(1, d), w)


def _mlaproj_kernel(lat_ref, cos_ref, sin_ref, gq_ref, gkv_ref, wqt_ref, wk_ref, wvt_ref, qt_ref, k_ref, vt_ref, *,
                    q_rank, kv_rank, scale):
    tm = lat_ref.shape[0]
    cos = cos_ref[...]
    sin = sin_ref[...]
    cs = jnp.concatenate([cos, cos], axis=0)
    sn = jnp.concatenate([-sin, sin], axis=0)

    cqn = _rms(lat_ref[:, 0:q_rank].astype(F32)) * gq_ref[...]
    ckvn = _rms(lat_ref[:, q_rank:q_rank + kv_rank].astype(F32)) * gkv_ref[...]
    cqn_t = cqn.T.astype(BF16)
    ckvn_t = ckvn.T.astype(BF16)
    ckvn = ckvn.astype(BF16)

    kr_t = lat_ref[:, q_rank + kv_rank:q_rank + kv_rank + V7X_LANES].astype(F32).T
    k_rope_t = kr_t[:MLA_ROPE] * cs + kr_t[MLA_ROPE:] * sn
    k_rope = jnp.concatenate([k_rope_t, jnp.zeros_like(k_rope_t)], axis=0).T.astype(BF16)

    for h in range(MLA_HEADS):
        qt = _dot(wqt_ref[h * MLA_QW:(h + 1) * MLA_QW, :], cqn_t)
        q_rope = qt[MLA_NOPE:MLA_NOPE + MLA_ROPE] * cs + qt[MLA_NOPE + MLA_ROPE:] * sn
        qt_ref[h * MLA_QW:h * MLA_QW + MLA_NOPE, :] = (qt[:MLA_NOPE] * scale).astype(BF16)
        qt_ref[h * MLA_QW + MLA_NOPE:h * MLA_QW + MLA_NOPE + MLA_ROPE, :] = (q_rope * scale).astype(BF16)
        qt_ref[h * MLA_QW + MLA_NOPE + MLA_ROPE:(h + 1) * MLA_QW, :] = jnp.zeros((MLA_ROPE, tm), BF16)
        vt_ref[h, 0, :MLA_V, :] = _dot(wvt_ref[h * MLA_V:(h + 1) * MLA_V, :], ckvn_t).astype(BF16)
        vt_ref[h, 0, MLA_V:, :] = jnp.ones((MLA_VP - MLA_V, tm), BF16)
    for g in range(MLA_HEADS // 2):
        kk = _dot(ckvn, wk_ref[:, 2 * g * MLA_NOPE:2 * (g + 1) * MLA_NOPE]).astype(BF16)
        for e in range(2):
            h = 2 * g + e
            k_ref[:, h * MLA_QW:h * MLA_QW + MLA_NOPE] = kk[:, e * MLA_NOPE:(e + 1) * MLA_NOPE]
            k_ref[:, h * MLA_QW + MLA_NOPE:(h + 1) * MLA_QW] = k_rope


def _mlaproj(p, lat_block, lat_width, cos_t, sin_t, gq, gkv, wqt, wk, wvt):
    t = p.shape[0]
    q_rank, kv_rank = gq.shape[0], gkv.shape[0]
    tm = _tile(MLA_STRIP, 512)
    per_strip = MLA_STRIP // tm
    half = MLA_ROPE // 2
    scale = (MLA_NOPE + MLA_ROPE) ** -0.5 * LOG2_E
    kern = functools.partial(_mlaproj_kernel, q_rank=q_rank, kv_rank=kv_rank, scale=scale)
    return pl.pallas_call(
        kern,
        grid=(t // tm,),
        in_specs=[pl.BlockSpec((tm, lat_width), lambda i: (i, lat_block)),
                  pl.BlockSpec((half, tm), lambda i: (0, i)),
                  pl.BlockSpec((half, tm), lambda i: (0, i)),
                  pl.BlockSpec((1, q_rank), lambda i: (0, 0)),
                  pl.BlockSpec((1, kv_rank), lambda i: (0, 0)),
                  pl.BlockSpec(wqt.shape, lambda i: (0, 0)),
                  pl.BlockSpec(wk.shape, lambda i: (0, 0)),
                  pl.BlockSpec(wvt.shape, lambda i: (0, 0))],
        out_specs=[pl.BlockSpec((MLA_HEADS * MLA_QW, tm), lambda i: (0, i)),
                   pl.BlockSpec((tm, MLA_HEADS * MLA_QW), lambda i: (i, 0)),
                   pl.BlockSpec((MLA_HEADS, 1, MLA_VP, tm), lambda i: (0, i // per_strip, 0, i % per_strip))],
        out_shape=[jax.ShapeDtypeStruct((MLA_HEADS * MLA_QW, t), BF16),
                   jax.ShapeDtypeStruct((t, MLA_HEADS * MLA_QW), BF16),
                   jax.ShapeDtypeStruct((MLA_HEADS, t // MLA_STRIP, MLA_VP, MLA_STRIP), BF16)],
        compiler_params=_params("parallel"),
        name="mlaproj",
    )(p, cos_t, sin_t, gq.reshape(1, q_rank), gkv.reshape(1, kv_rank), wqt, wk, wvt)


def _gla_kernel(q_ref, k_ref, v_ref, g_ref, gk_ref, gkw_ref, gkb_ref, on_ref, o_ref, st_ref, *, dk, dv):
    c = q_ref.shape[0]

    @pl.when(pl.program_id(1) == 0)
    def _():
        st_ref[...] = jnp.zeros_like(st_ref)

    z = _dot(gk_ref[...], gkw_ref[...]) + gkb_ref[...]
    log_a = jax.nn.log_sigmoid(z) * (LOG2_E / GLA_GATE_NORMALIZER)

    halves = [1 << e for e in range(c.bit_length() - 1)]
    row = lax.broadcasted_iota(jnp.int32, (c, dk), 0)
    ai = lax.broadcasted_iota(jnp.int32, (c, c), 0)
    aj = lax.broadcasted_iota(jnp.int32, (c, c), 1)
    diff = ai ^ aj
    causal_diff = jnp.where(ai > aj, diff, 0)
    level_mask = {s: (causal_diff >> (s.bit_length() - 1)) == 1 for s in halves}
    diag_mask = ai == aj

    for h in range(GLA_HEADS):
        g = log_a[:, h * dk:(h + 1) * dk]
        prefix = g
        total = g
        decay = {}
        for s in halves:
            odd = (row & s) != 0
            sibling = jnp.where(odd, pltpu.roll(total, s, axis=0), pltpu.roll(total, c - s, axis=0))
            decay[s] = jnp.exp2(jnp.where(odd, prefix, total - prefix).astype(BF16))
            prefix = prefix + jnp.where(odd, sibling, 0.0)
            total = total + sibling
        q16 = q_ref[:, h * dk:(h + 1) * dk] * (dk ** -0.5)
        k16 = k_ref[:, h * dk:(h + 1) * dk]
        vh = v_ref[:, h * dv:(h + 1) * dv]
        att = jnp.where(diag_mask, _dot_nt(q16, k16), 0.0)
        for s in halves:
            att = jnp.where(level_mask[s], _dot_nt(q16 * decay[s], k16 * decay[s]), att)
        qh = q16.astype(F32)
        kh = k16.astype(F32)
        st = st_ref[h]
        o = _dot_nt((qh * jnp.exp2(prefix)).astype(BF16), st.astype(BF16)) + _dot(att.astype(BF16), vh)
        k_dec = (kh * jnp.exp2(total - prefix)).astype(BF16)
        st_ref[h] = st * jnp.exp2(total[0:1, :]) + _dot_tn(vh, k_dec)
        gate = g_ref[:, h * dv:(h + 1) * dv].astype(F32)
        o_ref[:, h * dv:(h + 1) * dv] = (_rms(o) * on_ref[...] * (gate * jax.nn.sigmoid(gate))).astype(o_ref.dtype)


def _gla(p, gk_block, gkw, gkb, onorm_g, bsz, seq, dk, dv):
    t = p.shape[0]
    c = _tile(seq, 128)
    nc = seq // c
    hk, hv = GLA_HEADS * dk, GLA_HEADS * dv
    kern = functools.partial(_gla_kernel, dk=dk, dv=dv)
    row = lambda b, i: b * nc + i
    return pl.pallas_call(
        kern,
        grid=(bsz, nc),
        in_specs=[pl.BlockSpec((c, hk), lambda b, i: (row(b, i), 0)),
                  pl.BlockSpec((c, hk), lambda b, i: (row(b, i), 1)),
                  pl.BlockSpec((c, hv), lambda b, i: (row(b, i), (2 * hk) // hv)),
                  pl.BlockSpec((c, hv), lambda b, i: (row(b, i), (2 * hk) // hv + 1)),
                  pl.BlockSpec((c, V7X_LANES), lambda b, i: (row(b, i), gk_block)),
                  pl.BlockSpec(gkw.shape, lambda b, i: (0, 0)),
                  pl.BlockSpec((1, hk), lambda b, i: (0, 0)),
                  pl.BlockSpec((1, dv), lambda b, i: (0, 0))],
        out_specs=pl.BlockSpec((c, hv), lambda b, i: (row(b, i), 0)),
        out_shape=jax.ShapeDtypeStruct((t, hv), BF16),
        scratch_shapes=[pltpu.VMEM((GLA_HEADS, dv, dk), F32)],
        compiler_params=_params("parallel", "arbitrary"),
        name="gla",
    )(p, p, p, p, p, gkw, gkb.reshape(1, hk), onorm_g.reshape(1, dv))


def _flash_kernel(qt_ref, k_ref, vt_ref, o_ref, acc_ref, sa_ref, sb_ref, pa_ref, pb_ref):
    strip, cw = MLA_STRIP, MLA_CHAIN
    hp, nch = acc_ref.shape[0], acc_ref.shape[1]
    chains = [(h, a) for h in range(hp) for a in range(nch)]
    per_block = MLA_QBLOCK // strip
    assert per_block == 2 and nch == 2 and strip == cw
    i = pl.program_id(2)
    acc_ref[...] = jnp.zeros_like(acc_ref)
    pb_ref[...] = jnp.zeros_like(pb_ref)

    def scores(h, a, strip_idx):
        start = pl.multiple_of(strip_idx * strip, strip)
        return _dot(k_ref[pl.ds(start, strip), h * MLA_QW:(h + 1) * MLA_QW],
                    qt_ref[h * MLA_QW:(h + 1) * MLA_QW, a * cw:(a + 1) * cw])

    def col_max(s):
        return jnp.max(s, axis=0, keepdims=True)

    def softmax(s, s_max, m):
        m_new = jnp.maximum(m, s_max)
        return jnp.exp2((s - m_new).astype(BF16)), jnp.exp2(m - m_new), m_new

    def accumulate(h, a, strip_idx, p, alpha):
        acc_ref[h, a] = alpha * acc_ref[h, a] + _dot(vt_ref[h, strip_idx], p)

    def half_step(s_cur, s_nxt, p_cur, p_prev, t, carry):
        out = []
        for c, (h, a) in enumerate(chains):
            m, cur_max, alpha_prev = carry[c]
            s = scores(h, a, t + 1)
            s_nxt[h, a] = s
            accumulate(h, a, jnp.maximum(t - 1, 0), p_prev[h, a], alpha_prev)
            p, alpha, m = softmax(s_cur[h, a], cur_max, m)
            p_cur[h, a] = p
            out.append((m, col_max(s), alpha))
        return tuple(out)

    def body(jj, carry):
        carry = half_step(sa_ref, sb_ref, pa_ref, pb_ref, 2 * jj, carry)
        return half_step(sb_ref, sa_ref, pb_ref, pa_ref, 2 * jj + 1, carry)

    init = []
    for h, a in chains:
        s = scores(h, a, 0)
        sa_ref[h, a] = s
        init.append((jnp.full((1, cw), MASK_VALUE, F32), col_max(s), jnp.ones((1, cw), F32)))
    carry = lax.fori_loop(0, i, body, tuple(init))
    first = per_block * i
    causal = lax.broadcasted_iota(jnp.int32, (strip, cw), 0) <= lax.broadcasted_iota(jnp.int32, (strip, cw), 1)
    last = [scores(h, 1, first + 1) for h in range(hp)]

    def update(h, a, strip_idx, s, s_max, m):
        p, alpha, m = softmax(s, s_max, m)
        accumulate(h, a, strip_idx, p, alpha)
        return m

    for c, (h, a) in enumerate(chains):
        m, s_max, alpha_prev = carry[c]
        accumulate(h, a, jnp.maximum(first - 1, 0), pb_ref[h, a], alpha_prev)
        if a == 0:
            s = jnp.where(causal, sa_ref[h, a], MASK_VALUE)
            update(h, a, first, s, col_max(s), m)
        else:
            m = update(h, a, first, sa_ref[h, a], s_max, m)
            s = jnp.where(causal, last[h], MASK_VALUE)
            update(h, a, first + 1, s, col_max(s), m)
        o_t = acc_ref[h, a, :MLA_V, :] / acc_ref[h, a, MLA_V:MLA_V + 1, :]
        o_ref[a * cw:(a + 1) * cw, h * MLA_V:(h + 1) * MLA_V] = o_t.T.astype(o_ref.dtype)


def _flash(qt, k, vt, bsz, seq):
    t = k.shape[0]
    assert seq % MLA_QBLOCK == 0
    nq = seq // MLA_QBLOCK
    hp = MLA_HEAD_GROUP
    nch = MLA_QBLOCK // MLA_CHAIN
    return pl.pallas_call(
        _flash_kernel,
        grid=(bsz, MLA_HEADS // hp, nq),
        in_specs=[pl.BlockSpec((hp * MLA_QW, MLA_QBLOCK), lambda b, g, i: (g, b * nq + i)),
                  pl.BlockSpec((seq, hp * MLA_QW), lambda b, g, i: (b, g)),
                  pl.BlockSpec((hp, seq // MLA_STRIP, MLA_VP, MLA_STRIP), lambda b, g, i: (g, b, 0, 0))],
        out_specs=pl.BlockSpec((MLA_QBLOCK, hp * MLA_V), lambda b, g, i: (b * nq + i, g)),
        out_shape=jax.ShapeDtypeStruct((t, MLA_HEADS * MLA_V), BF16),
        scratch_shapes=[pltpu.VMEM((hp, nch, MLA_VP, MLA_CHAIN), F32),
                        pltpu.VMEM((hp, nch, MLA_STRIP, MLA_CHAIN), F32),
                        pltpu.VMEM((hp, nch, MLA_STRIP, MLA_CHAIN), F32),
                        pltpu.VMEM((hp, nch, MLA_STRIP, MLA_CHAIN), BF16),
                        pltpu.VMEM((hp, nch, MLA_STRIP, MLA_CHAIN), BF16)],
        compiler_params=_params("parallel", "parallel", "arbitrary"),
        name="flash",
    )(qt, k, vt)


def _mixout_kernel(og_ref, om_ref, ga_ref, gb_ref, x_ref, mod_ref, wg_ref, wm_ref, wo_ref, o_ref):
    yg = _dot(og_ref[...], wg_ref[...])
    ym = _dot(om_ref[...], wm_ref[...])
    ga = jax.nn.sigmoid(ga_ref[...].astype(F32))
    gb = jax.nn.sigmoid(gb_ref[...].astype(F32))
    merged = (ga * yg + gb * ym).astype(BF16)
    o_ref[...] = x_ref[...] + mod_ref[0, MOD_GATE_MIX:MOD_GATE_MIX + 1, :] * _dot(merged, wo_ref[...])


def _mixout(o_gla, o_mla, p, gate_col, x2, mod3, w_gla, w_mla, w_out, seq):
    t, d = x2.shape
    tm = _tile(seq, 256)
    assert gate_col % d == 0
    row = lambda i: (i, 0)
    resident = lambda w: pl.BlockSpec(w.shape, lambda i: (0, 0), pipeline_mode=pl.Buffered(1))
    return pl.pallas_call(
        _mixout_kernel,
        grid=(t // tm,),
        in_specs=[pl.BlockSpec((tm, o_gla.shape[1]), row),
                  pl.BlockSpec((tm, o_mla.shape[1]), row),
                  pl.BlockSpec((tm, d), lambda i: (i, gate_col // d)),
                  pl.BlockSpec((tm, d), lambda i: (i, gate_col // d + 1)),
                  pl.BlockSpec((tm, d), row),
                  pl.BlockSpec((1, N_MOD, d), lambda i: ((i * tm) // seq, 0, 0)),
                  resident(w_gla), resident(w_mla), resident(w_out)],
        out_specs=pl.BlockSpec((tm, d), row),
        out_shape=jax.ShapeDtypeStruct((t, d), F32),
        compiler_params=_params("parallel"),
        name="mixout",
    )(o_gla, o_mla, p, p, x2, mod3, w_gla, w_mla, w_out)


def _ffn_kernel(x_ref, mod_ref, g_ref, wg_ref, wu_ref, wd_ref, fg_ref, o_ref, h_ref, acc_ref, *, final_norm):
    f = pl.program_id(1)

    @pl.when(f == 0)
    def _():
        h = _modnorm(x_ref[...], g_ref[...],
                     mod_ref[0, MOD_SCALE_FFN:MOD_SCALE_FFN + 1, :], mod_ref[0, MOD_SHIFT_FFN:MOD_SHIFT_FFN + 1, :])
        h_ref[...] = h.astype(BF16)
        acc_ref[...] = jnp.zeros_like(acc_ref)

    h = h_ref[...]
    gate = _dot(h, wg_ref[...])
    up = _dot(h, wu_ref[...])
    act = (gate * jax.nn.sigmoid(gate) * up).astype(BF16)
    acc_ref[...] += _dot(act, wd_ref[...])

    @pl.when(f == pl.num_programs(1) - 1)
    def _():
        y = x_ref[...] + mod_ref[0, MOD_GATE_FFN:MOD_GATE_FFN + 1, :] * acc_ref[...]
        if final_norm:
            y = _rms(y) * fg_ref[...]
        o_ref[...] = y


def _ffn(x1, mod3, gain, w_in, w_down, final_g, seq, final_norm):
    t, d = x1.shape
    hidden = w_down.shape[0]
    tm = _tile(seq, 512)
    tf = _tile(hidden, 512)
    nf = hidden // tf
    kern = functools.partial(_ffn_kernel, final_norm=final_norm)
    return pl.pallas_call(
        kern,
        grid=(t // tm, nf),
        in_specs=[pl.BlockSpec((tm, d), lambda i, f: (i, 0)),
                  pl.BlockSpec((1, N_MOD, d), lambda i, f: ((i * tm) // seq, 0, 0)),
                  pl.BlockSpec((1, d), lambda i, f: (0, 0)),
                  pl.BlockSpec((d, tf), lambda i, f: (0, f)),
                  pl.BlockSpec((d, tf), lambda i, f: (0, nf + f)),
                  pl.BlockSpec((tf, d), lambda i, f: (f, 0)),
                  pl.BlockSpec((1, d), lambda i, f: (0, 0))],
        out_specs=pl.BlockSpec((tm, d), lambda i, f: (i, 0)),
        out_shape=jax.ShapeDtypeStruct((t, d), F32),
        scratch_shapes=[pltpu.VMEM((tm, d), BF16), pltpu.VMEM((tm, d), F32)],
        compiler_params=_params("parallel", "arbitrary"),
        name="ffn",
    )(x1, mod3, gain.reshape(1, d), w_in, w_in, w_down, final_g.reshape(1, d))


def _pack_w_in_kernel(wt_ref, o_ref, *, moves, kr_row, gk_row, tail_col):
    for dst, src, width in moves:
        o_ref[:, dst:dst + width] = wt_ref[src:src + width, :].T.astype(BF16)
    half = MLA_ROPE // 2
    kr = wt_ref[kr_row:kr_row + MLA_ROPE, :]
    gk = wt_ref[gk_row:gk_row + GLA_GATE_RANK, :]
    pad = jnp.zeros((V7X_LANES - GLA_GATE_RANK, kr.shape[1]), F32)
    o_ref[:, tail_col:] = jnp.concatenate([kr, kr[half:], kr[:half], gk, pad], axis=0).T.astype(BF16)


def _pack_w_in(w, d, dk_all, dv_all, q_rank, kv_rank):
    widths = (dk_all, dk_all, dv_all, dv_all, GLA_GATE_RANK, q_rank, kv_rank, MLA_ROPE, d, d)
    offs = [0]
    for wd in widths:
        offs.append(offs[-1] + wd)
    gla_w = offs[4]
    moves = ((0, 0, gla_w), (gla_w, offs[8], 2 * d), (gla_w + 2 * d, offs[5], q_rank + kv_rank))
    tail_col = gla_w + 2 * d + q_rank + kv_rank
    n_out = tail_col + 2 * V7X_LANES
    assert all(dst % V7X_LANES == 0 for dst, _, _ in moves) and tail_col % V7X_LANES == 0
    rows = w.shape[0]
    tr = _tile(rows, 128)
    kern = functools.partial(_pack_w_in_kernel, moves=moves, kr_row=offs[7], gk_row=offs[4], tail_col=tail_col)
    return pl.pallas_call(
        kern,
        grid=(rows // tr,),
        in_specs=[pl.BlockSpec((w.shape[1], tr), lambda i: (0, i))],
        out_specs=pl.BlockSpec((tr, n_out), lambda i: (i, 0)),
        out_shape=jax.ShapeDtypeStruct((rows, n_out), BF16),
        compiler_params=_params("parallel"),
        name="pack_w_in",
    )(w.T)


def _pack_wuq_t(w):
    r = w.shape[0]
    half = MLA_ROPE // 2
    w3 = w.reshape(r, MLA_HEADS, MLA_NOPE + MLA_ROPE)
    rope = w3[:, :, MLA_NOPE:]
    rot = jnp.concatenate([rope[:, :, half:], rope[:, :, :half]], axis=2)
    return jnp.concatenate([w3[:, :, :MLA_NOPE], rope, rot], axis=2).reshape(r, -1).T.astype(BF16)


def _pack_wukv(w):
    r = w.shape[0]
    w3 = w.reshape(r, MLA_HEADS, MLA_NOPE + MLA_V)
    wk = w3[:, :, :MLA_NOPE].reshape(r, -1).astype(BF16)
    wvt = w3[:, :, MLA_NOPE:].reshape(r, -1).T.astype(BF16)
    return wk, wvt


def kernel(x, c, positions, ada_w, ada_b, norm_mix_g, w_in, gla_gk_w, gla_gk_b, gla_onorm_g, gla_wo, mla_q_norm_g, mla_wuq, mla_kv_norm_g, mla_wukv, mla_wo, w_out, norm_ffn_g, ffn_w_in, ffn_w_down, final_norm_g):
    bsz, seq, d = x.shape
    depth = w_in.shape[0]
    dk_all = gla_gk_w.shape[2]
    dv_all = gla_wo.shape[1]
    dk, dv = dk_all // GLA_HEADS, dv_all // GLA_HEADS
    q_rank, kv_rank = mla_wuq.shape[1], mla_wukv.shape[1]
    lat_col = 2 * dk_all + 2 * dv_all + 2 * d
    lat_width = q_rank + kv_rank + 2 * V7X_LANES
    gate_col = 2 * dk_all + 2 * dv_all
    assert lat_col % lat_width == 0 and (lat_col + lat_width - V7X_LANES) % V7X_LANES == 0

    x2 = x.reshape(bsz * seq, d)
    cos_t, sin_t = _rope_tab(positions)
    for l in range(depth):
        mod3 = _ada(c, ada_w[l], ada_b[l]).reshape(bsz, N_MOD, d)
        w_all = _pack_w_in(w_in[l], d, dk_all, dv_all, q_rank, kv_rank)
        p = _inproj(x2, mod3, norm_mix_g[l], w_all, seq)
        wk, wvt = _pack_wukv(mla_wukv[l])
        qt, k, vt = _mlaproj(p, lat_col // lat_width, lat_width, cos_t, sin_t, mla_q_norm_g[l], mla_kv_norm_g[l],
                             _pack_wuq_t(mla_wuq[l]), wk, wvt)
        gkw = jnp.zeros((V7X_LANES, dk_all), BF16).at[:GLA_GATE_RANK].set(gla_gk_w[l].astype(BF16))
        o_gla = _gla(p, (lat_col + lat_width - V7X_LANES) // V7X_LANES, gkw, gla_gk_b[l], gla_onorm_g[l],
                     bsz, seq, dk, dv)
        o_mla = _flash(qt, k, vt, bsz, seq)
        x1 = _mixout(o_gla, o_mla, p, gate_col, x2, mod3, gla_wo[l].astype(BF16), mla_wo[l].astype(BF16),
                     w_out[l].astype(BF16), seq)
        x2 = _ffn(x1, mod3, norm_ffn_g[l], ffn_w_in[l].astype(BF16), ffn_w_down[l].astype(BF16), final_norm_g,
                  seq, final_norm=(l == depth - 1))
    return x2.reshape(bsz, seq, d)
```

```python
import functools

import jax
import jax.numpy as jnp
from jax import lax
from jax.experimental import pallas as pl
from jax.experimental.pallas import tpu as pltpu

F32 = jnp.float32
BF16 = jnp.bfloat16

GLA_HEADS = 4
GLA_GATE_RANK = 16
GLA_GATE_NORMALIZER = 16.0
GLA_CHUNK = 128
GLA_CHUNKS_PER_STEP = 2
MLA_HEADS = 16
MLA_NOPE = 128
MLA_ROPE = 64
MLA_V = 128
MLA_QW = MLA_NOPE + 2 * MLA_ROPE
MLA_QBLOCK = 1024
MLA_STRIP = 512
MLA_CHAIN = 512
MLA_VP = MLA_V + 16
MLA_HEAD_GROUP = 2
ROPE_THETA = 10000.0
LOG2_E = 1.4426950408889634
NORM_EPS = 1e-6
N_MOD = 6
MOD_SHIFT_MIX, MOD_SCALE_MIX, MOD_GATE_MIX, MOD_SHIFT_FFN, MOD_SCALE_FFN, MOD_GATE_FFN = range(N_MOD)

V7X_LANES = 128
V7X_VMEM_LIMIT_BYTES = 56 * 1024 * 1024

MASK_VALUE = -1e30


def _params(*semantics):
    return pltpu.CompilerParams(dimension_semantics=semantics, vmem_limit_bytes=V7X_VMEM_LIMIT_BYTES)


def _dot(a, b):
    return jnp.dot(a, b, preferred_element_type=F32)


def _dot_nt(a, b):
    return lax.dot_general(a, b, (((1,), (1,)), ((), ())), preferred_element_type=F32)


def _dot_tn(a, b):
    return lax.dot_general(a, b, (((0,), (0,)), ((), ())), preferred_element_type=F32)


def _rms(x):
    return x * lax.rsqrt(jnp.mean(x * x, axis=-1, keepdims=True) + NORM_EPS)


def _modnorm(x, gain, scale, shift):
    return _rms(x) * (gain * (1.0 + scale)) + shift


def _tile(n, want):
    t = min(n, want)
    assert n % t == 0, (n, want)
    return t


def _ada_kernel(c_ref, w_ref, b_ref, o_ref):
    c = c_ref[...]
    act = (c * jax.nn.sigmoid(c)).astype(BF16)
    o_ref[...] = _dot(act, w_ref[...].astype(BF16)) + b_ref[...]


def _ada(c, w, b):
    bsz, d = c.shape
    n = w.shape[1]
    tn = _tile(n, 1024)
    return pl.pallas_call(
        _ada_kernel,
        grid=(n // tn,),
        in_specs=[pl.BlockSpec((bsz, d), lambda j: (0, 0)),
                  pl.BlockSpec((d, tn), lambda j: (0, j)),
                  pl.BlockSpec((1, tn), lambda j: (0, j))],
        out_specs=pl.BlockSpec((bsz, tn), lambda j: (0, j)),
        out_shape=jax.ShapeDtypeStruct((bsz, n), F32),
        compiler_params=_params("arbitrary"),
        name="ada",
    )(c, w, b.reshape(1, n))


def _rope_tab_kernel(pos_ref, f_ref, cos_ref, sin_ref):
    ang = pos_ref[...].astype(F32) * f_ref[...]
    cos_ref[...] = jnp.cos(ang)
    sin_ref[...] = jnp.sin(ang)


def _rope_tab(positions):
    t = positions.size
    half = MLA_ROPE // 2
    inv_freq = ROPE_THETA ** (-jnp.arange(0, MLA_ROPE, 2, dtype=F32) / MLA_ROPE)
    tt = _tile(t, 4096)
    spec = pl.BlockSpec((half, tt), lambda i: (0, i))
    return pl.pallas_call(
        _rope_tab_kernel,
        grid=(t // tt,),
        in_specs=[pl.BlockSpec((1, tt), lambda i: (0, i)), pl.BlockSpec((half, 1), lambda i: (0, 0))],
        out_specs=[spec, spec],
        out_shape=[jax.ShapeDtypeStruct((half, t), F32)] * 2,
        compiler_params=_params("arbitrary"),
        name="rope_tab",
    )(positions.reshape(1, t), inv_freq.reshape(half, 1))


def _inproj_kernel(x_ref, mod_ref, g_ref, w_ref, o_ref, h_ref):
    @pl.when(pl.program_id(1) == 0)
    def _():
        h = _modnorm(x_ref[...], g_ref[...],
                     mod_ref[0, MOD_SCALE_MIX:MOD_SCALE_MIX + 1, :], mod_ref[0, MOD_SHIFT_MIX:MOD_SHIFT_MIX + 1, :])
        h_ref[...] = h.astype(BF16)

    o_ref[...] = _dot(h_ref[...], w_ref[...]).astype(o_ref.dtype)


def _inproj(x2, mod3, gain, w, seq):
    t, d = x2.shape
    n = w.shape[1]
    tm = _tile(seq, 1024)
    tn = _tile(n, 1280)
    return pl.pallas_call(
        _inproj_kernel,
        grid=(t // tm, n // tn),
        in_specs=[pl.BlockSpec((tm, d), lambda i, j: (i, 0)),
                  pl.BlockSpec((1, N_MOD, d), lambda i, j: ((i * tm) // seq, 0, 0)),
                  pl.BlockSpec((1, d), lambda i, j: (0, 0)),
                  pl.BlockSpec((d, tn), lambda i, j: (0, j))],
        out_specs=pl.BlockSpec((tm, tn), lambda i, j: (i, j)),
        out_shape=jax.ShapeDtypeStruct((t, n), BF16),
        scratch_shapes=[pltpu.VMEM((tm, d), BF16)],
        compiler_params=_params("parallel", "arbitrary"),
        name="inproj",
    )(x2, mod3, gain.reshape(1, d), w)


def _mlaproj_kernel(lat_ref, cos_ref, sin_ref, gq_ref, gkv_ref, wqt_ref, wk_ref, wvt_ref, qt_ref, k_ref, vt_ref, *,
                    q_rank, kv_rank, scale):
    tm = lat_ref.shape[0]
    cos = cos_ref[...]
    sin = sin_ref[...]
    cs = jnp.concatenate([cos, cos], axis=0)
    sn = jnp.concatenate([-sin, sin], axis=0)

    cqn = _rms(lat_ref[:, 0:q_rank].astype(F32)) * gq_ref[...]
    ckvn = _rms(lat_ref[:, q_rank:q_rank + kv_rank].astype(F32)) * gkv_ref[...]
    cqn_t = cqn.T.astype(BF16)
    ckvn_t = ckvn.T.astype(BF16)
    ckvn = ckvn.astype(BF16)

    kr_t = lat_ref[:, q_rank + kv_rank:q_rank + kv_rank + V7X_LANES].astype(F32).T
    k_rope_t = kr_t[:MLA_ROPE] * cs + kr_t[MLA_ROPE:] * sn
    k_rope = jnp.concatenate([k_rope_t, jnp.zeros_like(k_rope_t)], axis=0).T.astype(BF16)

    for h in range(MLA_HEADS):
        qt = _dot(wqt_ref[h * MLA_QW:(h + 1) * MLA_QW, :], cqn_t)
        q_rope = qt[MLA_NOPE:MLA_NOPE + MLA_ROPE] * cs + qt[MLA_NOPE + MLA_ROPE:] * sn
        qt_ref[h * MLA_QW:h * MLA_QW + MLA_NOPE, :] = (qt[:MLA_NOPE] * scale).astype(BF16)
        qt_ref[h * MLA_QW + MLA_NOPE:h * MLA_QW + MLA_NOPE + MLA_ROPE, :] = (q_rope * scale).astype(BF16)
        qt_ref[h * MLA_QW + MLA_NOPE + MLA_ROPE:(h + 1) * MLA_QW, :] = jnp.zeros((MLA_ROPE, tm), BF16)
        vt_ref[h, 0, :MLA_V, :] = _dot(wvt_ref[h * MLA_V:(h + 1) * MLA_V, :], ckvn_t).astype(BF16)
        vt_ref[h, 0, MLA_V:, :] = jnp.ones((MLA_VP - MLA_V, tm), BF16)
    for g in range(MLA_HEADS // 2):
        kk = _dot(ckvn, wk_ref[:, 2 * g * MLA_NOPE:2 * (g + 1) * MLA_NOPE]).astype(BF16)
        for e in range(2):
            h = 2 * g + e
            k_ref[:, h * MLA_QW:h * MLA_QW + MLA_NOPE] = kk[:, e * MLA_NOPE:(e + 1) * MLA_NOPE]
            k_ref[:, h * MLA_QW + MLA_NOPE:(h + 1) * MLA_QW] = k_rope


def _mlaproj(p, lat_block, lat_width, cos_t, sin_t, gq, gkv, wqt, wk, wvt):
    t = p.shape[0]
    q_rank, kv_rank = gq.shape[0], gkv.shape[0]
    tm = _tile(MLA_STRIP, 512)
    per_strip = MLA_STRIP // tm
    half = MLA_ROPE // 2
    scale = (MLA_NOPE + MLA_ROPE) ** -0.5 * LOG2_E
    kern = functools.partial(_mlaproj_kernel, q_rank=q_rank, kv_rank=kv_rank, scale=scale)
    return pl.pallas_call(
        kern,
        grid=(t // tm,),
        in_specs=[pl.BlockSpec((tm, lat_width), lambda i: (i, lat_block)),
                  pl.BlockSpec((half, tm), lambda i: (0, i)),
                  pl.BlockSpec((half, tm), lambda i: (0, i)),
                  pl.BlockSpec((1, q_rank), lambda i: (0, 0)),
                  pl.BlockSpec((1, kv_rank), lambda i: (0, 0)),
                  pl.BlockSpec(wqt.shape, lambda i: (0, 0)),
                  pl.BlockSpec(wk.shape, lambda i: (0, 0)),
                  pl.BlockSpec(wvt.shape, lambda i: (0, 0))],
        out_specs=[pl.BlockSpec((MLA_HEADS * MLA_QW, tm), lambda i: (0, i)),
                   pl.BlockSpec((tm, MLA_HEADS * MLA_QW), lambda i: (i, 0)),
                   pl.BlockSpec((MLA_HEADS, 1, MLA_VP, tm), lambda i: (0, i // per_strip, 0, i % per_strip))],
        out_shape=[jax.ShapeDtypeStruct((MLA_HEADS * MLA_QW, t), BF16),
                   jax.ShapeDtypeStruct((t, MLA_HEADS * MLA_QW), BF16),
                   jax.ShapeDtypeStruct((MLA_HEADS, t // MLA_STRIP, MLA_VP, MLA_STRIP), BF16)],
        compiler_params=_params("parallel"),
        name="mlaproj",
    )(p, cos_t, sin_t, gq.reshape(1, q_rank), gkv.reshape(1, kv_rank), wqt, wk, wvt)


def _gla_kernel(q_ref, k_ref, v_ref, g_ref, gk_ref, gkw_ref, gkb_ref, on_ref, scan_ref, o_ref, st_ref, *, dk, dv):
    c = GLA_CHUNK
    nsub = q_ref.shape[0] // c

    @pl.when(pl.program_id(1) == 0)
    def _():
        st_ref[...] = jnp.zeros_like(st_ref)

    z = _dot(gk_ref[...], gkw_ref[...]) + gkb_ref[...]
    log_a = jax.nn.log_sigmoid(z) * (LOG2_E / GLA_GATE_NORMALIZER)

    halves = [1 << e for e in range(c.bit_length() - 1)]
    ai = lax.broadcasted_iota(jnp.int32, (c, c), 0)
    aj = lax.broadcasted_iota(jnp.int32, (c, c), 1)
    diff = ai ^ aj
    causal_diff = jnp.where(ai > aj, diff, 0)
    level_mask = {s: (causal_diff >> (s.bit_length() - 1)) == 1 for s in halves}
    diag_mask = ai == aj

    g_hi = log_a.astype(BF16)
    g_lo = (log_a - g_hi.astype(F32)).astype(BF16)
    n_lv = len(halves) * c

    def chunk_local(h, u):
        rows, cols = slice(u * c, (u + 1) * c), slice(h * dk, (h + 1) * dk)
        level_sum = _dot(scan_ref[:n_lv, :], g_hi[rows, cols])
        cum = _dot(scan_ref[n_lv:, :], jnp.concatenate([g_hi[rows, cols], g_lo[rows, cols]], axis=1))
        prefix = cum[:, :dk] + cum[:, dk:]
        total = prefix[c - 1:c, :]
        q16 = q_ref[rows, cols] * (dk ** -0.5)
        k16 = k_ref[rows, cols]
        att = jnp.where(diag_mask, _dot_nt(q16, k16), 0.0)
        for li, s in enumerate(halves):
            decay = jnp.exp2(level_sum[li * c:(li + 1) * c, :].astype(BF16))
            att = jnp.where(level_mask[s], _dot_nt(q16 * decay, k16 * decay), att)
        q_dec = (q16.astype(F32) * jnp.exp2(prefix)).astype(BF16)
        k_dec = (k16.astype(F32) * jnp.exp2(total - prefix)).astype(BF16)
        return q_dec, k_dec, att.astype(BF16), jnp.exp2(total)

    for h in range(GLA_HEADS):
        local = [chunk_local(h, u) for u in range(nsub)]
        for u, (q_dec, k_dec, att, state_decay) in enumerate(local):
            rows = slice(u * c, (u + 1) * c)
            vh = v_ref[rows, h * dv:(h + 1) * dv]
            st = st_ref[h]
            o = _dot_nt(q_dec, st.astype(BF16)) + _dot(att, vh)
            st_ref[h] = st * state_decay + _dot_tn(vh, k_dec)
            gate = g_ref[rows, h * dv:(h + 1) * dv].astype(F32)
            o_ref[rows, h * dv:(h + 1) * dv] = (_rms(o) * on_ref[...] * (gate * jax.nn.sigmoid(gate))).astype(o_ref.dtype)


def _gla_scan_matrix(c):
    p = jnp.arange(c)[:, None]
    t = jnp.arange(c)[None, :]
    mats = []
    s = 1
    while s < c:
        start = (p // s) * s
        odd = (p // s) % 2 == 1
        mats.append(jnp.where(odd, (t >= start) & (t <= p), (t > p) & (t < start + s)))
        s *= 2
    mats.append(t <= p)
    return jnp.concatenate(mats, axis=0).astype(BF16)


def _gla(p, gk_block, gkw, gkb, onorm_g, bsz, seq, dk, dv):
    t = p.shape[0]
    c = _tile(seq, GLA_CHUNK * GLA_CHUNKS_PER_STEP)
    nc = seq // c
    hk, hv = GLA_HEADS * dk, GLA_HEADS * dv
    scan = _gla_scan_matrix(GLA_CHUNK)
    kern = functools.partial(_gla_kernel, dk=dk, dv=dv)
    row = lambda b, i: b * nc + i
    return pl.pallas_call(
        kern,
        grid=(bsz, nc),
        in_specs=[pl.BlockSpec((c, hk), lambda b, i: (row(b, i), 0)),
                  pl.BlockSpec((c, hk), lambda b, i: (row(b, i), 1)),
                  pl.BlockSpec((c, hv), lambda b, i: (row(b, i), (2 * hk) // hv)),
                  pl.BlockSpec((c, hv), lambda b, i: (row(b, i), (2 * hk) // hv + 1)),
                  pl.BlockSpec((c, V7X_LANES), lambda b, i: (row(b, i), gk_block)),
                  pl.BlockSpec(gkw.shape, lambda b, i: (0, 0)),
                  pl.BlockSpec((1, hk), lambda b, i: (0, 0)),
                  pl.BlockSpec((1, dv), lambda b, i: (0, 0)),
                  pl.BlockSpec(scan.shape, lambda b, i: (0, 0))],
        out_specs=pl.BlockSpec((c, hv), lambda b, i: (row(b, i), 0)),
        out_shape=jax.ShapeDtypeStruct((t, hv), BF16),
        scratch_shapes=[pltpu.VMEM((GLA_HEADS, dv, dk), F32)],
        compiler_params=_params("parallel", "arbitrary"),
        name="gla",
    )(p, p, p, p, p, gkw, gkb.reshape(1, hk), onorm_g.reshape(1, dv), scan)


def _flash_kernel(qt_ref, k_ref, vt_ref, o_ref, acc_ref, sa_ref, sb_ref, pa_ref, pb_ref):
    strip, cw = MLA_STRIP, MLA_CHAIN
    hp, nch = acc_ref.shape[0], acc_ref.shape[1]
    chains = [(h, a) for h in range(hp) for a in range(nch)]
    per_block = MLA_QBLOCK // strip
    assert per_block == 2 and nch == 2 and strip == cw
    i = pl.program_id(2)
    acc_ref[...] = jnp.zeros_like(acc_ref)
    pb_ref[...] = jnp.zeros_like(pb_ref)

    def scores(h, a, strip_idx):
        start = pl.multiple_of(strip_idx * strip, strip)
        return _dot(k_ref[pl.ds(start, strip), h * MLA_QW:(h + 1) * MLA_QW],
                    qt_ref[h * MLA_QW:(h + 1) * MLA_QW, a * cw:(a + 1) * cw])

    def col_max(s):
        return jnp.max(s, axis=0, keepdims=True)

    def softmax(s, s_max, m):
        m_new = jnp.maximum(m, s_max)
        return jnp.exp2((s - m_new).astype(BF16)), jnp.exp2(m - m_new), m_new

    def accumulate(h, a, strip_idx, p, alpha):
        acc_ref[h, a] = alpha * acc_ref[h, a] + _dot(vt_ref[h, strip_idx], p)

    def half_step(s_cur, s_nxt, p_cur, p_prev, t, carry):
        out = []
        for c, (h, a) in enumerate(chains):
            m, cur_max, alpha_prev = carry[c]
            s = scores(h, a, t + 1)
            s_nxt[h, a] = s
            accumulate(h, a, jnp.maximum(t - 1, 0), p_prev[h, a], alpha_prev)
            p, alpha, m = softmax(s_cur[h, a], cur_max, m)
            p_cur[h, a] = p
            out.append((m, col_max(s), alpha))
        return tuple(out)

    def body(jj, carry):
        carry = half_step(sa_ref, sb_ref, pa_ref, pb_ref, 2 * jj, carry)
        return half_step(sb_ref, sa_ref, pb_ref, pa_ref, 2 * jj + 1, carry)

    init = []
    for h, a in chains:
        s = scores(h, a, 0)
        sa_ref[h, a] = s
        init.append((jnp.full((1, cw), MASK_VALUE, F32), col_max(s), jnp.ones((1, cw), F32)))
    carry = lax.fori_loop(0, i, body, tuple(init))
    first = per_block * i
    causal = lax.broadcasted_iota(jnp.int32, (strip, cw), 0) <= lax.broadcasted_iota(jnp.int32, (strip, cw), 1)
    last = [scores(h, 1, first + 1) for h in range(hp)]

    def update(h, a, strip_idx, s, s_max, m):
        p, alpha, m = softmax(s, s_max, m)
        accumulate(h, a, strip_idx, p, alpha)
        return m

    for c, (h, a) in enumerate(chains):
        m, s_max, alpha_prev = carry[c]
        accumulate(h, a, jnp.maximum(first - 1, 0), pb_ref[h, a], alpha_prev)
        if a == 0:
            s = jnp.where(causal, sa_ref[h, a], MASK_VALUE)
            update(h, a, first, s, col_max(s), m)
        else:
            m = update(h, a, first, sa_ref[h, a], s_max, m)
            s = jnp.where(causal, last[h], MASK_VALUE)
            update(h, a, first + 1, s, col_max(s), m)
        o_t = acc_ref[h, a, :MLA_V, :] / acc_ref[h, a, MLA_V:MLA_V + 1, :]
        o_ref[a * cw:(a + 1) * cw, h * MLA_V:(h + 1) * MLA_V] = o_t.T.astype(o_ref.dtype)


def _flash(qt, k, vt, bsz, seq):
    t = k.shape[0]
    assert seq % MLA_QBLOCK == 0
    nq = seq // MLA_QBLOCK
    hp = MLA_HEAD_GROUP
    nch = MLA_QBLOCK // MLA_CHAIN
    return pl.pallas_call(
        _flash_kernel,
        grid=(bsz, MLA_HEADS // hp, nq),
        in_specs=[pl.BlockSpec((hp * MLA_QW, MLA_QBLOCK), lambda b, g, i: (g, b * nq + i)),
                  pl.BlockSpec((seq, hp * MLA_QW), lambda b, g, i: (b, g)),
                  pl.BlockSpec((hp, seq // MLA_STRIP, MLA_VP, MLA_STRIP), lambda b, g, i: (g, b, 0, 0))],
        out_specs=pl.BlockSpec((MLA_QBLOCK, hp * MLA_V), lambda b, g, i: (b * nq + i, g)),
        out_shape=jax.ShapeDtypeStruct((t, MLA_HEADS * MLA_V), BF16),
        scratch_shapes=[pltpu.VMEM((hp, nch, MLA_VP, MLA_CHAIN), F32),
                        pltpu.VMEM((hp, nch, MLA_STRIP, MLA_CHAIN), F32),
                        pltpu.VMEM((hp, nch, MLA_STRIP, MLA_CHAIN), F32),
                        pltpu.VMEM((hp, nch, MLA_STRIP, MLA_CHAIN), BF16),
                        pltpu.VMEM((hp, nch, MLA_STRIP, MLA_CHAIN), BF16)],
        compiler_params=_params("parallel", "parallel", "arbitrary"),
        name="flash",
    )(qt, k, vt)


def _mixout_kernel(og_ref, om_ref, ga_ref, gb_ref, x_ref, mod_ref, wg_ref, wm_ref, wo_ref, o_ref):
    yg = _dot(og_ref[...], wg_ref[...])
    ym = _dot(om_ref[...], wm_ref[...])
    ga = jax.nn.sigmoid(ga_ref[...].astype(F32))
    gb = jax.nn.sigmoid(gb_ref[...].astype(F32))
    merged = (ga * yg + gb * ym).astype(BF16)
    o_ref[...] = x_ref[...] + mod_ref[0, MOD_GATE_MIX:MOD_GATE_MIX + 1, :] * _dot(merged, wo_ref[...])


def _mixout(o_gla, o_mla, p, gate_col, x2, mod3, w_gla, w_mla, w_out, seq):
    t, d = x2.shape
    tm = _tile(seq, 256)
    assert gate_col % d == 0
    row = lambda i: (i, 0)
    resident = lambda w: pl.BlockSpec(w.shape, lambda i: (0, 0), pipeline_mode=pl.Buffered(1))
    return pl.pallas_call(
        _mixout_kernel,
        grid=(t // tm,),
        in_specs=[pl.BlockSpec((tm, o_gla.shape[1]), row),
                  pl.BlockSpec((tm, o_mla.shape[1]), row),
                  pl.BlockSpec((tm, d), lambda i: (i, gate_col // d)),
                  pl.BlockSpec((tm, d), lambda i: (i, gate_col // d + 1)),
                  pl.BlockSpec((tm, d), row),
                  pl.BlockSpec((1, N_MOD, d), lambda i: ((i * tm) // seq, 0, 0)),
                  resident(w_gla), resident(w_mla), resident(w_out)],
        out_specs=pl.BlockSpec((tm, d), row),
        out_shape=jax.ShapeDtypeStruct((t, d), F32),
        compiler_params=_params("parallel"),
        name="mixout",
    )(o_gla, o_mla, p, p, x2, mod3, w_gla, w_mla, w_out)


def _ffn_kernel(x_ref, mod_ref, g_ref, wg_ref, wu_ref, wd_ref, fg_ref, o_ref, h_ref, acc_ref, *, final_norm):
    f = pl.program_id(1)

    @pl.when(f == 0)
    def _():
        h = _modnorm(x_ref[...], g_ref[...],
                     mod_ref[0, MOD_SCALE_FFN:MOD_SCALE_FFN + 1, :], mod_ref[0, MOD_SHIFT_FFN:MOD_SHIFT_FFN + 1, :])
        h_ref[...] = h.astype(BF16)
        acc_ref[...] = jnp.zeros_like(acc_ref)

    h = h_ref[...]
    gate = _dot(h, wg_ref[...])
    up = _dot(h, wu_ref[...])
    act = (gate * jax.nn.sigmoid(gate) * up).astype(BF16)
    acc_ref[...] += _dot(act, wd_ref[...])

    @pl.when(f == pl.num_programs(1) - 1)
    def _():
        y = x_ref[...] + mod_ref[0, MOD_GATE_FFN:MOD_GATE_FFN + 1, :] * acc_ref[...]
        if final_norm:
            y = _rms(y) * fg_ref[...]
        o_ref[...] = y


def _ffn(x1, mod3, gain, w_in, w_down, final_g, seq, final_norm):
    t, d = x1.shape
    hidden = w_down.shape[0]
    tm = _tile(seq, 512)
    tf = _tile(hidden, 512)
    nf = hidden // tf
    kern = functools.partial(_ffn_kernel, final_norm=final_norm)
    return pl.pallas_call(
        kern,
        grid=(t // tm, nf),
        in_specs=[pl.BlockSpec((tm, d), lambda i, f: (i, 0)),
                  pl.BlockSpec((1, N_MOD, d), lambda i, f: ((i * tm) // seq, 0, 0)),
                  pl.BlockSpec((1, d), lambda i, f: (0, 0)),
                  pl.BlockSpec((d, tf), lambda i, f: (0, f)),
                  pl.BlockSpec((d, tf), lambda i, f: (0, nf + f)),
                  pl.BlockSpec((tf, d), lambda i, f: (f, 0)),
                  pl.BlockSpec((1, d), lambda i, f: (0, 0))],
        out_specs=pl.BlockSpec((tm, d), lambda i, f: (i, 0)),
        out_shape=jax.ShapeDtypeStruct((t, d), F32),
        scratch_shapes=[pltpu.VMEM((tm, d), BF16), pltpu.VMEM((tm, d), F32)],
        compiler_params=_params("parallel", "arbitrary"),
        name="ffn",
    )(x1, mod3, gain.reshape(1, d), w_in, w_in, w_down, final_g.reshape(1, d))


def _pack_w_in_kernel(wt_ref, o_ref, *, moves, kr_row, gk_row, tail_col):
    for dst, src, width in moves:
        o_ref[:, dst:dst + width] = wt_ref[src:src + width, :].T.astype(BF16)
    half = MLA_ROPE // 2
    kr = wt_ref[kr_row:kr_row + MLA_ROPE, :]
    gk = wt_ref[gk_row:gk_row + GLA_GATE_RANK, :]
    pad = jnp.zeros((V7X_LANES - GLA_GATE_RANK, kr.shape[1]), F32)
    o_ref[:, tail_col:] = jnp.concatenate([kr, kr[half:], kr[:half], gk, pad], axis=0).T.astype(BF16)


def _pack_w_in(w, d, dk_all, dv_all, q_rank, kv_rank):
    widths = (dk_all, dk_all, dv_all, dv_all, GLA_GATE_RANK, q_rank, kv_rank, MLA_ROPE, d, d)
    offs = [0]
    for wd in widths:
        offs.append(offs[-1] + wd)
    gla_w = offs[4]
    moves = ((0, 0, gla_w), (gla_w, offs[8], 2 * d), (gla_w + 2 * d, offs[5], q_rank + kv_rank))
    tail_col = gla_w + 2 * d + q_rank + kv_rank
    n_out = tail_col + 2 * V7X_LANES
    assert all(dst % V7X_LANES == 0 for dst, _, _ in moves) and tail_col % V7X_LANES == 0
    rows = w.shape[0]
    tr = _tile(rows, 128)
    kern = functools.partial(_pack_w_in_kernel, moves=moves, kr_row=offs[7], gk_row=offs[4], tail_col=tail_col)
    return pl.pallas_call(
        kern,
        grid=(rows // tr,),
        in_specs=[pl.BlockSpec((w.shape[1], tr), lambda i: (0, i))],
        out_specs=pl.BlockSpec((tr, n_out), lambda i: (i, 0)),
        out_shape=jax.ShapeDtypeStruct((rows, n_out), BF16),
        compiler_params=_params("parallel"),
        name="pack_w_in",
    )(w.T)


def _pack_wuq_t(w):
    r = w.shape[0]
    half = MLA_ROPE // 2
    w3 = w.reshape(r, MLA_HEADS, MLA_NOPE + MLA_ROPE)
    rope = w3[:, :, MLA_NOPE:]
    rot = jnp.concatenate([rope[:, :, half:], rope[:, :, :half]], axis=2)
    return jnp.concatenate([w3[:, :, :MLA_NOPE], rope, rot], axis=2).reshape(r, -1).T.astype(BF16)


def _pack_wukv(w):
    r = w.shape[0]
    w3 = w.reshape(r, MLA_HEADS, MLA_NOPE + MLA_V)
    wk = w3[:, :, :MLA_NOPE].reshape(r, -1).astype(BF16)
    wvt = w3[:, :, MLA_NOPE:].reshape(r, -1).T.astype(BF16)
    return wk, wvt


def kernel(x, c, positions, ada_w, ada_b, norm_mix_g, w_in, gla_gk_w, gla_gk_b, gla_onorm_g, gla_wo, mla_q_norm_g, mla_wuq, mla_kv_norm_g, mla_wukv, mla_wo, w_out, norm_ffn_g, ffn_w_in, ffn_w_down, final_norm_g):
    bsz, seq, d = x.shape
    depth = w_in.shape[0]
    dk_all = gla_gk_w.shape[2]
    dv_all = gla_wo.shape[1]
    dk, dv = dk_all // GLA_HEADS, dv_all // GLA_HEADS
    q_rank, kv_rank = mla_wuq.shape[1], mla_wukv.shape[1]
    lat_col = 2 * dk_all + 2 * dv_all + 2 * d
    lat_width = q_rank + kv_rank + 2 * V7X_LANES
    gate_col = 2 * dk_all + 2 * dv_all
    assert lat_col % lat_width == 0 and (lat_col + lat_width - V7X_LANES) % V7X_LANES == 0

    x2 = x.reshape(bsz * seq, d)
    cos_t, sin_t = _rope_tab(positions)
    for l in range(depth):
        mod3 = _ada(c, ada_w[l], ada_b[l]).reshape(bsz, N_MOD, d)
        w_all = _pack_w_in(w_in[l], d, dk_all, dv_all, q_rank, kv_rank)
        p = _inproj(x2, mod3, norm_mix_g[l], w_all, seq)
        wk, wvt = _pack_wukv(mla_wukv[l])
        qt, k, vt = _mlaproj(p, lat_col // lat_width, lat_width, cos_t, sin_t, mla_q_norm_g[l], mla_kv_norm_g[l],
                             _pack_wuq_t(mla_wuq[l]), wk, wvt)
        gkw = jnp.zeros((V7X_LANES, dk_all), BF16).at[:GLA_GATE_RANK].set(gla_gk_w[l].astype(BF16))
        o_gla = _gla(p, (lat_col + lat_width - V7X_LANES) // V7X_LANES, gkw, gla_gk_b[l], gla_onorm_g[l],
                     bsz, seq, dk, dv)
        o_mla = _flash(qt, k, vt, bsz, seq)
        x1 = _mixout(o_gla, o_mla, p, gate_col, x2, mod3, gla_wo[l].astype(BF16), mla_wo[l].astype(BF16),
                     w_out[l].astype(BF16), seq)
        x2 = _ffn(x1, mod3, norm_ffn_g[l], ffn_w_in[l].astype(BF16), ffn_w_down[l].astype(BF16), final_norm_g,
                  seq, final_norm=(l == depth - 1))
    return x2.reshape(bsz, seq, d)
```

```python
import functools

import jax
import jax.numpy as jnp
from jax import lax
from jax.experimental import pallas as pl
from jax.experimental.pallas import tpu as pltpu

F32 = jnp.float32
BF16 = jnp.bfloat16

GLA_HEADS = 4
GLA_GATE_RANK = 16
GLA_GATE_NORMALIZER = 16.0
GLA_CHUNK = 128
GLA_CHUNKS_PER_STEP = 4
MLA_HEADS = 16
MLA_NOPE = 128
MLA_ROPE = 64
MLA_V = 128
MLA_QW = MLA_NOPE + 2 * MLA_ROPE
MLA_QBLOCK = 1024
MLA_STRIP = 512
MLA_CHAIN = 512
MLA_VP = MLA_V + 16
MLA_HEAD_GROUP = 2
ROPE_THETA = 10000.0
LOG2_E = 1.4426950408889634
NORM_EPS = 1e-6
N_MOD = 6
MOD_SHIFT_MIX, MOD_SCALE_MIX, MOD_GATE_MIX, MOD_SHIFT_FFN, MOD_SCALE_FFN, MOD_GATE_FFN = range(N_MOD)

V7X_LANES = 128
V7X_VMEM_LIMIT_BYTES = 56 * 1024 * 1024

ADA_COL_BLOCK = 1024
ROPE_TAB_TOKENS = 4096
INPROJ_ROW_TILE, INPROJ_COL_BLOCK = 1024, 1280
MLAPROJ_ROW_TILE = 512
MIXOUT_ROW_TILE = 256
FFN_ROW_TILE, FFN_HIDDEN_BLOCK = 512, 512
PACK_ROW_TILE = 128

MASK_VALUE = -1e30


def _params(*semantics):
    return pltpu.CompilerParams(dimension_semantics=semantics, vmem_limit_bytes=V7X_VMEM_LIMIT_BYTES)


def _dot(a, b):
    return jnp.dot(a, b, preferred_element_type=F32)


def _dot_nt(a, b):
    return lax.dot_general(a, b, (((1,), (1,)), ((), ())), preferred_element_type=F32)


def _dot_tn(a, b):
    return lax.dot_general(a, b, (((0,), (0,)), ((), ())), preferred_element_type=F32)


def _rms(x):
    return x * lax.rsqrt(jnp.mean(x * x, axis=-1, keepdims=True) + NORM_EPS)


def _modnorm(x, gain, scale, shift):
    return _rms(x) * (gain * (1.0 + scale)) + shift


def _tile(n, want):
    t = min(n, want)
    assert n % t == 0, (n, want)
    return t


def _ada_kernel(c_ref, w_ref, b_ref, o_ref):
    c = c_ref[...]
    act = (c * jax.nn.sigmoid(c)).astype(BF16)
    o_ref[...] = _dot(act, w_ref[...].astype(BF16)) + b_ref[...]


def _ada(c, w, b):
    bsz, d = c.shape
    n = w.shape[1]
    tn = _tile(n, ADA_COL_BLOCK)
    return pl.pallas_call(
        _ada_kernel,
        grid=(n // tn,),
        in_specs=[pl.BlockSpec((bsz, d), lambda j: (0, 0)),
                  pl.BlockSpec((d, tn), lambda j: (0, j)),
                  pl.BlockSpec((1, tn), lambda j: (0, j))],
        out_specs=pl.BlockSpec((bsz, tn), lambda j: (0, j)),
        out_shape=jax.ShapeDtypeStruct((bsz, n), F32),
        compiler_params=_params("arbitrary"),
        name="ada",
    )(c, w, b.reshape(1, n))


def _rope_tab_kernel(pos_ref, f_ref, cos_ref, sin_ref):
    ang = pos_ref[...].astype(F32) * f_ref[...]
    cos_ref[...] = jnp.cos(ang)
    sin_ref[...] = jnp.sin(ang)


def _rope_tab(positions):
    t = positions.size
    half = MLA_ROPE // 2
    inv_freq = ROPE_THETA ** (-jnp.arange(0, MLA_ROPE, 2, dtype=F32) / MLA_ROPE)
    tt = _tile(t, ROPE_TAB_TOKENS)
    spec = pl.BlockSpec((half, tt), lambda i: (0, i))
    return pl.pallas_call(
        _rope_tab_kernel,
        grid=(t // tt,),
        in_specs=[pl.BlockSpec((1, tt), lambda i: (0, i)), pl.BlockSpec((half, 1), lambda i: (0, 0))],
        out_specs=[spec, spec],
        out_shape=[jax.ShapeDtypeStruct((half, t), F32)] * 2,
        compiler_params=_params("arbitrary"),
        name="rope_tab",
    )(positions.reshape(1, t), inv_freq.reshape(half, 1))


def _inproj_kernel(x_ref, mod_ref, g_ref, w_ref, o_ref, h_ref):
    @pl.when(pl.program_id(1) == 0)
    def _():
        h = _modnorm(x_ref[...], g_ref[...],
                     mod_ref[0, MOD_SCALE_MIX:MOD_SCALE_MIX + 1, :], mod_ref[0, MOD_SHIFT_MIX:MOD_SHIFT_MIX + 1, :])
        h_ref[...] = h.astype(BF16)

    o_ref[...] = _dot(h_ref[...], w_ref[...]).astype(o_ref.dtype)


def _inproj(x2, mod3, gain, w, seq):
    t, d = x2.shape
    n = w.shape[1]
    tm = _tile(seq, INPROJ_ROW_TILE)
    tn = _tile(n, INPROJ_COL_BLOCK)
    return pl.pallas_call(
        _inproj_kernel,
        grid=(t // tm, n // tn),
        in_specs=[pl.BlockSpec((tm, d), lambda i, j: (i, 0)),
                  pl.BlockSpec((1, N_MOD, d), lambda i, j: ((i * tm) // seq, 0, 0)),
                  pl.BlockSpec((1, d), lambda i, j: (0, 0)),
                  pl.BlockSpec((d, tn), lambda i, j: (0, j))],
        out_specs=pl.BlockSpec((tm, tn), lambda i, j: (i, j)),
        out_shape=jax.ShapeDtypeStruct((t, n), BF16),
        scratch_shapes=[pltpu.VMEM((tm, d), BF16)],
        compiler_params=_params("parallel", "arbitrary"),
        name="inproj",
    )(x2, mod3, gain.reshape(1, d), w)


def _mlaproj_kernel(lat_ref, cos_ref, sin_ref, gq_ref, gkv_ref, wqt_ref, wk_ref, wvt_ref, qt_ref, k_ref, vt_ref, *,
                    q_rank, kv_rank, scale):
    tm = lat_ref.shape[0]
    cos = cos_ref[...]
    sin = sin_ref[...]
    cs = jnp.concatenate([cos, cos], axis=0)
    sn = jnp.concatenate([-sin, sin], axis=0)

    cqn = _rms(lat_ref[:, 0:q_rank].astype(F32)) * gq_ref[...]
    ckvn = _rms(lat_ref[:, q_rank:q_rank + kv_rank].astype(F32)) * gkv_ref[...]
    cqn_t = cqn.T.astype(BF16)
    ckvn_t = ckvn.T.astype(BF16)
    ckvn = ckvn.astype(BF16)

    kr_t = lat_ref[:, q_rank + kv_rank:q_rank + kv_rank + V7X_LANES].astype(F32).T
    k_rope_t = kr_t[:MLA_ROPE] * cs + kr_t[MLA_ROPE:] * sn
    k_rope = jnp.concatenate([k_rope_t, jnp.zeros_like(k_rope_t)], axis=0).T.astype(BF16)

    for h in range(MLA_HEADS):
        qt = _dot(wqt_ref[h * MLA_QW:(h + 1) * MLA_QW, :], cqn_t)
        q_rope = qt[MLA_NOPE:MLA_NOPE + MLA_ROPE] * cs + qt[MLA_NOPE + MLA_ROPE:] * sn
        qt_ref[h * MLA_QW:h * MLA_QW + MLA_NOPE, :] = (qt[:MLA_NOPE] * scale).astype(BF16)
        qt_ref[h * MLA_QW + MLA_NOPE:h * MLA_QW + MLA_NOPE + MLA_ROPE, :] = (q_rope * scale).astype(BF16)
        qt_ref[h * MLA_QW + MLA_NOPE + MLA_ROPE:(h + 1) * MLA_QW, :] = jnp.zeros((MLA_ROPE, tm), BF16)
        vt_ref[h, 0, :MLA_V, :] = _dot(wvt_ref[h * MLA_V:(h + 1) * MLA_V, :], ckvn_t).astype(BF16)
        vt_ref[h, 0, MLA_V:, :] = jnp.ones((MLA_VP - MLA_V, tm), BF16)
    for g in range(MLA_HEADS // 2):
        kk = _dot(ckvn, wk_ref[:, 2 * g * MLA_NOPE:2 * (g + 1) * MLA_NOPE]).astype(BF16)
        for e in range(2):
            h = 2 * g + e
            k_ref[:, h * MLA_QW:h * MLA_QW + MLA_NOPE] = kk[:, e * MLA_NOPE:(e + 1) * MLA_NOPE]
            k_ref[:, h * MLA_QW + MLA_NOPE:(h + 1) * MLA_QW] = k_rope


def _mlaproj(p, lat_block, lat_width, cos_t, sin_t, gq, gkv, wqt, wk, wvt):
    t = p.shape[0]
    q_rank, kv_rank = gq.shape[0], gkv.shape[0]
    tm = _tile(MLA_STRIP, MLAPROJ_ROW_TILE)
    per_strip = MLA_STRIP // tm
    half = MLA_ROPE // 2
    scale = (MLA_NOPE + MLA_ROPE) ** -0.5 * LOG2_E
    kern = functools.partial(_mlaproj_kernel, q_rank=q_rank, kv_rank=kv_rank, scale=scale)
    return pl.pallas_call(
        kern,
        grid=(t // tm,),
        in_specs=[pl.BlockSpec((tm, lat_width), lambda i: (i, lat_block)),
                  pl.BlockSpec((half, tm), lambda i: (0, i)),
                  pl.BlockSpec((half, tm), lambda i: (0, i)),
                  pl.BlockSpec((1, q_rank), lambda i: (0, 0)),
                  pl.BlockSpec((1, kv_rank), lambda i: (0, 0)),
                  pl.BlockSpec(wqt.shape, lambda i: (0, 0)),
                  pl.BlockSpec(wk.shape, lambda i: (0, 0)),
                  pl.BlockSpec(wvt.shape, lambda i: (0, 0))],
        out_specs=[pl.BlockSpec((MLA_HEADS * MLA_QW, tm), lambda i: (0, i)),
                   pl.BlockSpec((tm, MLA_HEADS * MLA_QW), lambda i: (i, 0)),
                   pl.BlockSpec((MLA_HEADS, 1, MLA_VP, tm), lambda i: (0, i // per_strip, 0, i % per_strip))],
        out_shape=[jax.ShapeDtypeStruct((MLA_HEADS * MLA_QW, t), BF16),
                   jax.ShapeDtypeStruct((t, MLA_HEADS * MLA_QW), BF16),
                   jax.ShapeDtypeStruct((MLA_HEADS, t // MLA_STRIP, MLA_VP, MLA_STRIP), BF16)],
        compiler_params=_params("parallel"),
        name="mlaproj",
    )(p, cos_t, sin_t, gq.reshape(1, q_rank), gkv.reshape(1, kv_rank), wqt, wk, wvt)


def _gla_kernel(q_ref, k_ref, v_ref, g_ref, gk_ref, gkw_ref, gkb_ref, on_ref, scan_ref, o_ref, st_ref, *, dk, dv):
    c = GLA_CHUNK
    nsub = q_ref.shape[0] // c

    @pl.when(pl.program_id(1) == 0)
    def _():
        st_ref[...] = jnp.zeros_like(st_ref)

    z = _dot(gk_ref[...], gkw_ref[...]) + gkb_ref[...]
    log_a = jax.nn.log_sigmoid(z) * (LOG2_E / GLA_GATE_NORMALIZER)

    halves = [1 << e for e in range(c.bit_length() - 1)]
    ai = lax.broadcasted_iota(jnp.int32, (c, c), 0)
    aj = lax.broadcasted_iota(jnp.int32, (c, c), 1)
    diff = ai ^ aj
    causal_diff = jnp.where(ai > aj, diff, 0)
    level_mask = {s: (causal_diff >> (s.bit_length() - 1)) == 1 for s in halves}
    diag_mask = ai == aj

    g_hi = log_a.astype(BF16)
    g_lo = (log_a - g_hi.astype(F32)).astype(BF16)
    n_lv = len(halves) * c

    def chunk_local(h, u):
        rows, cols = slice(u * c, (u + 1) * c), slice(h * dk, (h + 1) * dk)
        level_sum = _dot(scan_ref[:n_lv, :], g_hi[rows, cols])
        cum = _dot(scan_ref[n_lv:, :], jnp.concatenate([g_hi[rows, cols], g_lo[rows, cols]], axis=1))
        prefix = cum[:, :dk] + cum[:, dk:]
        total = prefix[c - 1:c, :]
        q16 = q_ref[rows, cols] * (dk ** -0.5)
        k16 = k_ref[rows, cols]
        att = jnp.where(diag_mask, _dot_nt(q16, k16), 0.0)
        for li, s in enumerate(halves):
            decay = jnp.exp2(level_sum[li * c:(li + 1) * c, :].astype(BF16))
            att = jnp.where(level_mask[s], _dot_nt(q16 * decay, k16 * decay), att)
        q_dec = (q16.astype(F32) * jnp.exp2(prefix)).astype(BF16)
        k_dec = (k16.astype(F32) * jnp.exp2(total - prefix)).astype(BF16)
        return q_dec, k_dec, att.astype(BF16), jnp.exp2(total)

    for h in range(GLA_HEADS):
        local = [chunk_local(h, u) for u in range(nsub)]
        for u, (q_dec, k_dec, att, state_decay) in enumerate(local):
            rows = slice(u * c, (u + 1) * c)
            vh = v_ref[rows, h * dv:(h + 1) * dv]
            st = st_ref[h]
            o = _dot_nt(q_dec, st.astype(BF16)) + _dot(att, vh)
            st_ref[h] = st * state_decay + _dot_tn(vh, k_dec)
            gate = g_ref[rows, h * dv:(h + 1) * dv].astype(F32)
            o_ref[rows, h * dv:(h + 1) * dv] = (_rms(o) * on_ref[...] * (gate * jax.nn.sigmoid(gate))).astype(o_ref.dtype)


def _gla_scan_matrix(c):
    p = jnp.arange(c)[:, None]
    t = jnp.arange(c)[None, :]
    mats = []
    s = 1
    while s < c:
        start = (p // s) * s
        odd = (p // s) % 2 == 1
        mats.append(jnp.where(odd, (t >= start) & (t <= p), (t > p) & (t < start + s)))
        s *= 2
    mats.append(t <= p)
    return jnp.concatenate(mats, axis=0).astype(BF16)


def _gla(p, gk_block, gkw, gkb, onorm_g, bsz, seq, dk, dv):
    t = p.shape[0]
    c = _tile(seq, GLA_CHUNK * GLA_CHUNKS_PER_STEP)
    nc = seq // c
    hk, hv = GLA_HEADS * dk, GLA_HEADS * dv
    scan = _gla_scan_matrix(GLA_CHUNK)
    kern = functools.partial(_gla_kernel, dk=dk, dv=dv)
    row = lambda b, i: b * nc + i
    return pl.pallas_call(
        kern,
        grid=(bsz, nc),
        in_specs=[pl.BlockSpec((c, hk), lambda b, i: (row(b, i), 0)),
                  pl.BlockSpec((c, hk), lambda b, i: (row(b, i), 1)),
                  pl.BlockSpec((c, hv), lambda b, i: (row(b, i), (2 * hk) // hv)),
                  pl.BlockSpec((c, hv), lambda b, i: (row(b, i), (2 * hk) // hv + 1)),
                  pl.BlockSpec((c, V7X_LANES), lambda b, i: (row(b, i), gk_block)),
                  pl.BlockSpec(gkw.shape, lambda b, i: (0, 0)),
                  pl.BlockSpec((1, hk), lambda b, i: (0, 0)),
                  pl.BlockSpec((1, dv), lambda b, i: (0, 0)),
                  pl.BlockSpec(scan.shape, lambda b, i: (0, 0))],
        out_specs=pl.BlockSpec((c, hv), lambda b, i: (row(b, i), 0)),
        out_shape=jax.ShapeDtypeStruct((t, hv), BF16),
        scratch_shapes=[pltpu.VMEM((GLA_HEADS, dv, dk), F32)],
        compiler_params=_params("parallel", "arbitrary"),
        name="gla",
    )(p, p, p, p, p, gkw, gkb.reshape(1, hk), onorm_g.reshape(1, dv), scan)


def _flash_kernel(qt_ref, k_ref, vt_ref, o_ref, acc_ref, sa_ref, sb_ref, pa_ref, pb_ref):
    strip, cw = MLA_STRIP, MLA_CHAIN
    hp, nch = acc_ref.shape[0], acc_ref.shape[1]
    chains = [(h, a) for h in range(hp) for a in range(nch)]
    per_block = MLA_QBLOCK // strip
    assert per_block == 2 and nch == 2 and strip == cw
    i = pl.program_id(2)
    acc_ref[...] = jnp.zeros_like(acc_ref)
    pb_ref[...] = jnp.zeros_like(pb_ref)

    def scores(h, a, strip_idx):
        start = pl.multiple_of(strip_idx * strip, strip)
        return _dot(k_ref[pl.ds(start, strip), h * MLA_QW:(h + 1) * MLA_QW],
                    qt_ref[h * MLA_QW:(h + 1) * MLA_QW, a * cw:(a + 1) * cw])

    def col_max(s):
        return jnp.max(s, axis=0, keepdims=True)

    def softmax(s, s_max, m):
        m_new = jnp.maximum(m, s_max)
        return jnp.exp2((s - m_new).astype(BF16)), jnp.exp2(m - m_new), m_new

    def accumulate(h, a, strip_idx, p, alpha):
        acc_ref[h, a] = alpha * acc_ref[h, a] + _dot(vt_ref[h, strip_idx], p)

    def half_step(s_cur, s_nxt, p_cur, p_prev, t, carry):
        out = []
        for c, (h, a) in enumerate(chains):
            m, cur_max, alpha_prev = carry[c]
            s = scores(h, a, t + 1)
            s_nxt[h, a] = s
            accumulate(h, a, jnp.maximum(t - 1, 0), p_prev[h, a], alpha_prev)
            p, alpha, m = softmax(s_cur[h, a], cur_max, m)
            p_cur[h, a] = p
            out.append((m, col_max(s), alpha))
        return tuple(out)

    def body(jj, carry):
        carry = half_step(sa_ref, sb_ref, pa_ref, pb_ref, 2 * jj, carry)
        return half_step(sb_ref, sa_ref, pb_ref, pa_ref, 2 * jj + 1, carry)

    init = []
    for h, a in chains:
        s = scores(h, a, 0)
        sa_ref[h, a] = s
        init.append((jnp.full((1, cw), MASK_VALUE, F32), col_max(s), jnp.ones((1, cw), F32)))
    carry = lax.fori_loop(0, i, body, tuple(init))
    first = per_block * i
    causal = lax.broadcasted_iota(jnp.int32, (strip, cw), 0) <= lax.broadcasted_iota(jnp.int32, (strip, cw), 1)
    last = [scores(h, 1, first + 1) for h in range(hp)]

    def update(h, a, strip_idx, s, s_max, m):
        p, alpha, m = softmax(s, s_max, m)
        accumulate(h, a, strip_idx, p, alpha)
        return m

    for c, (h, a) in enumerate(chains):
        m, s_max, alpha_prev = carry[c]
        accumulate(h, a, jnp.maximum(first - 1, 0), pb_ref[h, a], alpha_prev)
        if a == 0:
            s = jnp.where(causal, sa_ref[h, a], MASK_VALUE)
            update(h, a, first, s, col_max(s), m)
        else:
            m = update(h, a, first, sa_ref[h, a], s_max, m)
            s = jnp.where(causal, last[h], MASK_VALUE)
            update(h, a, first + 1, s, col_max(s), m)
        o_t = acc_ref[h, a, :MLA_V, :] / acc_ref[h, a, MLA_V:MLA_V + 1, :]
        o_ref[a * cw:(a + 1) * cw, h * MLA_V:(h + 1) * MLA_V] = o_t.T.astype(o_ref.dtype)


def _flash(qt, k, vt, bsz, seq):
    t = k.shape[0]
    assert seq % MLA_QBLOCK == 0
    nq = seq // MLA_QBLOCK
    hp = MLA_HEAD_GROUP
    nch = MLA_QBLOCK // MLA_CHAIN
    return pl.pallas_call(
        _flash_kernel,
        grid=(bsz, MLA_HEADS // hp, nq),
        in_specs=[pl.BlockSpec((hp * MLA_QW, MLA_QBLOCK), lambda b, g, i: (g, b * nq + i)),
                  pl.BlockSpec((seq, hp * MLA_QW), lambda b, g, i: (b, g)),
                  pl.BlockSpec((hp, seq // MLA_STRIP, MLA_VP, MLA_STRIP), lambda b, g, i: (g, b, 0, 0))],
        out_specs=pl.BlockSpec((MLA_QBLOCK, hp * MLA_V), lambda b, g, i: (b * nq + i, g)),
        out_shape=jax.ShapeDtypeStruct((t, MLA_HEADS * MLA_V), BF16),
        scratch_shapes=[pltpu.VMEM((hp, nch, MLA_VP, MLA_CHAIN), F32),
                        pltpu.VMEM((hp, nch, MLA_STRIP, MLA_CHAIN), F32),
                        pltpu.VMEM((hp, nch, MLA_STRIP, MLA_CHAIN), F32),
                        pltpu.VMEM((hp, nch, MLA_STRIP, MLA_CHAIN), BF16),
                        pltpu.VMEM((hp, nch, MLA_STRIP, MLA_CHAIN), BF16)],
        compiler_params=_params("parallel", "parallel", "arbitrary"),
        name="flash",
    )(qt, k, vt)


def _mixout_kernel(og_ref, om_ref, ga_ref, gb_ref, x_ref, mod_ref, wg_ref, wm_ref, wo_ref, o_ref):
    yg = _dot(og_ref[...], wg_ref[...])
    ym = _dot(om_ref[...], wm_ref[...])
    ga = jax.nn.sigmoid(ga_ref[...].astype(F32))
    gb = jax.nn.sigmoid(gb_ref[...].astype(F32))
    merged = (ga * yg + gb * ym).astype(BF16)
    o_ref[...] = x_ref[...] + mod_ref[0, MOD_GATE_MIX:MOD_GATE_MIX + 1, :] * _dot(merged, wo_ref[...])


def _mixout(o_gla, o_mla, p, gate_col, x2, mod3, w_gla, w_mla, w_out, seq):
    t, d = x2.shape
    tm = _tile(seq, MIXOUT_ROW_TILE)
    assert gate_col % d == 0
    row = lambda i: (i, 0)
    resident = lambda w: pl.BlockSpec(w.shape, lambda i: (0, 0), pipeline_mode=pl.Buffered(1))
    return pl.pallas_call(
        _mixout_kernel,
        grid=(t // tm,),
        in_specs=[pl.BlockSpec((tm, o_gla.shape[1]), row),
                  pl.BlockSpec((tm, o_mla.shape[1]), row),
                  pl.BlockSpec((tm, d), lambda i: (i, gate_col // d)),
                  pl.BlockSpec((tm, d), lambda i: (i, gate_col // d + 1)),
                  pl.BlockSpec((tm, d), row),
                  pl.BlockSpec((1, N_MOD, d), lambda i: ((i * tm) // seq, 0, 0)),
                  resident(w_gla), resident(w_mla), resident(w_out)],
        out_specs=pl.BlockSpec((tm, d), row),
        out_shape=jax.ShapeDtypeStruct((t, d), F32),
        compiler_params=_params("parallel"),
        name="mixout",
    )(o_gla, o_mla, p, p, x2, mod3, w_gla, w_mla, w_out)


def _ffn_kernel(x_ref, mod_ref, g_ref, wg_ref, wu_ref, wd_ref, fg_ref, o_ref, h_ref, acc_ref, *, final_norm):
    f = pl.program_id(1)

    @pl.when(f == 0)
    def _():
        h = _modnorm(x_ref[...], g_ref[...],
                     mod_ref[0, MOD_SCALE_FFN:MOD_SCALE_FFN + 1, :], mod_ref[0, MOD_SHIFT_FFN:MOD_SHIFT_FFN + 1, :])
        h_ref[...] = h.astype(BF16)
        acc_ref[...] = jnp.zeros_like(acc_ref)

    h = h_ref[...]
    gate = _dot(h, wg_ref[...])
    up = _dot(h, wu_ref[...])
    act = (gate * jax.nn.sigmoid(gate) * up).astype(BF16)
    acc_ref[...] += _dot(act, wd_ref[...])

    @pl.when(f == pl.num_programs(1) - 1)
    def _():
        y = x_ref[...] + mod_ref[0, MOD_GATE_FFN:MOD_GATE_FFN + 1, :] * acc_ref[...]
        if final_norm:
            y = _rms(y) * fg_ref[...]
        o_ref[...] = y


def _ffn(x1, mod3, gain, w_in, w_down, final_g, seq, final_norm):
    t, d = x1.shape
    hidden = w_down.shape[0]
    tm = _tile(seq, FFN_ROW_TILE)
    tf = _tile(hidden, FFN_HIDDEN_BLOCK)
    nf = hidden // tf
    kern = functools.partial(_ffn_kernel, final_norm=final_norm)
    return pl.pallas_call(
        kern,
        grid=(t // tm, nf),
        in_specs=[pl.BlockSpec((tm, d), lambda i, f: (i, 0)),
                  pl.BlockSpec((1, N_MOD, d), lambda i, f: ((i * tm) // seq, 0, 0)),
                  pl.BlockSpec((1, d), lambda i, f: (0, 0)),
                  pl.BlockSpec((d, tf), lambda i, f: (0, f)),
                  pl.BlockSpec((d, tf), lambda i, f: (0, nf + f)),
                  pl.BlockSpec((tf, d), lambda i, f: (f, 0)),
                  pl.BlockSpec((1, d), lambda i, f: (0, 0))],
        out_specs=pl.BlockSpec((tm, d), lambda i, f: (i, 0)),
        out_shape=jax.ShapeDtypeStruct((t, d), F32),
        scratch_shapes=[pltpu.VMEM((tm, d), BF16), pltpu.VMEM((tm, d), F32)],
        compiler_params=_params("parallel", "arbitrary"),
        name="ffn",
    )(x1, mod3, gain.reshape(1, d), w_in, w_in, w_down, final_g.reshape(1, d))


def _pack_w_in_kernel(wt_ref, o_ref, *, moves, kr_row, gk_row, tail_col):
    for dst, src, width in moves:
        o_ref[:, dst:dst + width] = wt_ref[src:src + width, :].T.astype(BF16)
    half = MLA_ROPE // 2
    kr = wt_ref[kr_row:kr_row + MLA_ROPE, :]
    gk = wt_ref[gk_row:gk_row + GLA_GATE_RANK, :]
    pad = jnp.zeros((V7X_LANES - GLA_GATE_RANK, kr.shape[1]), F32)
    o_ref[:, tail_col:] = jnp.concatenate([kr, kr[half:], kr[:half], gk, pad], axis=0).T.astype(BF16)


def _pack_w_in(w, d, dk_all, dv_all, q_rank, kv_rank):
    widths = (dk_all, dk_all, dv_all, dv_all, GLA_GATE_RANK, q_rank, kv_rank, MLA_ROPE, d, d)
    offs = [0]
    for wd in widths:
        offs.append(offs[-1] + wd)
    gla_w = offs[4]
    moves = ((0, 0, gla_w), (gla_w, offs[8], 2 * d), (gla_w + 2 * d, offs[5], q_rank + kv_rank))
    tail_col = gla_w + 2 * d + q_rank + kv_rank
    n_out = tail_col + 2 * V7X_LANES
    assert all(dst % V7X_LANES == 0 for dst, _, _ in moves) and tail_col % V7X_LANES == 0
    rows = w.shape[0]
    tr = _tile(rows, PACK_ROW_TILE)
    kern = functools.partial(_pack_w_in_kernel, moves=moves, kr_row=offs[7], gk_row=offs[4], tail_col=tail_col)
    return pl.pallas_call(
        kern,
        grid=(rows // tr,),
        in_specs=[pl.BlockSpec((w.shape[1], tr), lambda i: (0, i))],
        out_specs=pl.BlockSpec((tr, n_out), lambda i: (i, 0)),
        out_shape=jax.ShapeDtypeStruct((rows, n_out), BF16),
        compiler_params=_params("parallel"),
        name="pack_w_in",
    )(w.T)


def _pack_wuq_t(w):
    r = w.shape[0]
    half = MLA_ROPE // 2
    w3 = w.reshape(r, MLA_HEADS, MLA_NOPE + MLA_ROPE)
    rope = w3[:, :, MLA_NOPE:]
    rot = jnp.concatenate([rope[:, :, half:], rope[:, :, :half]], axis=2)
    return jnp.concatenate([w3[:, :, :MLA_NOPE], rope, rot], axis=2).reshape(r, -1).T.astype(BF16)


def _pack_wukv(w):
    r = w.shape[0]
    w3 = w.reshape(r, MLA_HEADS, MLA_NOPE + MLA_V)
    wk = w3[:, :, :MLA_NOPE].reshape(r, -1).astype(BF16)
    wvt = w3[:, :, MLA_NOPE:].reshape(r, -1).T.astype(BF16)
    return wk, wvt


def kernel(x, c, positions, ada_w, ada_b, norm_mix_g, w_in, gla_gk_w, gla_gk_b, gla_onorm_g, gla_wo, mla_q_norm_g, mla_wuq, mla_kv_norm_g, mla_wukv, mla_wo, w_out, norm_ffn_g, ffn_w_in, ffn_w_down, final_norm_g):
    bsz, seq, d = x.shape
    depth = w_in.shape[0]
    dk_all = gla_gk_w.shape[2]
    dv_all = gla_wo.shape[1]
    dk, dv = dk_all // GLA_HEADS, dv_all // GLA_HEADS
    q_rank, kv_rank = mla_wuq.shape[1], mla_wukv.shape[1]
    lat_col = 2 * dk_all + 2 * dv_all + 2 * d
    lat_width = q_rank + kv_rank + 2 * V7X_LANES
    gate_col = 2 * dk_all + 2 * dv_all
    assert lat_col % lat_width == 0 and (lat_col + lat_width - V7X_LANES) % V7X_LANES == 0

    x2 = x.reshape(bsz * seq, d)
    cos_t, sin_t = _rope_tab(positions)
    for l in range(depth):
        mod3 = _ada(c, ada_w[l], ada_b[l]).reshape(bsz, N_MOD, d)
        w_all = _pack_w_in(w_in[l], d, dk_all, dv_all, q_rank, kv_rank)
        p = _inproj(x2, mod3, norm_mix_g[l], w_all, seq)
        wk, wvt = _pack_wukv(mla_wukv[l])
        qt, k, vt = _mlaproj(p, lat_col // lat_width, lat_width, cos_t, sin_t, mla_q_norm_g[l], mla_kv_norm_g[l],
                             _pack_wuq_t(mla_wuq[l]), wk, wvt)
        gkw = jnp.zeros((V7X_LANES, dk_all), BF16).at[:GLA_GATE_RANK].set(gla_gk_w[l].astype(BF16))
        o_gla = _gla(p, (lat_col + lat_width - V7X_LANES) // V7X_LANES, gkw, gla_gk_b[l], gla_onorm_g[l],
                     bsz, seq, dk, dv)
        o_mla = _flash(qt, k, vt, bsz, seq)
        x1 = _mixout(o_gla, o_mla, p, gate_col, x2, mod3, gla_wo[l].astype(BF16), mla_wo[l].astype(BF16),
                     w_out[l].astype(BF16), seq)
        x2 = _ffn(x1, mod3, norm_ffn_g[l], ffn_w_in[l].astype(BF16), ffn_w_down[l].astype(BF16), final_norm_g,
                  seq, final_norm=(l == depth - 1))
    return x2.reshape(bsz, seq, d)
```

```python
import functools

import jax
import jax.numpy as jnp
from jax import lax
from jax.experimental import pallas as pl
from jax.experimental.pallas import tpu as pltpu

F32 = jnp.float32
BF16 = jnp.bfloat16

GLA_HEADS = 4
GLA_GATE_RANK = 16
GLA_GATE_NORMALIZER = 16.0
GLA_CHUNK = 128
GLA_CHUNKS_PER_STEP = 4
MLA_HEADS = 16
MLA_NOPE = 128
MLA_ROPE = 64
MLA_V = 128
MLA_QW = MLA_NOPE + 2 * MLA_ROPE
MLA_QBLOCK = 1024
MLA_STRIP = 512
MLA_CHAIN = 512
MLA_VP = MLA_V + 16
MLA_HEAD_GROUP = 2
ROPE_THETA = 10000.0
LOG2_E = 1.4426950408889634
NORM_EPS = 1e-6
N_MOD = 6
MOD_SHIFT_MIX, MOD_SCALE_MIX, MOD_GATE_MIX, MOD_SHIFT_FFN, MOD_SCALE_FFN, MOD_GATE_FFN = range(N_MOD)

V7X_LANES = 128
V7X_VMEM_LIMIT_BYTES = 56 * 1024 * 1024

ADA_COL_BLOCK = 1024
ROPE_TAB_TOKENS = 4096
INPROJ_ROW_TILE, INPROJ_COL_BLOCK = 1024, 1280
MLAPROJ_ROW_TILE = 512
MIXOUT_ROW_TILE = 256
FFN_ROW_TILE, FFN_HIDDEN_BLOCK = 512, 512
PACK_ROW_TILE = 128

MASK_VALUE = -1e30


def _params(*semantics):
    return pltpu.CompilerParams(dimension_semantics=semantics, vmem_limit_bytes=V7X_VMEM_LIMIT_BYTES)


def _dot(a, b):
    return jnp.dot(a, b, preferred_element_type=F32)


def _dot_nt(a, b):
    return lax.dot_general(a, b, (((1,), (1,)), ((), ())), preferred_element_type=F32)


def _dot_tn(a, b):
    return lax.dot_general(a, b, (((0,), (0,)), ((), ())), preferred_element_type=F32)


def _rms(x):
    return x * lax.rsqrt(jnp.mean(x * x, axis=-1, keepdims=True) + NORM_EPS)


def _modnorm(x, gain, scale, shift):
    return _rms(x) * (gain * (1.0 + scale)) + shift


def _tile(n, want):
    t = min(n, want)
    assert n % t == 0, (n, want)
    return t


def _ada_kernel(c_ref, w_ref, b_ref, o_ref):
    c = c_ref[...]
    act = (c * jax.nn.sigmoid(c)).astype(BF16)
    o_ref[...] = _dot(act, w_ref[...].astype(BF16)) + b_ref[...]


def _ada(c, w, b):
    bsz, d = c.shape
    n = w.shape[1]
    tn = _tile(n, ADA_COL_BLOCK)
    return pl.pallas_call(
        _ada_kernel,
        grid=(n // tn,),
        in_specs=[pl.BlockSpec((bsz, d), lambda j: (0, 0)),
                  pl.BlockSpec((d, tn), lambda j: (0, j)),
                  pl.BlockSpec((1, tn), lambda j: (0, j))],
        out_specs=pl.BlockSpec((bsz, tn), lambda j: (0, j)),
        out_shape=jax.ShapeDtypeStruct((bsz, n), F32),
        compiler_params=_params("arbitrary"),
        name="ada",
    )(c, w, b.reshape(1, n))


def _rope_tab_kernel(pos_ref, f_ref, cos_ref, sin_ref):
    ang = pos_ref[...].astype(F32) * f_ref[...]
    cos_ref[...] = jnp.cos(ang)
    sin_ref[...] = jnp.sin(ang)


def _rope_tab(positions):
    t = positions.size
    half = MLA_ROPE // 2
    inv_freq = ROPE_THETA ** (-jnp.arange(0, MLA_ROPE, 2, dtype=F32) / MLA_ROPE)
    tt = _tile(t, ROPE_TAB_TOKENS)
    spec = pl.BlockSpec((half, tt), lambda i: (0, i))
    return pl.pallas_call(
        _rope_tab_kernel,
        grid=(t // tt,),
        in_specs=[pl.BlockSpec((1, tt), lambda i: (0, i)), pl.BlockSpec((half, 1), lambda i: (0, 0))],
        out_specs=[spec, spec],
        out_shape=[jax.ShapeDtypeStruct((half, t), F32)] * 2,
        compiler_params=_params("arbitrary"),
        name="rope_tab",
    )(positions.reshape(1, t), inv_freq.reshape(half, 1))


def _inproj_kernel(x_ref, mod_ref, g_ref, w_ref, o_ref, h_ref):
    @pl.when(pl.program_id(1) == 0)
    def _():
        h = _modnorm(x_ref[...], g_ref[...],
                     mod_ref[0, MOD_SCALE_MIX:MOD_SCALE_MIX + 1, :], mod_ref[0, MOD_SHIFT_MIX:MOD_SHIFT_MIX + 1, :])
        h_ref[...] = h.astype(BF16)

    o_ref[...] = _dot(h_ref[...], w_ref[...]).astype(o_ref.dtype)


def _inproj(x2, mod3, gain, w, seq):
    t, d = x2.shape
    n = w.shape[1]
    tm = _tile(seq, INPROJ_ROW_TILE)
    tn = _tile(n, INPROJ_COL_BLOCK)
    return pl.pallas_call(
        _inproj_kernel,
        grid=(t // tm, n // tn),
        in_specs=[pl.BlockSpec((tm, d), lambda i, j: (i, 0)),
                  pl.BlockSpec((1, N_MOD, d), lambda i, j: ((i * tm) // seq, 0, 0)),
                  pl.BlockSpec((1, d), lambda i, j: (0, 0)),
                  pl.BlockSpec((d, tn), lambda i, j: (0, j))],
        out_specs=pl.BlockSpec((tm, tn), lambda i, j: (i, j)),
        out_shape=jax.ShapeDtypeStruct((t, n), BF16),
        scratch_shapes=[pltpu.VMEM((tm, d), BF16)],
        compiler_params=_params("parallel", "arbitrary"),
        name="inproj",
    )(x2, mod3, gain.reshape(1, d), w)


def _mlaproj_kernel(lat_ref, cos_ref, sin_ref, gq_ref, gkv_ref, wqt_ref, wk_ref, wvt_ref, qt_ref, k_ref, vt_ref, *,
                    q_rank, kv_rank, scale):
    tm = lat_ref.shape[0]
    cos = cos_ref[...]
    sin = sin_ref[...]
    cs = jnp.concatenate([cos, cos], axis=0)
    sn = jnp.concatenate([-sin, sin], axis=0)

    cqn = _rms(lat_ref[:, 0:q_rank].astype(F32)) * gq_ref[...]
    ckvn = _rms(lat_ref[:, q_rank:q_rank + kv_rank].astype(F32)) * gkv_ref[...]
    cqn_t = cqn.T.astype(BF16)
    ckvn_t = ckvn.T.astype(BF16)
    ckvn = ckvn.astype(BF16)

    kr_t = lat_ref[:, q_rank + kv_rank:q_rank + kv_rank + V7X_LANES].astype(F32).T
    k_rope_t = kr_t[:MLA_ROPE] * cs + kr_t[MLA_ROPE:] * sn
    k_rope = jnp.concatenate([k_rope_t, jnp.zeros_like(k_rope_t)], axis=0).T.astype(BF16)

    for h in range(MLA_HEADS):
        qt = _dot(wqt_ref[h * MLA_QW:(h + 1) * MLA_QW, :], cqn_t)
        q_rope = qt[MLA_NOPE:MLA_NOPE + MLA_ROPE] * cs + qt[MLA_NOPE + MLA_ROPE:] * sn
        qt_ref[h * MLA_QW:h * MLA_QW + MLA_NOPE, :] = (qt[:MLA_NOPE] * scale).astype(BF16)
        qt_ref[h * MLA_QW + MLA_NOPE:h * MLA_QW + MLA_NOPE + MLA_ROPE, :] = (q_rope * scale).astype(BF16)
        qt_ref[h * MLA_QW + MLA_NOPE + MLA_ROPE:(h + 1) * MLA_QW, :] = jnp.zeros((MLA_ROPE, tm), BF16)
        vt_ref[h, 0, :MLA_V, :] = _dot(wvt_ref[h * MLA_V:(h + 1) * MLA_V, :], ckvn_t).astype(BF16)
        vt_ref[h, 0, MLA_V:, :] = jnp.ones((MLA_VP - MLA_V, tm), BF16)
    for g in range(MLA_HEADS // 2):
        kk = _dot(ckvn, wk_ref[:, 2 * g * MLA_NOPE:2 * (g + 1) * MLA_NOPE]).astype(BF16)
        for e in range(2):
            h = 2 * g + e
            k_ref[:, h * MLA_QW:h * MLA_QW + MLA_NOPE] = kk[:, e * MLA_NOPE:(e + 1) * MLA_NOPE]
            k_ref[:, h * MLA_QW + MLA_NOPE:(h + 1) * MLA_QW] = k_rope


def _mlaproj(p, lat_block, lat_width, cos_t, sin_t, gq, gkv, wqt, wk, wvt):
    t = p.shape[0]
    q_rank, kv_rank = gq.shape[0], gkv.shape[0]
    tm = _tile(MLA_STRIP, MLAPROJ_ROW_TILE)
    per_strip = MLA_STRIP // tm
    half = MLA_ROPE // 2
    scale = (MLA_NOPE + MLA_ROPE) ** -0.5 * LOG2_E
    kern = functools.partial(_mlaproj_kernel, q_rank=q_rank, kv_rank=kv_rank, scale=scale)
    return pl.pallas_call(
        kern,
        grid=(t // tm,),
        in_specs=[pl.BlockSpec((tm, lat_width), lambda i: (i, lat_block)),
                  pl.BlockSpec((half, tm), lambda i: (0, i)),
                  pl.BlockSpec((half, tm), lambda i: (0, i)),
                  pl.BlockSpec((1, q_rank), lambda i: (0, 0)),
                  pl.BlockSpec((1, kv_rank), lambda i: (0, 0)),
                  pl.BlockSpec(wqt.shape, lambda i: (0, 0)),
                  pl.BlockSpec(wk.shape, lambda i: (0, 0)),
                  pl.BlockSpec(wvt.shape, lambda i: (0, 0))],
        out_specs=[pl.BlockSpec((MLA_HEADS * MLA_QW, tm), lambda i: (0, i)),
                   pl.BlockSpec((tm, MLA_HEADS * MLA_QW), lambda i: (i, 0)),
                   pl.BlockSpec((MLA_HEADS, 1, MLA_VP, tm), lambda i: (0, i // per_strip, 0, i % per_strip))],
        out_shape=[jax.ShapeDtypeStruct((MLA_HEADS * MLA_QW, t), BF16),
                   jax.ShapeDtypeStruct((t, MLA_HEADS * MLA_QW), BF16),
                   jax.ShapeDtypeStruct((MLA_HEADS, t // MLA_STRIP, MLA_VP, MLA_STRIP), BF16)],
        compiler_params=_params("parallel"),
        name="mlaproj",
    )(p, cos_t, sin_t, gq.reshape(1, q_rank), gkv.reshape(1, kv_rank), wqt, wk, wvt)


def _gla_kernel(q_ref, k_ref, v_ref, g_ref, gk_ref, gkw_ref, gkb_ref, on_ref, scan_ref, o_ref, st_ref, *, dk, dv):
    c = GLA_CHUNK
    nsub = q_ref.shape[0] // c

    @pl.when(pl.program_id(1) == 0)
    def _():
        st_ref[...] = jnp.zeros_like(st_ref)

    z = _dot(gk_ref[...], gkw_ref[...]) + gkb_ref[...]
    log_a = jax.nn.log_sigmoid(z) * (LOG2_E / GLA_GATE_NORMALIZER)

    halves = [1 << e for e in range(c.bit_length() - 1)]
    ai = lax.broadcasted_iota(jnp.int32, (c, c), 0)
    aj = lax.broadcasted_iota(jnp.int32, (c, c), 1)
    diff = ai ^ aj
    causal_diff = jnp.where(ai > aj, diff, 0)
    level_mask = {s: (causal_diff >> (s.bit_length() - 1)) == 1 for s in halves}
    diag_mask = ai == aj

    g_hi = log_a.astype(BF16)
    g_lo = (log_a - g_hi.astype(F32)).astype(BF16)
    n_lv = len(halves) * c

    def chunk_local(h, u):
        rows, cols = slice(u * c, (u + 1) * c), slice(h * dk, (h + 1) * dk)
        level_sum = _dot(scan_ref[:n_lv, :], g_hi[rows, cols])
        cum = _dot(scan_ref[n_lv:, :], jnp.concatenate([g_hi[rows, cols], g_lo[rows, cols]], axis=1))
        prefix = cum[:, :dk] + cum[:, dk:]
        total = prefix[c - 1:c, :]
        q16 = q_ref[rows, cols] * (dk ** -0.5)
        k16 = k_ref[rows, cols]
        att = jnp.where(diag_mask, _dot_nt(q16, k16), 0.0)
        for li, s in enumerate(halves):
            decay = jnp.exp2(level_sum[li * c:(li + 1) * c, :].astype(BF16))
            att = jnp.where(level_mask[s], _dot_nt(q16 * decay, k16 * decay), att)
        q_dec = (q16.astype(F32) * jnp.exp2(prefix)).astype(BF16)
        k_dec = (k16.astype(F32) * jnp.exp2(total - prefix)).astype(BF16)
        return q_dec, k_dec, att.astype(BF16), jnp.exp2(total)

    for h in range(GLA_HEADS):
        local = [chunk_local(h, u) for u in range(nsub)]
        for u, (q_dec, k_dec, att, state_decay) in enumerate(local):
            rows = slice(u * c, (u + 1) * c)
            vh = v_ref[rows, h * dv:(h + 1) * dv]
            st = st_ref[h]
            o = _dot_nt(q_dec, st.astype(BF16)) + _dot(att, vh)
            st_ref[h] = st * state_decay + _dot_tn(vh, k_dec)
            gate = g_ref[rows, h * dv:(h + 1) * dv].astype(F32)
            o_ref[rows, h * dv:(h + 1) * dv] = (_rms(o) * on_ref[...] * (gate * jax.nn.sigmoid(gate))).astype(o_ref.dtype)


def _gla_scan_matrix(c):
    p = jnp.arange(c)[:, None]
    t = jnp.arange(c)[None, :]
    mats = []
    s = 1
    while s < c:
        start = (p // s) * s
        odd = (p // s) % 2 == 1
        mats.append(jnp.where(odd, (t >= start) & (t <= p), (t > p) & (t < start + s)))
        s *= 2
    mats.append(t <= p)
    return jnp.concatenate(mats, axis=0).astype(BF16)


def _gla(p, gk_block, gkw, gkb, onorm_g, bsz, seq, dk, dv):
    t = p.shape[0]
    c = _tile(seq, GLA_CHUNK * GLA_CHUNKS_PER_STEP)
    nc = seq // c
    hk, hv = GLA_HEADS * dk, GLA_HEADS * dv
    scan = _gla_scan_matrix(GLA_CHUNK)
    kern = functools.partial(_gla_kernel, dk=dk, dv=dv)
    row = lambda b, i: b * nc + i
    return pl.pallas_call(
        kern,
        grid=(bsz, nc),
        in_specs=[pl.BlockSpec((c, hk), lambda b, i: (row(b, i), 0)),
                  pl.BlockSpec((c, hk), lambda b, i: (row(b, i), 1)),
                  pl.BlockSpec((c, hv), lambda b, i: (row(b, i), (2 * hk) // hv)),
                  pl.BlockSpec((c, hv), lambda b, i: (row(b, i), (2 * hk) // hv + 1)),
                  pl.BlockSpec((c, V7X_LANES), lambda b, i: (row(b, i), gk_block)),
                  pl.BlockSpec(gkw.shape, lambda b, i: (0, 0)),
                  pl.BlockSpec((1, hk), lambda b, i: (0, 0)),
                  pl.BlockSpec((1, dv), lambda b, i: (0, 0)),
                  pl.BlockSpec(scan.shape, lambda b, i: (0, 0))],
        out_specs=pl.BlockSpec((c, hv), lambda b, i: (row(b, i), 0)),
        out_shape=jax.ShapeDtypeStruct((t, hv), BF16),
        scratch_shapes=[pltpu.VMEM((GLA_HEADS, dv, dk), F32)],
        compiler_params=_params("parallel", "arbitrary"),
        name="gla",
    )(p, p, p, p, p, gkw, gkb.reshape(1, hk), onorm_g.reshape(1, dv), scan)


def _flash_kernel(qt_ref, k_ref, vt_ref, o_ref, acc_ref, sa_ref, sb_ref, pa_ref, pb_ref):
    strip, cw = MLA_STRIP, MLA_CHAIN
    hp, nch = acc_ref.shape[0], acc_ref.shape[1]
    chains = [(h, a) for h in range(hp) for a in range(nch)]
    per_block = MLA_QBLOCK // strip
    assert per_block == 2 and nch == 2 and strip == cw
    i = pl.program_id(2)
    acc_ref[...] = jnp.zeros_like(acc_ref)
    pb_ref[...] = jnp.zeros_like(pb_ref)

    def scores(h, a, strip_idx):
        start = pl.multiple_of(strip_idx * strip, strip)
        return _dot(k_ref[pl.ds(start, strip), h * MLA_QW:(h + 1) * MLA_QW],
                    qt_ref[h * MLA_QW:(h + 1) * MLA_QW, a * cw:(a + 1) * cw])

    def col_max(s):
        return jnp.max(s, axis=0, keepdims=True)

    def softmax(s, s_max, m):
        m_new = jnp.maximum(m, s_max)
        return jnp.exp2((s - m_new).astype(BF16)), jnp.exp2(m - m_new), m_new

    def accumulate(h, a, strip_idx, p, alpha):
        acc_ref[h, a] = alpha * acc_ref[h, a] + _dot(vt_ref[h, strip_idx], p)

    def half_step(s_cur, s_nxt, p_cur, p_prev, t, carry):
        out = []
        for c, (h, a) in enumerate(chains):
            m, cur_max, alpha_prev = carry[c]
            s = scores(h, a, t + 1)
            s_nxt[h, a] = s
            accumulate(h, a, jnp.maximum(t - 1, 0), p_prev[h, a], alpha_prev)
            p, alpha, m = softmax(s_cur[h, a], cur_max, m)
            p_cur[h, a] = p
            out.append((m, col_max(s), alpha))
        return tuple(out)

    def body(jj, carry):
        carry = half_step(sa_ref, sb_ref, pa_ref, pb_ref, 2 * jj, carry)
        return half_step(sb_ref, sa_ref, pb_ref, pa_ref, 2 * jj + 1, carry)

    init = []
    for h, a in chains:
        s = scores(h, a, 0)
        sa_ref[h, a] = s
        init.append((jnp.full((1, cw), MASK_VALUE, F32), col_max(s), jnp.ones((1, cw), F32)))
    carry = lax.fori_loop(0, i, body, tuple(init))
    first = per_block * i
    causal = lax.broadcasted_iota(jnp.int32, (strip, cw), 0) <= lax.broadcasted_iota(jnp.int32, (strip, cw), 1)
    last = [scores(h, 1, first + 1) for h in range(hp)]

    def update(h, a, strip_idx, s, s_max, m):
        p, alpha, m = softmax(s, s_max, m)
        accumulate(h, a, strip_idx, p, alpha)
        return m

    for c, (h, a) in enumerate(chains):
        m, s_max, alpha_prev = carry[c]
        accumulate(h, a, jnp.maximum(first - 1, 0), pb_ref[h, a], alpha_prev)
        if a == 0:
            s = jnp.where(causal, sa_ref[h, a], MASK_VALUE)
            update(h, a, first, s, col_max(s), m)
        else:
            m = update(h, a, first, sa_ref[h, a], s_max, m)
            s = jnp.where(causal, last[h], MASK_VALUE)
            update(h, a, first + 1, s, col_max(s), m)
        o_t = acc_ref[h, a, :MLA_V, :] / acc_ref[h, a, MLA_V:MLA_V + 1, :]
        o_ref[a * cw:(a + 1) * cw, h * MLA_V:(h + 1) * MLA_V] = o_t.T.astype(o_ref.dtype)


def _flash(qt, k, vt, bsz, seq):
    t = k.shape[0]
    assert seq % MLA_QBLOCK == 0
    nq = seq // MLA_QBLOCK
    hp = MLA_HEAD_GROUP
    nch = MLA_QBLOCK // MLA_CHAIN
    return pl.pallas_call(
        _flash_kernel,
        grid=(bsz, MLA_HEADS // hp, nq),
        in_specs=[pl.BlockSpec((hp * MLA_QW, MLA_QBLOCK), lambda b, g, i: (g, b * nq + i)),
                  pl.BlockSpec((seq, hp * MLA_QW), lambda b, g, i: (b, g)),
                  pl.BlockSpec((hp, seq // MLA_STRIP, MLA_VP, MLA_STRIP), lambda b, g, i: (g, b, 0, 0))],
        out_specs=pl.BlockSpec((MLA_QBLOCK, hp * MLA_V), lambda b, g, i: (b * nq + i, g)),
        out_shape=jax.ShapeDtypeStruct((t, MLA_HEADS * MLA_V), BF16),
        scratch_shapes=[pltpu.VMEM((hp, nch, MLA_VP, MLA_CHAIN), F32),
                        pltpu.VMEM((hp, nch, MLA_STRIP, MLA_CHAIN), F32),
                        pltpu.VMEM((hp, nch, MLA_STRIP, MLA_CHAIN), F32),
                        pltpu.VMEM((hp, nch, MLA_STRIP, MLA_CHAIN), BF16),
                        pltpu.VMEM((hp, nch, MLA_STRIP, MLA_CHAIN), BF16)],
        compiler_params=_params("parallel", "parallel", "arbitrary"),
        name="flash",
    )(qt, k, vt)


def _mixout_kernel(og_ref, om_ref, ga_ref, gb_ref, x_ref, mod_ref, gf_ref, wg_ref, wm_ref, wo_ref, o_ref, h_ref):
    yg = _dot(og_ref[...], wg_ref[...])
    ym = _dot(om_ref[...], wm_ref[...])
    ga = jax.nn.sigmoid(ga_ref[...].astype(F32))
    gb = jax.nn.sigmoid(gb_ref[...].astype(F32))
    merged = (ga * yg + gb * ym).astype(BF16)
    x1 = x_ref[...] + mod_ref[0, MOD_GATE_MIX:MOD_GATE_MIX + 1, :] * _dot(merged, wo_ref[...])
    o_ref[...] = x1
    h_ref[...] = _modnorm(x1, gf_ref[...], mod_ref[0, MOD_SCALE_FFN:MOD_SCALE_FFN + 1, :],
                          mod_ref[0, MOD_SHIFT_FFN:MOD_SHIFT_FFN + 1, :]).astype(BF16)


def _mixout(o_gla, o_mla, p, gate_col, x2, mod3, ffn_gain, w_gla, w_mla, w_out, seq):
    t, d = x2.shape
    tm = _tile(seq, MIXOUT_ROW_TILE)
    assert gate_col % d == 0
    row = lambda i: (i, 0)
    resident = lambda w: pl.BlockSpec(w.shape, lambda i: (0, 0), pipeline_mode=pl.Buffered(1))
    return pl.pallas_call(
        _mixout_kernel,
        grid=(t // tm,),
        in_specs=[pl.BlockSpec((tm, o_gla.shape[1]), row),
                  pl.BlockSpec((tm, o_mla.shape[1]), row),
                  pl.BlockSpec((tm, d), lambda i: (i, gate_col // d)),
                  pl.BlockSpec((tm, d), lambda i: (i, gate_col // d + 1)),
                  pl.BlockSpec((tm, d), row),
                  pl.BlockSpec((1, N_MOD, d), lambda i: ((i * tm) // seq, 0, 0)),
                  pl.BlockSpec((1, d), lambda i: (0, 0)),
                  resident(w_gla), resident(w_mla), resident(w_out)],
        out_specs=[pl.BlockSpec((tm, d), row), pl.BlockSpec((tm, d), row)],
        out_shape=[jax.ShapeDtypeStruct((t, d), F32), jax.ShapeDtypeStruct((t, d), BF16)],
        compiler_params=_params("parallel"),
        name="mixout",
    )(o_gla, o_mla, p, p, x2, mod3, ffn_gain.reshape(1, d), w_gla, w_mla, w_out)


def _ffn_kernel(x_ref, h_ref, mod_ref, wg_ref, wu_ref, wd_ref, fg_ref, o_ref, acc_ref, *, final_norm):
    f = pl.program_id(1)

    @pl.when(f == 0)
    def _():
        acc_ref[...] = jnp.zeros_like(acc_ref)

    h = h_ref[...]
    gate = _dot(h, wg_ref[...])
    up = _dot(h, wu_ref[...])
    act = (gate * jax.nn.sigmoid(gate) * up).astype(BF16)
    acc_ref[...] += _dot(act, wd_ref[...])

    @pl.when(f == pl.num_programs(1) - 1)
    def _():
        y = x_ref[...] + mod_ref[0, MOD_GATE_FFN:MOD_GATE_FFN + 1, :] * acc_ref[...]
        if final_norm:
            y = _rms(y) * fg_ref[...]
        o_ref[...] = y


def _ffn(x1, h, mod3, w_in, w_down, final_g, seq, final_norm):
    t, d = x1.shape
    hidden = w_down.shape[0]
    tm = _tile(seq, FFN_ROW_TILE)
    tf = _tile(hidden, FFN_HIDDEN_BLOCK)
    nf = hidden // tf
    kern = functools.partial(_ffn_kernel, final_norm=final_norm)
    return pl.pallas_call(
        kern,
        grid=(t // tm, nf),
        in_specs=[pl.BlockSpec((tm, d), lambda i, f: (i, 0)),
                  pl.BlockSpec((tm, d), lambda i, f: (i, 0)),
                  pl.BlockSpec((1, N_MOD, d), lambda i, f: ((i * tm) // seq, 0, 0)),
                  pl.BlockSpec((d, tf), lambda i, f: (0, f)),
                  pl.BlockSpec((d, tf), lambda i, f: (0, nf + f)),
                  pl.BlockSpec((tf, d), lambda i, f: (f, 0)),
                  pl.BlockSpec((1, d), lambda i, f: (0, 0))],
        out_specs=pl.BlockSpec((tm, d), lambda i, f: (i, 0)),
        out_shape=jax.ShapeDtypeStruct((t, d), F32),
        scratch_shapes=[pltpu.VMEM((tm, d), F32)],
        compiler_params=_params("parallel", "arbitrary"),
        name="ffn",
    )(x1, h, mod3, w_in, w_in, w_down, final_g.reshape(1, d))


def _pack_w_in_kernel(wt_ref, o_ref, *, moves, kr_row, gk_row, tail_col):
    for dst, src, width in moves:
        o_ref[:, dst:dst + width] = wt_ref[src:src + width, :].T.astype(BF16)
    half = MLA_ROPE // 2
    kr = wt_ref[kr_row:kr_row + MLA_ROPE, :]
    gk = wt_ref[gk_row:gk_row + GLA_GATE_RANK, :]
    pad = jnp.zeros((V7X_LANES - GLA_GATE_RANK, kr.shape[1]), F32)
    o_ref[:, tail_col:] = jnp.concatenate([kr, kr[half:], kr[:half], gk, pad], axis=0).T.astype(BF16)


def _pack_w_in(w, d, dk_all, dv_all, q_rank, kv_rank):
    widths = (dk_all, dk_all, dv_all, dv_all, GLA_GATE_RANK, q_rank, kv_rank, MLA_ROPE, d, d)
    offs = [0]
    for wd in widths:
        offs.append(offs[-1] + wd)
    gla_w = offs[4]
    moves = ((0, 0, gla_w), (gla_w, offs[8], 2 * d), (gla_w + 2 * d, offs[5], q_rank + kv_rank))
    tail_col = gla_w + 2 * d + q_rank + kv_rank
    n_out = tail_col + 2 * V7X_LANES
    assert all(dst % V7X_LANES == 0 for dst, _, _ in moves) and tail_col % V7X_LANES == 0
    rows = w.shape[0]
    tr = _tile(rows, PACK_ROW_TILE)
    kern = functools.partial(_pack_w_in_kernel, moves=moves, kr_row=offs[7], gk_row=offs[4], tail_col=tail_col)
    return pl.pallas_call(
        kern,
        grid=(rows // tr,),
        in_specs=[pl.BlockSpec((w.shape[1], tr), lambda i: (0, i))],
        out_specs=pl.BlockSpec((tr, n_out), lambda i: (i, 0)),
        out_shape=jax.ShapeDtypeStruct((rows, n_out), BF16),
        compiler_params=_params("parallel"),
        name="pack_w_in",
    )(w.T)


def _pack_wuq_t(w):
    r = w.shape[0]
    half = MLA_ROPE // 2
    w3 = w.reshape(r, MLA_HEADS, MLA_NOPE + MLA_ROPE)
    rope = w3[:, :, MLA_NOPE:]
    rot = jnp.concatenate([rope[:, :, half:], rope[:, :, :half]], axis=2)
    return jnp.concatenate([w3[:, :, :MLA_NOPE], rope, rot], axis=2).reshape(r, -1).T.astype(BF16)


def _pack_wukv(w):
    r = w.shape[0]
    w3 = w.reshape(r, MLA_HEADS, MLA_NOPE + MLA_V)
    wk = w3[:, :, :MLA_NOPE].reshape(r, -1).astype(BF16)
    wvt = w3[:, :, MLA_NOPE:].reshape(r, -1).T.astype(BF16)
    return wk, wvt


def kernel(x, c, positions, ada_w, ada_b, norm_mix_g, w_in, gla_gk_w, gla_gk_b, gla_onorm_g, gla_wo, mla_q_norm_g, mla_wuq, mla_kv_norm_g, mla_wukv, mla_wo, w_out, norm_ffn_g, ffn_w_in, ffn_w_down, final_norm_g):
    bsz, seq, d = x.shape
    depth = w_in.shape[0]
    dk_all = gla_gk_w.shape[2]
    dv_all = gla_wo.shape[1]
    dk, dv = dk_all // GLA_HEADS, dv_all // GLA_HEADS
    q_rank, kv_rank = mla_wuq.shape[1], mla_wukv.shape[1]
    lat_col = 2 * dk_all + 2 * dv_all + 2 * d
    lat_width = q_rank + kv_rank + 2 * V7X_LANES
    gate_col = 2 * dk_all + 2 * dv_all
    assert lat_col % lat_width == 0 and (lat_col + lat_width - V7X_LANES) % V7X_LANES == 0

    x2 = x.reshape(bsz * seq, d)
    cos_t, sin_t = _rope_tab(positions)
    for l in range(depth):
        mod3 = _ada(c, ada_w[l], ada_b[l]).reshape(bsz, N_MOD, d)
        w_all = _pack_w_in(w_in[l], d, dk_all, dv_all, q_rank, kv_rank)
        p = _inproj(x2, mod3, norm_mix_g[l], w_all, seq)
        wk, wvt = _pack_wukv(mla_wukv[l])
        qt, k, vt = _mlaproj(p, lat_col // lat_width, lat_width, cos_t, sin_t, mla_q_norm_g[l], mla_kv_norm_g[l],
                             _pack_wuq_t(mla_wuq[l]), wk, wvt)
        gkw = jnp.zeros((V7X_LANES, dk_all), BF16).at[:GLA_GATE_RANK].set(gla_gk_w[l].astype(BF16))
        o_gla = _gla(p, (lat_col + lat_width - V7X_LANES) // V7X_LANES, gkw, gla_gk_b[l], gla_onorm_g[l],
                     bsz, seq, dk, dv)
        o_mla = _flash(qt, k, vt, bsz, seq)
        x1, h_ffn = _mixout(o_gla, o_mla, p, gate_col, x2, mod3, norm_ffn_g[l], gla_wo[l].astype(BF16),
                            mla_wo[l].astype(BF16), w_out[l].astype(BF16), seq)
        x2 = _ffn(x1, h_ffn, mod3, ffn_w_in[l].astype(BF16), ffn_w_down[l].astype(BF16), final_norm_g,
                  seq, final_norm=(l == depth - 1))
    return x2.reshape(bsz, seq, d)
```

```python
import functools

import jax
import jax.numpy as jnp
from jax import lax
from jax.experimental import pallas as pl
from jax.experimental.pallas import tpu as pltpu

F32 = jnp.float32
BF16 = jnp.bfloat16

GLA_HEADS = 4
GLA_GATE_RANK = 16
GLA_GATE_NORMALIZER = 16.0
GLA_CHUNK = 128
GLA_CHUNKS_PER_STEP = 4
MLA_HEADS = 16
MLA_NOPE = 128
MLA_ROPE = 64
MLA_V = 128
MLA_QW = MLA_NOPE + 2 * MLA_ROPE
MLA_QBLOCK = 1024
MLA_STRIP = 512
MLA_CHAIN = 512
MLA_VP = MLA_V + 16
MLA_HEAD_GROUP = 2
ROPE_THETA = 10000.0
LOG2_E = 1.4426950408889634
NORM_EPS = 1e-6
N_MOD = 6
MOD_SHIFT_MIX, MOD_SCALE_MIX, MOD_GATE_MIX, MOD_SHIFT_FFN, MOD_SCALE_FFN, MOD_GATE_FFN = range(N_MOD)

V7X_LANES = 128
V7X_VMEM_LIMIT_BYTES = 56 * 1024 * 1024

ADA_COL_BLOCK = 1024
ROPE_TAB_TOKENS = 4096
INPROJ_ROW_TILE, INPROJ_COL_BLOCK = 1024, 1280
MLAPROJ_ROW_TILE = 512
MIXOUT_ROW_TILE = 256
FFN_ROW_TILE, FFN_HIDDEN_BLOCK = 512, 512
PACK_ROW_TILE = 128

MASK_VALUE = -1e30


def _params(*semantics):
    return pltpu.CompilerParams(dimension_semantics=semantics, vmem_limit_bytes=V7X_VMEM_LIMIT_BYTES)


def _dot(a, b):
    return jnp.dot(a, b, preferred_element_type=F32)


def _dot_nt(a, b):
    return lax.dot_general(a, b, (((1,), (1,)), ((), ())), preferred_element_type=F32)


def _dot_tn(a, b):
    return lax.dot_general(a, b, (((0,), (0,)), ((), ())), preferred_element_type=F32)


def _rms(x):
    return x * lax.rsqrt(jnp.mean(x * x, axis=-1, keepdims=True) + NORM_EPS)


def _modnorm(x, gain, scale, shift):
    return _rms(x) * (gain * (1.0 + scale)) + shift


def _tile(n, want):
    t = min(n, want)
    assert n % t == 0, (n, want)
    return t


def _ada_kernel(c_ref, w_ref, b_ref, o_ref):
    c = c_ref[...]
    act = (c * jax.nn.sigmoid(c)).astype(BF16)
    o_ref[...] = _dot(act, w_ref[...].astype(BF16)) + b_ref[...]


def _ada(c, w, b):
    bsz, d = c.shape
    n = w.shape[1]
    tn = _tile(n, ADA_COL_BLOCK)
    return pl.pallas_call(
        _ada_kernel,
        grid=(n // tn,),
        in_specs=[pl.BlockSpec((bsz, d), lambda j: (0, 0)),
                  pl.BlockSpec((d, tn), lambda j: (0, j)),
                  pl.BlockSpec((1, tn), lambda j: (0, j))],
        out_specs=pl.BlockSpec((bsz, tn), lambda j: (0, j)),
        out_shape=jax.ShapeDtypeStruct((bsz, n), F32),
        compiler_params=_params("arbitrary"),
        name="ada",
    )(c, w, b.reshape(1, n))


def _rope_tab_kernel(pos_ref, f_ref, cos_ref, sin_ref):
    ang = pos_ref[...].astype(F32) * f_ref[...]
    cos_ref[...] = jnp.cos(ang)
    sin_ref[...] = jnp.sin(ang)


def _rope_tab(positions):
    t = positions.size
    half = MLA_ROPE // 2
    inv_freq = ROPE_THETA ** (-jnp.arange(0, MLA_ROPE, 2, dtype=F32) / MLA_ROPE)
    tt = _tile(t, ROPE_TAB_TOKENS)
    spec = pl.BlockSpec((half, tt), lambda i: (0, i))
    return pl.pallas_call(
        _rope_tab_kernel,
        grid=(t // tt,),
        in_specs=[pl.BlockSpec((1, tt), lambda i: (0, i)), pl.BlockSpec((half, 1), lambda i: (0, 0))],
        out_specs=[spec, spec],
        out_shape=[jax.ShapeDtypeStruct((half, t), F32)] * 2,
        compiler_params=_params("arbitrary"),
        name="rope_tab",
    )(positions.reshape(1, t), inv_freq.reshape(half, 1))


def _inproj_kernel(x_ref, mod_ref, g_ref, w_ref, o_ref, h_ref):
    @pl.when(pl.program_id(1) == 0)
    def _():
        h = _modnorm(x_ref[...], g_ref[...],
                     mod_ref[0, MOD_SCALE_MIX:MOD_SCALE_MIX + 1, :], mod_ref[0, MOD_SHIFT_MIX:MOD_SHIFT_MIX + 1, :])
        h_ref[...] = h.astype(BF16)

    o_ref[...] = _dot(h_ref[...], w_ref[...]).astype(o_ref.dtype)


def _inproj(x2, mod3, gain, w, seq):
    t, d = x2.shape
    n = w.shape[1]
    tm = _tile(seq, INPROJ_ROW_TILE)
    tn = _tile(n, INPROJ_COL_BLOCK)
    return pl.pallas_call(
        _inproj_kernel,
        grid=(t // tm, n // tn),
        in_specs=[pl.BlockSpec((tm, d), lambda i, j: (i, 0)),
                  pl.BlockSpec((1, N_MOD, d), lambda i, j: ((i * tm) // seq, 0, 0)),
                  pl.BlockSpec((1, d), lambda i, j: (0, 0)),
                  pl.BlockSpec((d, tn), lambda i, j: (0, j))],
        out_specs=pl.BlockSpec((tm, tn), lambda i, j: (i, j)),
        out_shape=jax.ShapeDtypeStruct((t, n), BF16),
        scratch_shapes=[pltpu.VMEM((tm, d), BF16)],
        compiler_params=_params("parallel", "arbitrary"),
        name="inproj",
    )(x2, mod3, gain.reshape(1, d), w)


def _mlaproj_kernel(lat_ref, cos_ref, sin_ref, gq_ref, gkv_ref, wqt_ref, wk_ref, wvt_ref, qt_ref, k_ref, vt_ref, *,
                    q_rank, kv_rank, scale):
    tm = lat_ref.shape[0]
    cos = cos_ref[...]
    sin = sin_ref[...]
    cs = jnp.concatenate([cos, cos], axis=0)
    sn = jnp.concatenate([-sin, sin], axis=0)

    cqn = _rms(lat_ref[:, 0:q_rank].astype(F32)) * gq_ref[...]
    ckvn = _rms(lat_ref[:, q_rank:q_rank + kv_rank].astype(F32)) * gkv_ref[...]
    cqn_t = cqn.T.astype(BF16)
    ckvn_t = ckvn.T.astype(BF16)
    ckvn = ckvn.astype(BF16)

    kr_t = lat_ref[:, q_rank + kv_rank:q_rank + kv_rank + V7X_LANES].astype(F32).T
    k_rope_t = kr_t[:MLA_ROPE] * cs + kr_t[MLA_ROPE:] * sn
    k_rope = jnp.concatenate([k_rope_t, jnp.zeros_like(k_rope_t)], axis=0).T.astype(BF16)

    for h in range(MLA_HEADS):
        qt = _dot(wqt_ref[h * MLA_QW:(h + 1) * MLA_QW, :], cqn_t)
        q_rope = qt[MLA_NOPE:MLA_NOPE + MLA_ROPE] * cs + qt[MLA_NOPE + MLA_ROPE:] * sn
        qt_ref[h * MLA_QW:h * MLA_QW + MLA_NOPE, :] = (qt[:MLA_NOPE] * scale).astype(BF16)
        qt_ref[h * MLA_QW + MLA_NOPE:h * MLA_QW + MLA_NOPE + MLA_ROPE, :] = (q_rope * scale).astype(BF16)
        qt_ref[h * MLA_QW + MLA_NOPE + MLA_ROPE:(h + 1) * MLA_QW, :] = jnp.zeros((MLA_ROPE, tm), BF16)
        vt_ref[h, 0, :MLA_V, :] = _dot(wvt_ref[h * MLA_V:(h + 1) * MLA_V, :], ckvn_t).astype(BF16)
        vt_ref[h, 0, MLA_V:, :] = jnp.ones((MLA_VP - MLA_V, tm), BF16)
    for g in range(MLA_HEADS // 2):
        kk = _dot(ckvn, wk_ref[:, 2 * g * MLA_NOPE:2 * (g + 1) * MLA_NOPE]).astype(BF16)
        for e in range(2):
            grp, col = divmod((2 * g + e) * MLA_QW, MLA_HEAD_GROUP * MLA_QW)
            k_ref[grp, :, col:col + MLA_NOPE] = kk[:, e * MLA_NOPE:(e + 1) * MLA_NOPE]
            k_ref[grp, :, col + MLA_NOPE:col + MLA_QW] = k_rope


def _mlaproj(p, lat_block, lat_width, cos_t, sin_t, gq, gkv, wqt, wk, wvt):
    t = p.shape[0]
    q_rank, kv_rank = gq.shape[0], gkv.shape[0]
    tm = _tile(MLA_STRIP, MLAPROJ_ROW_TILE)
    per_strip = MLA_STRIP // tm
    half = MLA_ROPE // 2
    scale = (MLA_NOPE + MLA_ROPE) ** -0.5 * LOG2_E
    kern = functools.partial(_mlaproj_kernel, q_rank=q_rank, kv_rank=kv_rank, scale=scale)
    return pl.pallas_call(
        kern,
        grid=(t // tm,),
        in_specs=[pl.BlockSpec((tm, lat_width), lambda i: (i, lat_block)),
                  pl.BlockSpec((half, tm), lambda i: (0, i)),
                  pl.BlockSpec((half, tm), lambda i: (0, i)),
                  pl.BlockSpec((1, q_rank), lambda i: (0, 0)),
                  pl.BlockSpec((1, kv_rank), lambda i: (0, 0)),
                  pl.BlockSpec(wqt.shape, lambda i: (0, 0)),
                  pl.BlockSpec(wk.shape, lambda i: (0, 0)),
                  pl.BlockSpec(wvt.shape, lambda i: (0, 0))],
        out_specs=[pl.BlockSpec((MLA_HEADS * MLA_QW, tm), lambda i: (0, i)),
                   pl.BlockSpec((MLA_HEADS // MLA_HEAD_GROUP, tm, MLA_HEAD_GROUP * MLA_QW), lambda i: (0, i, 0)),
                   pl.BlockSpec((MLA_HEADS, 1, MLA_VP, tm), lambda i: (0, i // per_strip, 0, i % per_strip))],
        out_shape=[jax.ShapeDtypeStruct((MLA_HEADS * MLA_QW, t), BF16),
                   jax.ShapeDtypeStruct((MLA_HEADS // MLA_HEAD_GROUP, t, MLA_HEAD_GROUP * MLA_QW), BF16),
                   jax.ShapeDtypeStruct((MLA_HEADS, t // MLA_STRIP, MLA_VP, MLA_STRIP), BF16)],
        compiler_params=_params("parallel"),
        name="mlaproj",
    )(p, cos_t, sin_t, gq.reshape(1, q_rank), gkv.reshape(1, kv_rank), wqt, wk, wvt)


def _gla_kernel(q_ref, k_ref, v_ref, g_ref, gk_ref, gkw_ref, gkb_ref, on_ref, scan_ref, o_ref, st_ref, *, dk, dv):
    c = GLA_CHUNK
    nsub = q_ref.shape[0] // c

    @pl.when(pl.program_id(1) == 0)
    def _():
        st_ref[...] = jnp.zeros_like(st_ref)

    z = _dot(gk_ref[...], gkw_ref[...]) + gkb_ref[...]
    log_a = jax.nn.log_sigmoid(z) * (LOG2_E / GLA_GATE_NORMALIZER)

    halves = [1 << e for e in range(c.bit_length() - 1)]
    ai = lax.broadcasted_iota(jnp.int32, (c, c), 0)
    aj = lax.broadcasted_iota(jnp.int32, (c, c), 1)
    diff = ai ^ aj
    causal_diff = jnp.where(ai > aj, diff, 0)
    level_mask = {s: (causal_diff >> (s.bit_length() - 1)) == 1 for s in halves}
    diag_mask = ai == aj

    g_hi = log_a.astype(BF16)
    g_lo = (log_a - g_hi.astype(F32)).astype(BF16)
    n_lv = len(halves) * c

    def chunk_local(h, u):
        rows, cols = slice(u * c, (u + 1) * c), slice(h * dk, (h + 1) * dk)
        level_sum = _dot(scan_ref[:n_lv, :], g_hi[rows, cols])
        cum = _dot(scan_ref[n_lv:, :], jnp.concatenate([g_hi[rows, cols], g_lo[rows, cols]], axis=1))
        prefix = cum[:, :dk] + cum[:, dk:]
        total = prefix[c - 1:c, :]
        q16 = q_ref[rows, cols] * (dk ** -0.5)
        k16 = k_ref[rows, cols]
        att = jnp.where(diag_mask, _dot_nt(q16, k16), 0.0)
        for li, s in enumerate(halves):
            decay = jnp.exp2(level_sum[li * c:(li + 1) * c, :].astype(BF16))
            att = jnp.where(level_mask[s], _dot_nt(q16 * decay, k16 * decay), att)
        q_dec = (q16.astype(F32) * jnp.exp2(prefix)).astype(BF16)
        k_dec = (k16.astype(F32) * jnp.exp2(total - prefix)).astype(BF16)
        return q_dec, k_dec, att.astype(BF16), jnp.exp2(total)

    for h in range(GLA_HEADS):
        local = [chunk_local(h, u) for u in range(nsub)]
        for u, (q_dec, k_dec, att, state_decay) in enumerate(local):
            rows = slice(u * c, (u + 1) * c)
            vh = v_ref[rows, h * dv:(h + 1) * dv]
            st = st_ref[h]
            o = _dot_nt(q_dec, st.astype(BF16)) + _dot(att, vh)
            st_ref[h] = st * state_decay + _dot_tn(vh, k_dec)
            gate = g_ref[rows, h * dv:(h + 1) * dv].astype(F32)
            o_ref[rows, h * dv:(h + 1) * dv] = (_rms(o) * on_ref[...] * (gate * jax.nn.sigmoid(gate))).astype(o_ref.dtype)


def _gla_scan_matrix(c):
    p = jnp.arange(c)[:, None]
    t = jnp.arange(c)[None, :]
    mats = []
    s = 1
    while s < c:
        start = (p // s) * s
        odd = (p // s) % 2 == 1
        mats.append(jnp.where(odd, (t >= start) & (t <= p), (t > p) & (t < start + s)))
        s *= 2
    mats.append(t <= p)
    return jnp.concatenate(mats, axis=0).astype(BF16)


def _gla(p, gk_block, gkw, gkb, onorm_g, bsz, seq, dk, dv):
    t = p.shape[0]
    c = _tile(seq, GLA_CHUNK * GLA_CHUNKS_PER_STEP)
    nc = seq // c
    hk, hv = GLA_HEADS * dk, GLA_HEADS * dv
    scan = _gla_scan_matrix(GLA_CHUNK)
    kern = functools.partial(_gla_kernel, dk=dk, dv=dv)
    row = lambda b, i: b * nc + i
    return pl.pallas_call(
        kern,
        grid=(bsz, nc),
        in_specs=[pl.BlockSpec((c, hk), lambda b, i: (row(b, i), 0)),
                  pl.BlockSpec((c, hk), lambda b, i: (row(b, i), 1)),
                  pl.BlockSpec((c, hv), lambda b, i: (row(b, i), (2 * hk) // hv)),
                  pl.BlockSpec((c, hv), lambda b, i: (row(b, i), (2 * hk) // hv + 1)),
                  pl.BlockSpec((c, V7X_LANES), lambda b, i: (row(b, i), gk_block)),
                  pl.BlockSpec(gkw.shape, lambda b, i: (0, 0)),
                  pl.BlockSpec((1, hk), lambda b, i: (0, 0)),
                  pl.BlockSpec((1, dv), lambda b, i: (0, 0)),
                  pl.BlockSpec(scan.shape, lambda b, i: (0, 0))],
        out_specs=pl.BlockSpec((c, hv), lambda b, i: (row(b, i), 0)),
        out_shape=jax.ShapeDtypeStruct((t, hv), BF16),
        scratch_shapes=[pltpu.VMEM((GLA_HEADS, dv, dk), F32)],
        compiler_params=_params("parallel", "arbitrary"),
        name="gla",
    )(p, p, p, p, p, gkw, gkb.reshape(1, hk), onorm_g.reshape(1, dv), scan)


def _flash_kernel(qt_ref, k_ref, vt_ref, o_ref, acc_ref, sa_ref, sb_ref, pa_ref, pb_ref):
    strip, cw = MLA_STRIP, MLA_CHAIN
    hp, nch = acc_ref.shape[0], acc_ref.shape[1]
    chains = [(h, a) for h in range(hp) for a in range(nch)]
    per_block = MLA_QBLOCK // strip
    assert per_block == 2 and nch == 2 and strip == cw
    i = pl.program_id(2)
    acc_ref[...] = jnp.zeros_like(acc_ref)
    pb_ref[...] = jnp.zeros_like(pb_ref)

    def scores(h, a, strip_idx):
        start = pl.multiple_of(strip_idx * strip, strip)
        return _dot(k_ref[0, pl.ds(start, strip), h * MLA_QW:(h + 1) * MLA_QW],
                    qt_ref[h * MLA_QW:(h + 1) * MLA_QW, a * cw:(a + 1) * cw])

    def col_max(s):
        return jnp.max(s, axis=0, keepdims=True)

    def softmax(s, s_max, m):
        m_new = jnp.maximum(m, s_max)
        return jnp.exp2((s - m_new).astype(BF16)), jnp.exp2(m - m_new), m_new

    def accumulate(h, a, strip_idx, p, alpha):
        acc_ref[h, a] = alpha * acc_ref[h, a] + _dot(vt_ref[h, strip_idx], p)

    def half_step(s_cur, s_nxt, p_cur, p_prev, t, carry):
        out = []
        for c, (h, a) in enumerate(chains):
            m, cur_max, alpha_prev = carry[c]
            s = scores(h, a, t + 1)
            s_nxt[h, a] = s
            accumulate(h, a, jnp.maximum(t - 1, 0), p_prev[h, a], alpha_prev)
            p, alpha, m = softmax(s_cur[h, a], cur_max, m)
            p_cur[h, a] = p
            out.append((m, col_max(s), alpha))
        return tuple(out)

    def body(jj, carry):
        carry = half_step(sa_ref, sb_ref, pa_ref, pb_ref, 2 * jj, carry)
        return half_step(sb_ref, sa_ref, pb_ref, pa_ref, 2 * jj + 1, carry)

    init = []
    for h, a in chains:
        s = scores(h, a, 0)
        sa_ref[h, a] = s
        init.append((jnp.full((1, cw), MASK_VALUE, F32), col_max(s), jnp.ones((1, cw), F32)))
    carry = lax.fori_loop(0, i, body, tuple(init))
    first = per_block * i
    causal = lax.broadcasted_iota(jnp.int32, (strip, cw), 0) <= lax.broadcasted_iota(jnp.int32, (strip, cw), 1)
    last = [scores(h, 1, first + 1) for h in range(hp)]

    def update(h, a, strip_idx, s, s_max, m):
        p, alpha, m = softmax(s, s_max, m)
        accumulate(h, a, strip_idx, p, alpha)
        return m

    for c, (h, a) in enumerate(chains):
        m, s_max, alpha_prev = carry[c]
        accumulate(h, a, jnp.maximum(first - 1, 0), pb_ref[h, a], alpha_prev)
        if a == 0:
            s = jnp.where(causal, sa_ref[h, a], MASK_VALUE)
            update(h, a, first, s, col_max(s), m)
        else:
            m = update(h, a, first, sa_ref[h, a], s_max, m)
            s = jnp.where(causal, last[h], MASK_VALUE)
            update(h, a, first + 1, s, col_max(s), m)
        o_t = acc_ref[h, a, :MLA_V, :] / acc_ref[h, a, MLA_V:MLA_V + 1, :]
        o_ref[a * cw:(a + 1) * cw, h * MLA_V:(h + 1) * MLA_V] = o_t.T.astype(o_ref.dtype)


def _flash(qt, k, vt, bsz, seq):
    t = k.shape[1]
    assert seq % MLA_QBLOCK == 0
    nq = seq // MLA_QBLOCK
    hp = MLA_HEAD_GROUP
    nch = MLA_QBLOCK // MLA_CHAIN
    return pl.pallas_call(
        _flash_kernel,
        grid=(bsz, MLA_HEADS // hp, nq),
        in_specs=[pl.BlockSpec((hp * MLA_QW, MLA_QBLOCK), lambda b, g, i: (g, b * nq + i)),
                  pl.BlockSpec((1, seq, hp * MLA_QW), lambda b, g, i: (g, b, 0)),
                  pl.BlockSpec((hp, seq // MLA_STRIP, MLA_VP, MLA_STRIP), lambda b, g, i: (g, b, 0, 0))],
        out_specs=pl.BlockSpec((MLA_QBLOCK, hp * MLA_V), lambda b, g, i: (b * nq + i, g)),
        out_shape=jax.ShapeDtypeStruct((t, MLA_HEADS * MLA_V), BF16),
        scratch_shapes=[pltpu.VMEM((hp, nch, MLA_VP, MLA_CHAIN), F32),
                        pltpu.VMEM((hp, nch, MLA_STRIP, MLA_CHAIN), F32),
                        pltpu.VMEM((hp, nch, MLA_STRIP, MLA_CHAIN), F32),
                        pltpu.VMEM((hp, nch, MLA_STRIP, MLA_CHAIN), BF16),
                        pltpu.VMEM((hp, nch, MLA_STRIP, MLA_CHAIN), BF16)],
        compiler_params=_params("parallel", "parallel", "arbitrary"),
        name="flash",
    )(qt, k, vt)


def _mixout_kernel(og_ref, om_ref, ga_ref, gb_ref, x_ref, mod_ref, gf_ref, wg_ref, wm_ref, wo_ref, o_ref, h_ref):
    yg = _dot(og_ref[...], wg_ref[...])
    ym = _dot(om_ref[...], wm_ref[...])
    ga = jax.nn.sigmoid(ga_ref[...].astype(F32))
    gb = jax.nn.sigmoid(gb_ref[...].astype(F32))
    merged = (ga * yg + gb * ym).astype(BF16)
    x1 = x_ref[...] + mod_ref[0, MOD_GATE_MIX:MOD_GATE_MIX + 1, :] * _dot(merged, wo_ref[...])
    o_ref[...] = x1
    h_ref[...] = _modnorm(x1, gf_ref[...], mod_ref[0, MOD_SCALE_FFN:MOD_SCALE_FFN + 1, :],
                          mod_ref[0, MOD_SHIFT_FFN:MOD_SHIFT_FFN + 1, :]).astype(BF16)


def _mixout(o_gla, o_mla, p, gate_col, x2, mod3, ffn_gain, w_gla, w_mla, w_out, seq):
    t, d = x2.shape
    tm = _tile(seq, MIXOUT_ROW_TILE)
    assert gate_col % d == 0
    row = lambda i: (i, 0)
    resident = lambda w: pl.BlockSpec(w.shape, lambda i: (0, 0), pipeline_mode=pl.Buffered(1))
    return pl.pallas_call(
        _mixout_kernel,
        grid=(t // tm,),
        in_specs=[pl.BlockSpec((tm, o_gla.shape[1]), row),
                  pl.BlockSpec((tm, o_mla.shape[1]), row),
                  pl.BlockSpec((tm, d), lambda i: (i, gate_col // d)),
                  pl.BlockSpec((tm, d), lambda i: (i, gate_col // d + 1)),
                  pl.BlockSpec((tm, d), row),
                  pl.BlockSpec((1, N_MOD, d), lambda i: ((i * tm) // seq, 0, 0)),
                  pl.BlockSpec((1, d), lambda i: (0, 0)),
                  resident(w_gla), resident(w_mla), resident(w_out)],
        out_specs=[pl.BlockSpec((tm, d), row), pl.BlockSpec((tm, d), row)],
        out_shape=[jax.ShapeDtypeStruct((t, d), F32), jax.ShapeDtypeStruct((t, d), BF16)],
        compiler_params=_params("parallel"),
        name="mixout",
    )(o_gla, o_mla, p, p, x2, mod3, ffn_gain.reshape(1, d), w_gla, w_mla, w_out)


def _ffn_kernel(x_ref, h_ref, mod_ref, wg_ref, wu_ref, wd_ref, fg_ref, o_ref, acc_ref, *, final_norm):
    f = pl.program_id(1)

    @pl.when(f == 0)
    def _():
        acc_ref[...] = jnp.zeros_like(acc_ref)

    h = h_ref[...]
    gate = _dot(h, wg_ref[...])
    up = _dot(h, wu_ref[...])
    act = (gate * jax.nn.sigmoid(gate) * up).astype(BF16)
    acc_ref[...] += _dot(act, wd_ref[...])

    @pl.when(f == pl.num_programs(1) - 1)
    def _():
        y = x_ref[...] + mod_ref[0, MOD_GATE_FFN:MOD_GATE_FFN + 1, :] * acc_ref[...]
        if final_norm:
            y = _rms(y) * fg_ref[...]
        o_ref[...] = y


def _ffn(x1, h, mod3, w_in, w_down, final_g, seq, final_norm):
    t, d = x1.shape
    hidden = w_down.shape[0]
    tm = _tile(seq, FFN_ROW_TILE)
    tf = _tile(hidden, FFN_HIDDEN_BLOCK)
    nf = hidden // tf
    kern = functools.partial(_ffn_kernel, final_norm=final_norm)
    return pl.pallas_call(
        kern,
        grid=(t // tm, nf),
        in_specs=[pl.BlockSpec((tm, d), lambda i, f: (i, 0)),
                  pl.BlockSpec((tm, d), lambda i, f: (i, 0)),
                  pl.BlockSpec((1, N_MOD, d), lambda i, f: ((i * tm) // seq, 0, 0)),
                  pl.BlockSpec((d, tf), lambda i, f: (0, f)),
                  pl.BlockSpec((d, tf), lambda i, f: (0, nf + f)),
                  pl.BlockSpec((tf, d), lambda i, f: (f, 0)),
                  pl.BlockSpec((1, d), lambda i, f: (0, 0))],
        out_specs=pl.BlockSpec((tm, d), lambda i, f: (i, 0)),
        out_shape=jax.ShapeDtypeStruct((t, d), F32),
        scratch_shapes=[pltpu.VMEM((tm, d), F32)],
        compiler_params=_params("parallel", "arbitrary"),
        name="ffn",
    )(x1, h, mod3, w_in, w_in, w_down, final_g.reshape(1, d))


def _pack_w_in_kernel(wt_ref, o_ref, *, moves, kr_row, gk_row, tail_col):
    for dst, src, width in moves:
        o_ref[:, dst:dst + width] = wt_ref[src:src + width, :].T.astype(BF16)
    half = MLA_ROPE // 2
    kr = wt_ref[kr_row:kr_row + MLA_ROPE, :]
    gk = wt_ref[gk_row:gk_row + GLA_GATE_RANK, :]
    pad = jnp.zeros((V7X_LANES - GLA_GATE_RANK, kr.shape[1]), F32)
    o_ref[:, tail_col:] = jnp.concatenate([kr, kr[half:], kr[:half], gk, pad], axis=0).T.astype(BF16)


def _pack_w_in(w, d, dk_all, dv_all, q_rank, kv_rank):
    widths = (dk_all, dk_all, dv_all, dv_all, GLA_GATE_RANK, q_rank, kv_rank, MLA_ROPE, d, d)
    offs = [0]
    for wd in widths:
        offs.append(offs[-1] + wd)
    gla_w = offs[4]
    moves = ((0, 0, gla_w), (gla_w, offs[8], 2 * d), (gla_w + 2 * d, offs[5], q_rank + kv_rank))
    tail_col = gla_w + 2 * d + q_rank + kv_rank
    n_out = tail_col + 2 * V7X_LANES
    assert all(dst % V7X_LANES == 0 for dst, _, _ in moves) and tail_col % V7X_LANES == 0
    rows = w.shape[0]
    tr = _tile(rows, PACK_ROW_TILE)
    kern = functools.partial(_pack_w_in_kernel, moves=moves, kr_row=offs[7], gk_row=offs[4], tail_col=tail_col)
    return pl.pallas_call(
        kern,
        grid=(rows // tr,),
        in_specs=[pl.BlockSpec((w.shape[1], tr), lambda i: (0, i))],
        out_specs=pl.BlockSpec((tr, n_out), lambda i: (i, 0)),
        out_shape=jax.ShapeDtypeStruct((rows, n_out), BF16),
        compiler_params=_params("parallel"),
        name="pack_w_in",
    )(w.T)


def _pack_wuq_t(w):
    r = w.shape[0]
    half = MLA_ROPE // 2
    w3 = w.reshape(r, MLA_HEADS, MLA_NOPE + MLA_ROPE)
    rope = w3[:, :, MLA_NOPE:]
    rot = jnp.concatenate([rope[:, :, half:], rope[:, :, :half]], axis=2)
    return jnp.concatenate([w3[:, :, :MLA_NOPE], rope, rot], axis=2).reshape(r, -1).T.astype(BF16)


def _pack_wukv(w):
    r = w.shape[0]
    w3 = w.reshape(r, MLA_HEADS, MLA_NOPE + MLA_V)
    wk = w3[:, :, :MLA_NOPE].reshape(r, -1).astype(BF16)
    wvt = w3[:, :, MLA_NOPE:].reshape(r, -1).T.astype(BF16)
    return wk, wvt


def kernel(x, c, positions, ada_w, ada_b, norm_mix_g, w_in, gla_gk_w, gla_gk_b, gla_onorm_g, gla_wo, mla_q_norm_g, mla_wuq, mla_kv_norm_g, mla_wukv, mla_wo, w_out, norm_ffn_g, ffn_w_in, ffn_w_down, final_norm_g):
    bsz, seq, d = x.shape
    depth = w_in.shape[0]
    dk_all = gla_gk_w.shape[2]
    dv_all = gla_wo.shape[1]
    dk, dv = dk_all // GLA_HEADS, dv_all // GLA_HEADS
    q_rank, kv_rank = mla_wuq.shape[1], mla_wukv.shape[1]
    lat_col = 2 * dk_all + 2 * dv_all + 2 * d
    lat_width = q_rank + kv_rank + 2 * V7X_LANES
    gate_col = 2 * dk_all + 2 * dv_all
    assert lat_col % lat_width == 0 and (lat_col + lat_width - V7X_LANES) % V7X_LANES == 0

    x2 = x.reshape(bsz * seq, d)
    cos_t, sin_t = _rope_tab(positions)
    for l in range(depth):
        mod3 = _ada(c, ada_w[l], ada_b[l]).reshape(bsz, N_MOD, d)
        w_all = _pack_w_in(w_in[l], d, dk_all, dv_all, q_rank, kv_rank)
        p = _inproj(x2, mod3, norm_mix_g[l], w_all, seq)
        wk, wvt = _pack_wukv(mla_wukv[l])
        qt, k, vt = _mlaproj(p, lat_col // lat_width, lat_width, cos_t, sin_t, mla_q_norm_g[l], mla_kv_norm_g[l],
                             _pack_wuq_t(mla_wuq[l]), wk, wvt)
        gkw = jnp.zeros((V7X_LANES, dk_all), BF16).at[:GLA_GATE_RANK].set(gla_gk_w[l].astype(BF16))
        o_gla = _gla(p, (lat_col + lat_width - V7X_LANES) // V7X_LANES, gkw, gla_gk_b[l], gla_onorm_g[l],
                     bsz, seq, dk, dv)
        o_mla = _flash(qt, k, vt, bsz, seq)
        x1, h_ffn = _mixout(o_gla, o_mla, p, gate_col, x2, mod3, norm_ffn_g[l], gla_wo[l].astype(BF16),
                            mla_wo[l].astype(BF16), w_out[l].astype(BF16), seq)
        x2 = _ffn(x1, h_ffn, mod3, ffn_w_in[l].astype(BF16), ffn_w_down[l].astype(BF16), final_norm_g,
                  seq, final_norm=(l == depth - 1))
    return x2.reshape(bsz, seq, d)
```

```python
import functools

import jax
import jax.numpy as jnp
from jax import lax
from jax.experimental import pallas as pl
from jax.experimental.pallas import tpu as pltpu

F32 = jnp.float32
BF16 = jnp.bfloat16

GLA_HEADS = 4
GLA_GATE_RANK = 16
GLA_GATE_NORMALIZER = 16.0
GLA_CHUNK = 128
GLA_CHUNKS_PER_STEP = 4
MLA_HEADS = 16
MLA_NOPE = 128
MLA_ROPE = 64
MLA_V = 128
MLA_QW = MLA_NOPE + 2 * MLA_ROPE
MLA_QBLOCK = 1024
MLA_STRIP = 512
MLA_CHAIN = 512
MLA_VP = MLA_V + 16
MLA_HEAD_GROUP = 2
ROPE_THETA = 10000.0
LOG2_E = 1.4426950408889634
NORM_EPS = 1e-6
N_MOD = 6
MOD_SHIFT_MIX, MOD_SCALE_MIX, MOD_GATE_MIX, MOD_SHIFT_FFN, MOD_SCALE_FFN, MOD_GATE_FFN = range(N_MOD)

V7X_LANES = 128
V7X_VMEM_LIMIT_BYTES = 56 * 1024 * 1024

ADA_COL_BLOCK = 1024
ROPE_TAB_TOKENS = 4096
INPROJ_ROW_TILE, INPROJ_COL_BLOCK = 1024, 1280
MLAPROJ_ROW_TILE = 512
MIXOUT_ROW_TILE = 256
FFN_ROW_TILE, FFN_HIDDEN_BLOCK = 512, 512
PACK_ROW_TILE = 128

MASK_VALUE = -1e30


def _params(*semantics):
    return pltpu.CompilerParams(dimension_semantics=semantics, vmem_limit_bytes=V7X_VMEM_LIMIT_BYTES)


def _dot(a, b):
    return jnp.dot(a, b, preferred_element_type=F32)


def _dot_nt(a, b):
    return lax.dot_general(a, b, (((1,), (1,)), ((), ())), preferred_element_type=F32)


def _dot_tn(a, b):
    return lax.dot_general(a, b, (((0,), (0,)), ((), ())), preferred_element_type=F32)


def _rms(x):
    return x * lax.rsqrt(jnp.mean(x * x, axis=-1, keepdims=True) + NORM_EPS)


def _modnorm(x, gain, scale, shift):
    return _rms(x) * (gain * (1.0 + scale)) + shift


def _tile(n, want):
    t = min(n, want)
    assert n % t == 0, (n, want)
    return t


def _ada_kernel(c_ref, w_ref, b_ref, o_ref):
    c = c_ref[...]
    act = (c * jax.nn.sigmoid(c)).astype(BF16)
    o_ref[...] = _dot(act, w_ref[...].astype(BF16)) + b_ref[...]


def _ada(c, w, b):
    bsz, d = c.shape
    n = w.shape[1]
    tn = _tile(n, ADA_COL_BLOCK)
    return pl.pallas_call(
        _ada_kernel,
        grid=(n // tn,),
        in_specs=[pl.BlockSpec((bsz, d), lambda j: (0, 0)),
                  pl.BlockSpec((d, tn), lambda j: (0, j)),
                  pl.BlockSpec((1, tn), lambda j: (0, j))],
        out_specs=pl.BlockSpec((bsz, tn), lambda j: (0, j)),
        out_shape=jax.ShapeDtypeStruct((bsz, n), F32),
        compiler_params=_params("arbitrary"),
        name="ada",
    )(c, w, b.reshape(1, n))


def _rope_tab_kernel(pos_ref, f_ref, cos_ref, sin_ref):
    ang = pos_ref[...].astype(F32) * f_ref[...]
    cos_ref[...] = jnp.cos(ang)
    sin_ref[...] = jnp.sin(ang)


def _rope_tab(positions):
    t = positions.size
    half = MLA_ROPE // 2
    inv_freq = ROPE_THETA ** (-jnp.arange(0, MLA_ROPE, 2, dtype=F32) / MLA_ROPE)
    tt = _tile(t, ROPE_TAB_TOKENS)
    spec = pl.BlockSpec((half, tt), lambda i: (0, i))
    return pl.pallas_call(
        _rope_tab_kernel,
        grid=(t // tt,),
        in_specs=[pl.BlockSpec((1, tt), lambda i: (0, i)), pl.BlockSpec((half, 1), lambda i: (0, 0))],
        out_specs=[spec, spec],
        out_shape=[jax.ShapeDtypeStruct((half, t), F32)] * 2,
        compiler_params=_params("arbitrary"),
        name="rope_tab",
    )(positions.reshape(1, t), inv_freq.reshape(half, 1))


def _inproj_kernel(x_ref, mod_ref, g_ref, w_ref, o_ref, h_ref):
    @pl.when(pl.program_id(1) == 0)
    def _():
        h = _modnorm(x_ref[...], g_ref[...],
                     mod_ref[0, MOD_SCALE_MIX:MOD_SCALE_MIX + 1, :], mod_ref[0, MOD_SHIFT_MIX:MOD_SHIFT_MIX + 1, :])
        h_ref[...] = h.astype(BF16)

    o_ref[...] = _dot(h_ref[...], w_ref[...]).astype(o_ref.dtype)


def _inproj(x2, mod3, gain, w, seq):
    t, d = x2.shape
    n = w.shape[1]
    tm = _tile(seq, INPROJ_ROW_TILE)
    tn = _tile(n, INPROJ_COL_BLOCK)
    return pl.pallas_call(
        _inproj_kernel,
        grid=(t // tm, n // tn),
        in_specs=[pl.BlockSpec((tm, d), lambda i, j: (i, 0)),
                  pl.BlockSpec((1, N_MOD, d), lambda i, j: ((i * tm) // seq, 0, 0)),
                  pl.BlockSpec((1, d), lambda i, j: (0, 0)),
                  pl.BlockSpec((d, tn), lambda i, j: (0, j))],
        out_specs=pl.BlockSpec((tm, tn), lambda i, j: (i, j)),
        out_shape=jax.ShapeDtypeStruct((t, n), BF16),
        scratch_shapes=[pltpu.VMEM((tm, d), BF16)],
        compiler_params=_params("parallel", "arbitrary"),
        name="inproj",
    )(x2, mod3, gain.reshape(1, d), w)


def _mlaproj_kernel(lat_ref, cos_ref, sin_ref, gq_ref, gkv_ref, wqt_ref, wk_ref, wvt_ref, qt_ref, k_ref, vt_ref, *,
                    q_rank, kv_rank, scale):
    tm = lat_ref.shape[0]
    cos = cos_ref[...]
    sin = sin_ref[...]
    cs = jnp.concatenate([cos, cos], axis=0)
    sn = jnp.concatenate([-sin, sin], axis=0)

    cqn = _rms(lat_ref[:, 0:q_rank].astype(F32)) * gq_ref[...]
    ckvn = _rms(lat_ref[:, q_rank:q_rank + kv_rank].astype(F32)) * gkv_ref[...]
    cqn_t = cqn.T.astype(BF16)
    ckvn_t = ckvn.T.astype(BF16)
    ckvn = ckvn.astype(BF16)

    kr_t = lat_ref[:, q_rank + kv_rank:q_rank + kv_rank + V7X_LANES].astype(F32).T
    k_rope_t = kr_t[:MLA_ROPE] * cs + kr_t[MLA_ROPE:] * sn
    k_rope = jnp.concatenate([k_rope_t, jnp.zeros_like(k_rope_t)], axis=0).T.astype(BF16)

    for h in range(MLA_HEADS):
        qt = _dot(wqt_ref[h * MLA_QW:(h + 1) * MLA_QW, :], cqn_t)
        q_rope = qt[MLA_NOPE:MLA_NOPE + MLA_ROPE] * cs + qt[MLA_NOPE + MLA_ROPE:] * sn
        qt_ref[h * MLA_QW:h * MLA_QW + MLA_NOPE, :] = (qt[:MLA_NOPE] * scale).astype(BF16)
        qt_ref[h * MLA_QW + MLA_NOPE:h * MLA_QW + MLA_NOPE + MLA_ROPE, :] = (q_rope * scale).astype(BF16)
        qt_ref[h * MLA_QW + MLA_NOPE + MLA_ROPE:(h + 1) * MLA_QW, :] = jnp.zeros((MLA_ROPE, tm), BF16)
        vt_ref[h, 0, :MLA_V, :] = _dot(wvt_ref[h * MLA_V:(h + 1) * MLA_V, :], ckvn_t).astype(BF16)
        vt_ref[h, 0, MLA_V:, :] = jnp.ones((MLA_VP - MLA_V, tm), BF16)
    for g in range(MLA_HEADS // 2):
        kk = _dot(ckvn, wk_ref[:, 2 * g * MLA_NOPE:2 * (g + 1) * MLA_NOPE]).astype(BF16)
        for e in range(2):
            k_ref[2 * g + e, :, :MLA_NOPE] = kk[:, e * MLA_NOPE:(e + 1) * MLA_NOPE]
            k_ref[2 * g + e, :, MLA_NOPE:] = k_rope


def _mlaproj(p, lat_block, lat_width, cos_t, sin_t, gq, gkv, wqt, wk, wvt):
    t = p.shape[0]
    q_rank, kv_rank = gq.shape[0], gkv.shape[0]
    tm = _tile(MLA_STRIP, MLAPROJ_ROW_TILE)
    per_strip = MLA_STRIP // tm
    half = MLA_ROPE // 2
    scale = (MLA_NOPE + MLA_ROPE) ** -0.5 * LOG2_E
    kern = functools.partial(_mlaproj_kernel, q_rank=q_rank, kv_rank=kv_rank, scale=scale)
    return pl.pallas_call(
        kern,
        grid=(t // tm,),
        in_specs=[pl.BlockSpec((tm, lat_width), lambda i: (i, lat_block)),
                  pl.BlockSpec((half, tm), lambda i: (0, i)),
                  pl.BlockSpec((half, tm), lambda i: (0, i)),
                  pl.BlockSpec((1, q_rank), lambda i: (0, 0)),
                  pl.BlockSpec((1, kv_rank), lambda i: (0, 0)),
                  pl.BlockSpec(wqt.shape, lambda i: (0, 0)),
                  pl.BlockSpec(wk.shape, lambda i: (0, 0)),
                  pl.BlockSpec(wvt.shape, lambda i: (0, 0))],
        out_specs=[pl.BlockSpec((MLA_HEADS * MLA_QW, tm), lambda i: (0, i)),
                   pl.BlockSpec((MLA_HEADS, tm, MLA_QW), lambda i: (0, i, 0)),
                   pl.BlockSpec((MLA_HEADS, 1, MLA_VP, tm), lambda i: (0, i // per_strip, 0, i % per_strip))],
        out_shape=[jax.ShapeDtypeStruct((MLA_HEADS * MLA_QW, t), BF16),
                   jax.ShapeDtypeStruct((MLA_HEADS, t, MLA_QW), BF16),
                   jax.ShapeDtypeStruct((MLA_HEADS, t // MLA_STRIP, MLA_VP, MLA_STRIP), BF16)],
        compiler_params=_params("parallel"),
        name="mlaproj",
    )(p, cos_t, sin_t, gq.reshape(1, q_rank), gkv.reshape(1, kv_rank), wqt, wk, wvt)


def _gla_kernel(q_ref, k_ref, v_ref, g_ref, gk_ref, gkw_ref, gkb_ref, on_ref, scan_ref, o_ref, st_ref, *, dk, dv):
    c = GLA_CHUNK
    nsub = q_ref.shape[0] // c

    @pl.when(pl.program_id(1) == 0)
    def _():
        st_ref[...] = jnp.zeros_like(st_ref)

    z = _dot(gk_ref[...], gkw_ref[...]) + gkb_ref[...]
    log_a = jax.nn.log_sigmoid(z) * (LOG2_E / GLA_GATE_NORMALIZER)

    halves = [1 << e for e in range(c.bit_length() - 1)]
    ai = lax.broadcasted_iota(jnp.int32, (c, c), 0)
    aj = lax.broadcasted_iota(jnp.int32, (c, c), 1)
    diff = ai ^ aj
    causal_diff = jnp.where(ai > aj, diff, 0)
    level_mask = {s: (causal_diff >> (s.bit_length() - 1)) == 1 for s in halves}
    diag_mask = ai == aj

    g_hi = log_a.astype(BF16)
    g_lo = (log_a - g_hi.astype(F32)).astype(BF16)
    n_lv = len(halves) * c

    def chunk_local(h, u):
        rows, cols = slice(u * c, (u + 1) * c), slice(h * dk, (h + 1) * dk)
        level_sum = _dot(scan_ref[:n_lv, :], g_hi[rows, cols])
        cum = _dot(scan_ref[n_lv:, :], jnp.concatenate([g_hi[rows, cols], g_lo[rows, cols]], axis=1))
        prefix = cum[:, :dk] + cum[:, dk:]
        total = prefix[c - 1:c, :]
        q16 = q_ref[rows, cols] * (dk ** -0.5)
        k16 = k_ref[rows, cols]
        att = jnp.where(diag_mask, _dot_nt(q16, k16), 0.0)
        for li, s in enumerate(halves):
            decay = jnp.exp2(level_sum[li * c:(li + 1) * c, :].astype(BF16))
            att = jnp.where(level_mask[s], _dot_nt(q16 * decay, k16 * decay), att)
        q_dec = (q16.astype(F32) * jnp.exp2(prefix)).astype(BF16)
        k_dec = (k16.astype(F32) * jnp.exp2(total - prefix)).astype(BF16)
        return q_dec, k_dec, att.astype(BF16), jnp.exp2(total)

    for h in range(GLA_HEADS):
        local = [chunk_local(h, u) for u in range(nsub)]
        for u, (q_dec, k_dec, att, state_decay) in enumerate(local):
            rows = slice(u * c, (u + 1) * c)
            vh = v_ref[rows, h * dv:(h + 1) * dv]
            st = st_ref[h]
            o = _dot_nt(q_dec, st.astype(BF16)) + _dot(att, vh)
            st_ref[h] = st * state_decay + _dot_tn(vh, k_dec)
            gate = g_ref[rows, h * dv:(h + 1) * dv].astype(F32)
            o_ref[rows, h * dv:(h + 1) * dv] = (_rms(o) * on_ref[...] * (gate * jax.nn.sigmoid(gate))).astype(o_ref.dtype)


def _gla_scan_matrix(c):
    p = jnp.arange(c)[:, None]
    t = jnp.arange(c)[None, :]
    mats = []
    s = 1
    while s < c:
        start = (p // s) * s
        odd = (p // s) % 2 == 1
        mats.append(jnp.where(odd, (t >= start) & (t <= p), (t > p) & (t < start + s)))
        s *= 2
    mats.append(t <= p)
    return jnp.concatenate(mats, axis=0).astype(BF16)


def _gla(p, gk_block, gkw, gkb, onorm_g, bsz, seq, dk, dv):
    t = p.shape[0]
    c = _tile(seq, GLA_CHUNK * GLA_CHUNKS_PER_STEP)
    nc = seq // c
    hk, hv = GLA_HEADS * dk, GLA_HEADS * dv
    scan = _gla_scan_matrix(GLA_CHUNK)
    kern = functools.partial(_gla_kernel, dk=dk, dv=dv)
    row = lambda b, i: b * nc + i
    return pl.pallas_call(
        kern,
        grid=(bsz, nc),
        in_specs=[pl.BlockSpec((c, hk), lambda b, i: (row(b, i), 0)),
                  pl.BlockSpec((c, hk), lambda b, i: (row(b, i), 1)),
                  pl.BlockSpec((c, hv), lambda b, i: (row(b, i), (2 * hk) // hv)),
                  pl.BlockSpec((c, hv), lambda b, i: (row(b, i), (2 * hk) // hv + 1)),
                  pl.BlockSpec((c, V7X_LANES), lambda b, i: (row(b, i), gk_block)),
                  pl.BlockSpec(gkw.shape, lambda b, i: (0, 0)),
                  pl.BlockSpec((1, hk), lambda b, i: (0, 0)),
                  pl.BlockSpec((1, dv), lambda b, i: (0, 0)),
                  pl.BlockSpec(scan.shape, lambda b, i: (0, 0))],
        out_specs=pl.BlockSpec((c, hv), lambda b, i: (row(b, i), 0)),
        out_shape=jax.ShapeDtypeStruct((t, hv), BF16),
        scratch_shapes=[pltpu.VMEM((GLA_HEADS, dv, dk), F32)],
        compiler_params=_params("parallel", "arbitrary"),
        name="gla",
    )(p, p, p, p, p, gkw, gkb.reshape(1, hk), onorm_g.reshape(1, dv), scan)


def _flash_kernel(qt_ref, k_ref, vt_ref, o_ref, acc_ref, sa_ref, sb_ref, pa_ref, pb_ref):
    strip, cw = MLA_STRIP, MLA_CHAIN
    hp, nch = acc_ref.shape[0], acc_ref.shape[1]
    chains = [(h, a) for h in range(hp) for a in range(nch)]
    per_block = MLA_QBLOCK // strip
    assert per_block == 2 and nch == 2 and strip == cw
    i = pl.program_id(2)
    acc_ref[...] = jnp.zeros_like(acc_ref)
    pb_ref[...] = jnp.zeros_like(pb_ref)

    def scores(h, a, strip_idx):
        start = pl.multiple_of(strip_idx * strip, strip)
        return _dot(k_ref[h, pl.ds(start, strip), :],
                    qt_ref[h * MLA_QW:(h + 1) * MLA_QW, a * cw:(a + 1) * cw])

    def col_max(s):
        return jnp.max(s, axis=0, keepdims=True)

    def softmax(s, s_max, m):
        m_new = jnp.maximum(m, s_max)
        return jnp.exp2((s - m_new).astype(BF16)), jnp.exp2(m - m_new), m_new

    def accumulate(h, a, strip_idx, p, alpha):
        acc_ref[h, a] = alpha * acc_ref[h, a] + _dot(vt_ref[h, strip_idx], p)

    def half_step(s_cur, s_nxt, p_cur, p_prev, t, carry):
        out = []
        for c, (h, a) in enumerate(chains):
            m, cur_max, alpha_prev = carry[c]
            s = scores(h, a, t + 1)
            s_nxt[h, a] = s
            accumulate(h, a, jnp.maximum(t - 1, 0), p_prev[h, a], alpha_prev)
            p, alpha, m = softmax(s_cur[h, a], cur_max, m)
            p_cur[h, a] = p
            out.append((m, col_max(s), alpha))
        return tuple(out)

    def body(jj, carry):
        carry = half_step(sa_ref, sb_ref, pa_ref, pb_ref, 2 * jj, carry)
        return half_step(sb_ref, sa_ref, pb_ref, pa_ref, 2 * jj + 1, carry)

    init = []
    for h, a in chains:
        s = scores(h, a, 0)
        sa_ref[h, a] = s
        init.append((jnp.full((1, cw), MASK_VALUE, F32), col_max(s), jnp.ones((1, cw), F32)))
    carry = lax.fori_loop(0, i, body, tuple(init))
    first = per_block * i
    causal = lax.broadcasted_iota(jnp.int32, (strip, cw), 0) <= lax.broadcasted_iota(jnp.int32, (strip, cw), 1)
    last = [scores(h, 1, first + 1) for h in range(hp)]

    def update(h, a, strip_idx, s, s_max, m):
        p, alpha, m = softmax(s, s_max, m)
        accumulate(h, a, strip_idx, p, alpha)
        return m

    for c, (h, a) in enumerate(chains):
        m, s_max, alpha_prev = carry[c]
        accumulate(h, a, jnp.maximum(first - 1, 0), pb_ref[h, a], alpha_prev)
        if a == 0:
            s = jnp.where(causal, sa_ref[h, a], MASK_VALUE)
            update(h, a, first, s, col_max(s), m)
        else:
            m = update(h, a, first, sa_ref[h, a], s_max, m)
            s = jnp.where(causal, last[h], MASK_VALUE)
            update(h, a, first + 1, s, col_max(s), m)
        o_t = acc_ref[h, a, :MLA_V, :] / acc_ref[h, a, MLA_V:MLA_V + 1, :]
        o_ref[a * cw:(a + 1) * cw, h * MLA_V:(h + 1) * MLA_V] = o_t.T.astype(o_ref.dtype)


def _flash(qt, k, vt, bsz, seq):
    t = k.shape[1]
    assert seq % MLA_QBLOCK == 0
    nq = seq // MLA_QBLOCK
    hp = MLA_HEAD_GROUP
    nch = MLA_QBLOCK // MLA_CHAIN
    return pl.pallas_call(
        _flash_kernel,
        grid=(bsz, MLA_HEADS // hp, nq),
        in_specs=[pl.BlockSpec((hp * MLA_QW, MLA_QBLOCK), lambda b, g, i: (g, b * nq + i)),
                  pl.BlockSpec((hp, seq, MLA_QW), lambda b, g, i: (g, b, 0)),
                  pl.BlockSpec((hp, seq // MLA_STRIP, MLA_VP, MLA_STRIP), lambda b, g, i: (g, b, 0, 0))],
        out_specs=pl.BlockSpec((MLA_QBLOCK, hp * MLA_V), lambda b, g, i: (b * nq + i, g)),
        out_shape=jax.ShapeDtypeStruct((t, MLA_HEADS * MLA_V), BF16),
        scratch_shapes=[pltpu.VMEM((hp, nch, MLA_VP, MLA_CHAIN), F32),
                        pltpu.VMEM((hp, nch, MLA_STRIP, MLA_CHAIN), F32),
                        pltpu.VMEM((hp, nch, MLA_STRIP, MLA_CHAIN), F32),
                        pltpu.VMEM((hp, nch, MLA_STRIP, MLA_CHAIN), BF16),
                        pltpu.VMEM((hp, nch, MLA_STRIP, MLA_CHAIN), BF16)],
        compiler_params=_params("parallel", "parallel", "arbitrary"),
        name="flash",
    )(qt, k, vt)


def _mixout_kernel(og_ref, om_ref, ga_ref, gb_ref, x_ref, mod_ref, gf_ref, wg_ref, wm_ref, wo_ref, o_ref, h_ref):
    yg = _dot(og_ref[...], wg_ref[...])
    ym = _dot(om_ref[...], wm_ref[...])
    ga = jax.nn.sigmoid(ga_ref[...].astype(F32))
    gb = jax.nn.sigmoid(gb_ref[...].astype(F32))
    merged = (ga * yg + gb * ym).astype(BF16)
    x1 = x_ref[...] + mod_ref[0, MOD_GATE_MIX:MOD_GATE_MIX + 1, :] * _dot(merged, wo_ref[...])
    o_ref[...] = x1
    h_ref[...] = _modnorm(x1, gf_ref[...], mod_ref[0, MOD_SCALE_FFN:MOD_SCALE_FFN + 1, :],
                          mod_ref[0, MOD_SHIFT_FFN:MOD_SHIFT_FFN + 1, :]).astype(BF16)


def _mixout(o_gla, o_mla, p, gate_col, x2, mod3, ffn_gain, w_gla, w_mla, w_out, seq):
    t, d = x2.shape
    tm = _tile(seq, MIXOUT_ROW_TILE)
    assert gate_col % d == 0
    row = lambda i: (i, 0)
    resident = lambda w: pl.BlockSpec(w.shape, lambda i: (0, 0), pipeline_mode=pl.Buffered(1))
    return pl.pallas_call(
        _mixout_kernel,
        grid=(t // tm,),
        in_specs=[pl.BlockSpec((tm, o_gla.shape[1]), row),
                  pl.BlockSpec((tm, o_mla.shape[1]), row),
                  pl.BlockSpec((tm, d), lambda i: (i, gate_col // d)),
                  pl.BlockSpec((tm, d), lambda i: (i, gate_col // d + 1)),
                  pl.BlockSpec((tm, d), row),
                  pl.BlockSpec((1, N_MOD, d), lambda i: ((i * tm) // seq, 0, 0)),
                  pl.BlockSpec((1, d), lambda i: (0, 0)),
                  resident(w_gla), resident(w_mla), resident(w_out)],
        out_specs=[pl.BlockSpec((tm, d), row), pl.BlockSpec((tm, d), row)],
        out_shape=[jax.ShapeDtypeStruct((t, d), F32), jax.ShapeDtypeStruct((t, d), BF16)],
        compiler_params=_params("parallel"),
        name="mixout",
    )(o_gla, o_mla, p, p, x2, mod3, ffn_gain.reshape(1, d), w_gla, w_mla, w_out)


def _ffn_kernel(x_ref, h_ref, mod_ref, wg_ref, wu_ref, wd_ref, fg_ref, o_ref, acc_ref, *, final_norm):
    f = pl.program_id(1)

    @pl.when(f == 0)
    def _():
        acc_ref[...] = jnp.zeros_like(acc_ref)

    h = h_ref[...]
    gate = _dot(h, wg_ref[...])
    up = _dot(h, wu_ref[...])
    act = (gate * jax.nn.sigmoid(gate) * up).astype(BF16)
    acc_ref[...] += _dot(act, wd_ref[...])

    @pl.when(f == pl.num_programs(1) - 1)
    def _():
        y = x_ref[...] + mod_ref[0, MOD_GATE_FFN:MOD_GATE_FFN + 1, :] * acc_ref[...]
        if final_norm:
            y = _rms(y) * fg_ref[...]
        o_ref[...] = y


def _ffn(x1, h, mod3, w_in, w_down, final_g, seq, final_norm):
    t, d = x1.shape
    hidden = w_down.shape[0]
    tm = _tile(seq, FFN_ROW_TILE)
    tf = _tile(hidden, FFN_HIDDEN_BLOCK)
    nf = hidden // tf
    kern = functools.partial(_ffn_kernel, final_norm=final_norm)
    return pl.pallas_call(
        kern,
        grid=(t // tm, nf),
        in_specs=[pl.BlockSpec((tm, d), lambda i, f: (i, 0)),
                  pl.BlockSpec((tm, d), lambda i, f: (i, 0)),
                  pl.BlockSpec((1, N_MOD, d), lambda i, f: ((i * tm) // seq, 0, 0)),
                  pl.BlockSpec((d, tf), lambda i, f: (0, f)),
                  pl.BlockSpec((d, tf), lambda i, f: (0, nf + f)),
                  pl.BlockSpec((tf, d), lambda i, f: (f, 0)),
                  pl.BlockSpec((1, d), lambda i, f: (0, 0))],
        out_specs=pl.BlockSpec((tm, d), lambda i, f: (i, 0)),
        out_shape=jax.ShapeDtypeStruct((t, d), F32),
        scratch_shapes=[pltpu.VMEM((tm, d), F32)],
        compiler_params=_params("parallel", "arbitrary"),
        name="ffn",
    )(x1, h, mod3, w_in, w_in, w_down, final_g.reshape(1, d))


def _pack_w_in_kernel(wt_ref, o_ref, *, moves, kr_row, gk_row, tail_col):
    for dst, src, width in moves:
        o_ref[:, dst:dst + width] = wt_ref[src:src + width, :].T.astype(BF16)
    half = MLA_ROPE // 2
    kr = wt_ref[kr_row:kr_row + MLA_ROPE, :]
    gk = wt_ref[gk_row:gk_row + GLA_GATE_RANK, :]
    pad = jnp.zeros((V7X_LANES - GLA_GATE_RANK, kr.shape[1]), F32)
    o_ref[:, tail_col:] = jnp.concatenate([kr, kr[half:], kr[:half], gk, pad], axis=0).T.astype(BF16)


def _pack_w_in(w, d, dk_all, dv_all, q_rank, kv_rank):
    widths = (dk_all, dk_all, dv_all, dv_all, GLA_GATE_RANK, q_rank, kv_rank, MLA_ROPE, d, d)
    offs = [0]
    for wd in widths:
        offs.append(offs[-1] + wd)
    gla_w = offs[4]
    moves = ((0, 0, gla_w), (gla_w, offs[8], 2 * d), (gla_w + 2 * d, offs[5], q_rank + kv_rank))
    tail_col = gla_w + 2 * d + q_rank + kv_rank
    n_out = tail_col + 2 * V7X_LANES
    assert all(dst % V7X_LANES == 0 for dst, _, _ in moves) and tail_col % V7X_LANES == 0
    rows = w.shape[0]
    tr = _tile(rows, PACK_ROW_TILE)
    kern = functools.partial(_pack_w_in_kernel, moves=moves, kr_row=offs[7], gk_row=offs[4], tail_col=tail_col)
    return pl.pallas_call(
        kern,
        grid=(rows // tr,),
        in_specs=[pl.BlockSpec((w.shape[1], tr), lambda i: (0, i))],
        out_specs=pl.BlockSpec((tr, n_out), lambda i: (i, 0)),
        out_shape=jax.ShapeDtypeStruct((rows, n_out), BF16),
        compiler_params=_params("parallel"),
        name="pack_w_in",
    )(w.T)


def _pack_wuq_t(w):
    r = w.shape[0]
    half = MLA_ROPE // 2
    w3 = w.reshape(r, MLA_HEADS, MLA_NOPE + MLA_ROPE)
    rope = w3[:, :, MLA_NOPE:]
    rot = jnp.concatenate([rope[:, :, half:], rope[:, :, :half]], axis=2)
    return jnp.concatenate([w3[:, :, :MLA_NOPE], rope, rot], axis=2).reshape(r, -1).T.astype(BF16)


def _pack_wukv(w):
    r = w.shape[0]
    w3 = w.reshape(r, MLA_HEADS, MLA_NOPE + MLA_V)
    wk = w3[:, :, :MLA_NOPE].reshape(r, -1).astype(BF16)
    wvt = w3[:, :, MLA_NOPE:].reshape(r, -1).T.astype(BF16)
    return wk, wvt


def kernel(x, c, positions, ada_w, ada_b, norm_mix_g, w_in, gla_gk_w, gla_gk_b, gla_onorm_g, gla_wo, mla_q_norm_g, mla_wuq, mla_kv_norm_g, mla_wukv, mla_wo, w_out, norm_ffn_g, ffn_w_in, ffn_w_down, final_norm_g):
    bsz, seq, d = x.shape
    depth = w_in.shape[0]
    dk_all = gla_gk_w.shape[2]
    dv_all = gla_wo.shape[1]
    dk, dv = dk_all // GLA_HEADS, dv_all // GLA_HEADS
    q_rank, kv_rank = mla_wuq.shape[1], mla_wukv.shape[1]
    lat_col = 2 * dk_all + 2 * dv_all + 2 * d
    lat_width = q_rank + kv_rank + 2 * V7X_LANES
    gate_col = 2 * dk_all + 2 * dv_all
    assert lat_col % lat_width == 0 and (lat_col + lat_width - V7X_LANES) % V7X_LANES == 0

    x2 = x.reshape(bsz * seq, d)
    cos_t, sin_t = _rope_tab(positions)
    for l in range(depth):
        mod3 = _ada(c, ada_w[l], ada_b[l]).reshape(bsz, N_MOD, d)
        w_all = _pack_w_in(w_in[l], d, dk_all, dv_all, q_rank, kv_rank)
        p = _inproj(x2, mod3, norm_mix_g[l], w_all, seq)
        wk, wvt = _pack_wukv(mla_wukv[l])
        qt, k, vt = _mlaproj(p, lat_col // lat_width, lat_width, cos_t, sin_t, mla_q_norm_g[l], mla_kv_norm_g[l],
                             _pack_wuq_t(mla_wuq[l]), wk, wvt)
        gkw = jnp.zeros((V7X_LANES, dk_all), BF16).at[:GLA_GATE_RANK].set(gla_gk_w[l].astype(BF16))
        o_gla = _gla(p, (lat_col + lat_width - V7X_LANES) // V7X_LANES, gkw, gla_gk_b[l], gla_onorm_g[l],
                     bsz, seq, dk, dv)
        o_mla = _flash(qt, k, vt, bsz, seq)
        x1, h_ffn = _mixout(o_gla, o_mla, p, gate_col, x2, mod3, norm_ffn_g[l], gla_wo[l].astype(BF16),
                            mla_wo[l].astype(BF16), w_out[l].astype(BF16), seq)
        x2 = _ffn(x1, h_ffn, mod3, ffn_w_in[l].astype(BF16), ffn_w_down[l].astype(BF16), final_norm_g,
                  seq, final_norm=(l == depth - 1))
    return x2.reshape(bsz, seq, d)
```

```python
import functools

import jax
import jax.numpy as jnp
from jax import lax
from jax.experimental import pallas as pl
from jax.experimental.pallas import tpu as pltpu

F32 = jnp.float32
BF16 = jnp.bfloat16

GLA_HEADS = 4
GLA_GATE_RANK = 16
GLA_GATE_NORMALIZER = 16.0
GLA_CHUNK = 128
GLA_CHUNKS_PER_STEP = 4
MLA_HEADS = 16
MLA_NOPE = 128
MLA_ROPE = 64
MLA_V = 128
MLA_QW = MLA_NOPE + 2 * MLA_ROPE
MLA_QBLOCK = 1024
MLA_STRIP = 512
MLA_CHAIN = 512
MLA_VP = MLA_V + 16
MLA_HEAD_GROUP = 2
ROPE_THETA = 10000.0
LOG2_E = 1.4426950408889634
NORM_EPS = 1e-6
N_MOD = 6
MOD_SHIFT_MIX, MOD_SCALE_MIX, MOD_GATE_MIX, MOD_SHIFT_FFN, MOD_SCALE_FFN, MOD_GATE_FFN = range(N_MOD)

V7X_LANES = 128
V7X_VMEM_LIMIT_BYTES = 56 * 1024 * 1024

ADA_COL_BLOCK = 1024
ROPE_TAB_TOKENS = 4096
INPROJ_ROW_TILE, INPROJ_COL_BLOCK = 1024, 1280
MLAPROJ_ROW_TILE = 512
MIXOUT_ROW_TILE = 256
FFN_UP_ROW_TILE, FFN_HIDDEN_BLOCK = 1024, 512
FFN_DOWN_ROW_TILE = 256
PACK_ROW_TILE = 128

MASK_VALUE = -1e30


def _params(*semantics):
    return pltpu.CompilerParams(dimension_semantics=semantics, vmem_limit_bytes=V7X_VMEM_LIMIT_BYTES)


def _dot(a, b):
    return jnp.dot(a, b, preferred_element_type=F32)


def _dot_nt(a, b):
    return lax.dot_general(a, b, (((1,), (1,)), ((), ())), preferred_element_type=F32)


def _dot_tn(a, b):
    return lax.dot_general(a, b, (((0,), (0,)), ((), ())), preferred_element_type=F32)


def _rms(x):
    return x * lax.rsqrt(jnp.mean(x * x, axis=-1, keepdims=True) + NORM_EPS)


def _modnorm(x, gain, scale, shift):
    return _rms(x) * (gain * (1.0 + scale)) + shift


def _tile(n, want):
    t = min(n, want)
    assert n % t == 0, (n, want)
    return t


def _ada_kernel(c_ref, w_ref, b_ref, o_ref):
    c = c_ref[...]
    act = (c * jax.nn.sigmoid(c)).astype(BF16)
    o_ref[...] = _dot(act, w_ref[...].astype(BF16)) + b_ref[...]


def _ada(c, w, b):
    bsz, d = c.shape
    n = w.shape[1]
    tn = _tile(n, ADA_COL_BLOCK)
    return pl.pallas_call(
        _ada_kernel,
        grid=(n // tn,),
        in_specs=[pl.BlockSpec((bsz, d), lambda j: (0, 0)),
                  pl.BlockSpec((d, tn), lambda j: (0, j)),
                  pl.BlockSpec((1, tn), lambda j: (0, j))],
        out_specs=pl.BlockSpec((bsz, tn), lambda j: (0, j)),
        out_shape=jax.ShapeDtypeStruct((bsz, n), F32),
        compiler_params=_params("arbitrary"),
        name="ada",
    )(c, w, b.reshape(1, n))


def _rope_tab_kernel(pos_ref, f_ref, cos_ref, sin_ref):
    ang = pos_ref[...].astype(F32) * f_ref[...]
    cos_ref[...] = jnp.cos(ang)
    sin_ref[...] = jnp.sin(ang)


def _rope_tab(positions):
    t = positions.size
    half = MLA_ROPE // 2
    inv_freq = ROPE_THETA ** (-jnp.arange(0, MLA_ROPE, 2, dtype=F32) / MLA_ROPE)
    tt = _tile(t, ROPE_TAB_TOKENS)
    spec = pl.BlockSpec((half, tt), lambda i: (0, i))
    return pl.pallas_call(
        _rope_tab_kernel,
        grid=(t // tt,),
        in_specs=[pl.BlockSpec((1, tt), lambda i: (0, i)), pl.BlockSpec((half, 1), lambda i: (0, 0))],
        out_specs=[spec, spec],
        out_shape=[jax.ShapeDtypeStruct((half, t), F32)] * 2,
        compiler_params=_params("arbitrary"),
        name="rope_tab",
    )(positions.reshape(1, t), inv_freq.reshape(half, 1))


def _inproj_kernel(x_ref, mod_ref, g_ref, w_ref, o_ref, h_ref):
    @pl.when(pl.program_id(1) == 0)
    def _():
        h = _modnorm(x_ref[...], g_ref[...],
                     mod_ref[0, MOD_SCALE_MIX:MOD_SCALE_MIX + 1, :], mod_ref[0, MOD_SHIFT_MIX:MOD_SHIFT_MIX + 1, :])
        h_ref[...] = h.astype(BF16)

    o_ref[...] = _dot(h_ref[...], w_ref[...]).astype(o_ref.dtype)


def _inproj(x2, mod3, gain, w, seq):
    t, d = x2.shape
    n = w.shape[1]
    tm = _tile(seq, INPROJ_ROW_TILE)
    tn = _tile(n, INPROJ_COL_BLOCK)
    return pl.pallas_call(
        _inproj_kernel,
        grid=(t // tm, n // tn),
        in_specs=[pl.BlockSpec((tm, d), lambda i, j: (i, 0)),
                  pl.BlockSpec((1, N_MOD, d), lambda i, j: ((i * tm) // seq, 0, 0)),
                  pl.BlockSpec((1, d), lambda i, j: (0, 0)),
                  pl.BlockSpec((d, tn), lambda i, j: (0, j))],
        out_specs=pl.BlockSpec((tm, tn), lambda i, j: (i, j)),
        out_shape=jax.ShapeDtypeStruct((t, n), BF16),
        scratch_shapes=[pltpu.VMEM((tm, d), BF16)],
        compiler_params=_params("parallel", "arbitrary"),
        name="inproj",
    )(x2, mod3, gain.reshape(1, d), w)


def _mlaproj_kernel(lat_ref, cos_ref, sin_ref, gq_ref, gkv_ref, wqt_ref, wk_ref, wvt_ref, qt_ref, k_ref, vt_ref, *,
                    q_rank, kv_rank, scale):
    tm = lat_ref.shape[0]
    cos = cos_ref[...]
    sin = sin_ref[...]
    cs = jnp.concatenate([cos, cos], axis=0)
    sn = jnp.concatenate([-sin, sin], axis=0)

    cqn = _rms(lat_ref[:, 0:q_rank].astype(F32)) * gq_ref[...]
    ckvn = _rms(lat_ref[:, q_rank:q_rank + kv_rank].astype(F32)) * gkv_ref[...]
    cqn_t = cqn.T.astype(BF16)
    ckvn_t = ckvn.T.astype(BF16)
    ckvn = ckvn.astype(BF16)

    kr_t = lat_ref[:, q_rank + kv_rank:q_rank + kv_rank + V7X_LANES].astype(F32).T
    k_rope_t = kr_t[:MLA_ROPE] * cs + kr_t[MLA_ROPE:] * sn
    k_rope = jnp.concatenate([k_rope_t, jnp.zeros_like(k_rope_t)], axis=0).T.astype(BF16)

    for h in range(MLA_HEADS):
        qt = _dot(wqt_ref[h * MLA_QW:(h + 1) * MLA_QW, :], cqn_t)
        q_rope = qt[MLA_NOPE:MLA_NOPE + MLA_ROPE] * cs + qt[MLA_NOPE + MLA_ROPE:] * sn
        qt_ref[h * MLA_QW:h * MLA_QW + MLA_NOPE, :] = (qt[:MLA_NOPE] * scale).astype(BF16)
        qt_ref[h * MLA_QW + MLA_NOPE:h * MLA_QW + MLA_NOPE + MLA_ROPE, :] = (q_rope * scale).astype(BF16)
        qt_ref[h * MLA_QW + MLA_NOPE + MLA_ROPE:(h + 1) * MLA_QW, :] = jnp.zeros((MLA_ROPE, tm), BF16)
        vt_ref[h, 0, :MLA_V, :] = _dot(wvt_ref[h * MLA_V:(h + 1) * MLA_V, :], ckvn_t).astype(BF16)
        vt_ref[h, 0, MLA_V:, :] = jnp.ones((MLA_VP - MLA_V, tm), BF16)
    for g in range(MLA_HEADS // 2):
        kk = _dot(ckvn, wk_ref[:, 2 * g * MLA_NOPE:2 * (g + 1) * MLA_NOPE]).astype(BF16)
        for e in range(2):
            grp, col = divmod((2 * g + e) * MLA_QW, MLA_HEAD_GROUP * MLA_QW)
            k_ref[grp, :, col:col + MLA_NOPE] = kk[:, e * MLA_NOPE:(e + 1) * MLA_NOPE]
            k_ref[grp, :, col + MLA_NOPE:col + MLA_QW] = k_rope


def _mlaproj(p, lat_block, lat_width, cos_t, sin_t, gq, gkv, wqt, wk, wvt):
    t = p.shape[0]
    q_rank, kv_rank = gq.shape[0], gkv.shape[0]
    tm = _tile(MLA_STRIP, MLAPROJ_ROW_TILE)
    per_strip = MLA_STRIP // tm
    half = MLA_ROPE // 2
    scale = (MLA_NOPE + MLA_ROPE) ** -0.5 * LOG2_E
    kern = functools.partial(_mlaproj_kernel, q_rank=q_rank, kv_rank=kv_rank, scale=scale)
    return pl.pallas_call(
        kern,
        grid=(t // tm,),
        in_specs=[pl.BlockSpec((tm, lat_width), lambda i: (i, lat_block)),
                  pl.BlockSpec((half, tm), lambda i: (0, i)),
                  pl.BlockSpec((half, tm), lambda i: (0, i)),
                  pl.BlockSpec((1, q_rank), lambda i: (0, 0)),
                  pl.BlockSpec((1, kv_rank), lambda i: (0, 0)),
                  pl.BlockSpec(wqt.shape, lambda i: (0, 0)),
                  pl.BlockSpec(wk.shape, lambda i: (0, 0)),
                  pl.BlockSpec(wvt.shape, lambda i: (0, 0))],
        out_specs=[pl.BlockSpec((MLA_HEADS * MLA_QW, tm), lambda i: (0, i)),
                   pl.BlockSpec((MLA_HEADS // MLA_HEAD_GROUP, tm, MLA_HEAD_GROUP * MLA_QW), lambda i: (0, i, 0)),
                   pl.BlockSpec((MLA_HEADS, 1, MLA_VP, tm), lambda i: (0, i // per_strip, 0, i % per_strip))],
        out_shape=[jax.ShapeDtypeStruct((MLA_HEADS * MLA_QW, t), BF16),
                   jax.ShapeDtypeStruct((MLA_HEADS // MLA_HEAD_GROUP, t, MLA_HEAD_GROUP * MLA_QW), BF16),
                   jax.ShapeDtypeStruct((MLA_HEADS, t // MLA_STRIP, MLA_VP, MLA_STRIP), BF16)],
        compiler_params=_params("parallel"),
        name="mlaproj",
    )(p, cos_t, sin_t, gq.reshape(1, q_rank), gkv.reshape(1, kv_rank), wqt, wk, wvt)


def _gla_kernel(q_ref, k_ref, v_ref, g_ref, gk_ref, gkw_ref, gkb_ref, on_ref, scan_ref, o_ref, st_ref, *, dk, dv):
    c = GLA_CHUNK
    nsub = q_ref.shape[0] // c

    @pl.when(pl.program_id(1) == 0)
    def _():
        st_ref[...] = jnp.zeros_like(st_ref)

    z = _dot(gk_ref[...], gkw_ref[...]) + gkb_ref[...]
    log_a = jax.nn.log_sigmoid(z) * (LOG2_E / GLA_GATE_NORMALIZER)

    halves = [1 << e for e in range(c.bit_length() - 1)]
    ai = lax.broadcasted_iota(jnp.int32, (c, c), 0)
    aj = lax.broadcasted_iota(jnp.int32, (c, c), 1)
    diff = ai ^ aj
    causal_diff = jnp.where(ai > aj, diff, 0)
    level_mask = {s: (causal_diff >> (s.bit_length() - 1)) == 1 for s in halves}
    diag_mask = ai == aj

    g_hi = log_a.astype(BF16)
    g_lo = (log_a - g_hi.astype(F32)).astype(BF16)
    n_lv = len(halves) * c

    def chunk_local(h, u):
        rows, cols = slice(u * c, (u + 1) * c), slice(h * dk, (h + 1) * dk)
        level_sum = _dot(scan_ref[:n_lv, :], g_hi[rows, cols])
        cum = _dot(scan_ref[n_lv:, :], jnp.concatenate([g_hi[rows, cols], g_lo[rows, cols]], axis=1))
        prefix = cum[:, :dk] + cum[:, dk:]
        total = prefix[c - 1:c, :]
        q16 = q_ref[rows, cols] * (dk ** -0.5)
        k16 = k_ref[rows, cols]
        att = jnp.where(diag_mask, _dot_nt(q16, k16), 0.0)
        for li, s in enumerate(halves):
            decay = jnp.exp2(level_sum[li * c:(li + 1) * c, :].astype(BF16))
            att = jnp.where(level_mask[s], _dot_nt(q16 * decay, k16 * decay), att)
        q_dec = (q16.astype(F32) * jnp.exp2(prefix)).astype(BF16)
        k_dec = (k16.astype(F32) * jnp.exp2(total - prefix)).astype(BF16)
        return q_dec, k_dec, att.astype(BF16), jnp.exp2(total)

    for h in range(GLA_HEADS):
        local = [chunk_local(h, u) for u in range(nsub)]
        for u, (q_dec, k_dec, att, state_decay) in enumerate(local):
            rows = slice(u * c, (u + 1) * c)
            vh = v_ref[rows, h * dv:(h + 1) * dv]
            st = st_ref[h]
            o = _dot_nt(q_dec, st.astype(BF16)) + _dot(att, vh)
            st_ref[h] = st * state_decay + _dot_tn(vh, k_dec)
            gate = g_ref[rows, h * dv:(h + 1) * dv].astype(F32)
            o_ref[rows, h * dv:(h + 1) * dv] = (_rms(o) * on_ref[...] * (gate * jax.nn.sigmoid(gate))).astype(o_ref.dtype)


def _gla_scan_matrix(c):
    p = jnp.arange(c)[:, None]
    t = jnp.arange(c)[None, :]
    mats = []
    s = 1
    while s < c:
        start = (p // s) * s
        odd = (p // s) % 2 == 1
        mats.append(jnp.where(odd, (t >= start) & (t <= p), (t > p) & (t < start + s)))
        s *= 2
    mats.append(t <= p)
    return jnp.concatenate(mats, axis=0).astype(BF16)


def _gla(p, gk_block, gkw, gkb, onorm_g, bsz, seq, dk, dv):
    t = p.shape[0]
    c = _tile(seq, GLA_CHUNK * GLA_CHUNKS_PER_STEP)
    nc = seq // c
    hk, hv = GLA_HEADS * dk, GLA_HEADS * dv
    scan = _gla_scan_matrix(GLA_CHUNK)
    kern = functools.partial(_gla_kernel, dk=dk, dv=dv)
    row = lambda b, i: b * nc + i
    return pl.pallas_call(
        kern,
        grid=(bsz, nc),
        in_specs=[pl.BlockSpec((c, hk), lambda b, i: (row(b, i), 0)),
                  pl.BlockSpec((c, hk), lambda b, i: (row(b, i), 1)),
                  pl.BlockSpec((c, hv), lambda b, i: (row(b, i), (2 * hk) // hv)),
                  pl.BlockSpec((c, hv), lambda b, i: (row(b, i), (2 * hk) // hv + 1)),
                  pl.BlockSpec((c, V7X_LANES), lambda b, i: (row(b, i), gk_block)),
                  pl.BlockSpec(gkw.shape, lambda b, i: (0, 0)),
                  pl.BlockSpec((1, hk), lambda b, i: (0, 0)),
                  pl.BlockSpec((1, dv), lambda b, i: (0, 0)),
                  pl.BlockSpec(scan.shape, lambda b, i: (0, 0))],
        out_specs=pl.BlockSpec((c, hv), lambda b, i: (row(b, i), 0)),
        out_shape=jax.ShapeDtypeStruct((t, hv), BF16),
        scratch_shapes=[pltpu.VMEM((GLA_HEADS, dv, dk), F32)],
        compiler_params=_params("parallel", "arbitrary"),
        name="gla",
    )(p, p, p, p, p, gkw, gkb.reshape(1, hk), onorm_g.reshape(1, dv), scan)


def _flash_kernel(qt_ref, k_ref, vt_ref, o_ref, acc_ref, sa_ref, sb_ref, pa_ref, pb_ref):
    strip, cw = MLA_STRIP, MLA_CHAIN
    hp, nch = acc_ref.shape[0], acc_ref.shape[1]
    chains = [(h, a) for h in range(hp) for a in range(nch)]
    per_block = MLA_QBLOCK // strip
    assert per_block == 2 and nch == 2 and strip == cw
    i = pl.program_id(2)
    acc_ref[...] = jnp.zeros_like(acc_ref)
    pb_ref[...] = jnp.zeros_like(pb_ref)

    def scores(h, a, strip_idx):
        start = pl.multiple_of(strip_idx * strip, strip)
        return _dot(k_ref[0, pl.ds(start, strip), h * MLA_QW:(h + 1) * MLA_QW],
                    qt_ref[h * MLA_QW:(h + 1) * MLA_QW, a * cw:(a + 1) * cw])

    def col_max(s):
        return jnp.max(s, axis=0, keepdims=True)

    def softmax(s, s_max, m):
        m_new = jnp.maximum(m, s_max)
        return jnp.exp2((s - m_new).astype(BF16)), jnp.exp2(m - m_new), m_new

    def accumulate(h, a, strip_idx, p, alpha):
        acc_ref[h, a] = alpha * acc_ref[h, a] + _dot(vt_ref[h, strip_idx], p)

    def half_step(s_cur, s_nxt, p_cur, p_prev, t, carry):
        out = []
        for c, (h, a) in enumerate(chains):
            m, cur_max, alpha_prev = carry[c]
            s = scores(h, a, t + 1)
            s_nxt[h, a] = s
            accumulate(h, a, jnp.maximum(t - 1, 0), p_prev[h, a], alpha_prev)
            p, alpha, m = softmax(s_cur[h, a], cur_max, m)
            p_cur[h, a] = p
            out.append((m, col_max(s), alpha))
        return tuple(out)

    def body(jj, carry):
        carry = half_step(sa_ref, sb_ref, pa_ref, pb_ref, 2 * jj, carry)
        return half_step(sb_ref, sa_ref, pb_ref, pa_ref, 2 * jj + 1, carry)

    init = []
    for h, a in chains:
        s = scores(h, a, 0)
        sa_ref[h, a] = s
        init.append((jnp.full((1, cw), MASK_VALUE, F32), col_max(s), jnp.ones((1, cw), F32)))
    carry = lax.fori_loop(0, i, body, tuple(init))
    first = per_block * i
    causal = lax.broadcasted_iota(jnp.int32, (strip, cw), 0) <= lax.broadcasted_iota(jnp.int32, (strip, cw), 1)
    last = [scores(h, 1, first + 1) for h in range(hp)]

    def update(h, a, strip_idx, s, s_max, m):
        p, alpha, m = softmax(s, s_max, m)
        accumulate(h, a, strip_idx, p, alpha)
        return m

    for c, (h, a) in enumerate(chains):
        m, s_max, alpha_prev = carry[c]
        accumulate(h, a, jnp.maximum(first - 1, 0), pb_ref[h, a], alpha_prev)
        if a == 0:
            s = jnp.where(causal, sa_ref[h, a], MASK_VALUE)
            update(h, a, first, s, col_max(s), m)
        else:
            m = update(h, a, first, sa_ref[h, a], s_max, m)
            s = jnp.where(causal, last[h], MASK_VALUE)
            update(h, a, first + 1, s, col_max(s), m)
        o_t = acc_ref[h, a, :MLA_V, :] / acc_ref[h, a, MLA_V:MLA_V + 1, :]
        o_ref[a * cw:(a + 1) * cw, h * MLA_V:(h + 1) * MLA_V] = o_t.T.astype(o_ref.dtype)


def _flash(qt, k, vt, bsz, seq):
    t = k.shape[1]
    assert seq % MLA_QBLOCK == 0
    nq = seq // MLA_QBLOCK
    hp = MLA_HEAD_GROUP
    nch = MLA_QBLOCK // MLA_CHAIN
    return pl.pallas_call(
        _flash_kernel,
        grid=(bsz, MLA_HEADS // hp, nq),
        in_specs=[pl.BlockSpec((hp * MLA_QW, MLA_QBLOCK), lambda b, g, i: (g, b * nq + i)),
                  pl.BlockSpec((1, seq, hp * MLA_QW), lambda b, g, i: (g, b, 0)),
                  pl.BlockSpec((hp, seq // MLA_STRIP, MLA_VP, MLA_STRIP), lambda b, g, i: (g, b, 0, 0))],
        out_specs=pl.BlockSpec((MLA_QBLOCK, hp * MLA_V), lambda b, g, i: (b * nq + i, g)),
        out_shape=jax.ShapeDtypeStruct((t, MLA_HEADS * MLA_V), BF16),
        scratch_shapes=[pltpu.VMEM((hp, nch, MLA_VP, MLA_CHAIN), F32),
                        pltpu.VMEM((hp, nch, MLA_STRIP, MLA_CHAIN), F32),
                        pltpu.VMEM((hp, nch, MLA_STRIP, MLA_CHAIN), F32),
                        pltpu.VMEM((hp, nch, MLA_STRIP, MLA_CHAIN), BF16),
                        pltpu.VMEM((hp, nch, MLA_STRIP, MLA_CHAIN), BF16)],
        compiler_params=_params("parallel", "parallel", "arbitrary"),
        name="flash",
    )(qt, k, vt)


def _mixout_kernel(og_ref, om_ref, ga_ref, gb_ref, x_ref, mod_ref, gf_ref, wg_ref, wm_ref, wo_ref, o_ref, h_ref):
    yg = _dot(og_ref[...], wg_ref[...])
    ym = _dot(om_ref[...], wm_ref[...])
    ga = jax.nn.sigmoid(ga_ref[...].astype(F32))
    gb = jax.nn.sigmoid(gb_ref[...].astype(F32))
    merged = (ga * yg + gb * ym).astype(BF16)
    x1 = x_ref[...] + mod_ref[0, MOD_GATE_MIX:MOD_GATE_MIX + 1, :] * _dot(merged, wo_ref[...])
    o_ref[...] = x1
    h_ref[...] = _modnorm(x1, gf_ref[...], mod_ref[0, MOD_SCALE_FFN:MOD_SCALE_FFN + 1, :],
                          mod_ref[0, MOD_SHIFT_FFN:MOD_SHIFT_FFN + 1, :]).astype(BF16)


def _mixout(o_gla, o_mla, p, gate_col, x2, mod3, ffn_gain, w_gla, w_mla, w_out, seq):
    t, d = x2.shape
    tm = _tile(seq, MIXOUT_ROW_TILE)
    assert gate_col % d == 0
    row = lambda i: (i, 0)
    resident = lambda w: pl.BlockSpec(w.shape, lambda i: (0, 0), pipeline_mode=pl.Buffered(1))
    return pl.pallas_call(
        _mixout_kernel,
        grid=(t // tm,),
        in_specs=[pl.BlockSpec((tm, o_gla.shape[1]), row),
                  pl.BlockSpec((tm, o_mla.shape[1]), row),
                  pl.BlockSpec((tm, d), lambda i: (i, gate_col // d)),
                  pl.BlockSpec((tm, d), lambda i: (i, gate_col // d + 1)),
                  pl.BlockSpec((tm, d), row),
                  pl.BlockSpec((1, N_MOD, d), lambda i: ((i * tm) // seq, 0, 0)),
                  pl.BlockSpec((1, d), lambda i: (0, 0)),
                  resident(w_gla), resident(w_mla), resident(w_out)],
        out_specs=[pl.BlockSpec((tm, d), row), pl.BlockSpec((tm, d), row)],
        out_shape=[jax.ShapeDtypeStruct((t, d), F32), jax.ShapeDtypeStruct((t, d), BF16)],
        compiler_params=_params("parallel"),
        name="mixout",
    )(o_gla, o_mla, p, p, x2, mod3, ffn_gain.reshape(1, d), w_gla, w_mla, w_out)


def _ffn_up_kernel(h_ref, wg_ref, wu_ref, o_ref):
    h = h_ref[...]
    gate = _dot(h, wg_ref[...])
    up = _dot(h, wu_ref[...])
    o_ref[...] = (gate * jax.nn.sigmoid(gate) * up).astype(o_ref.dtype)


def _ffn_down_kernel(a_ref, x_ref, mod_ref, wd_ref, fg_ref, o_ref, *, final_norm):
    y = x_ref[...] + mod_ref[0, MOD_GATE_FFN:MOD_GATE_FFN + 1, :] * _dot(a_ref[...], wd_ref[...])
    if final_norm:
        y = _rms(y) * fg_ref[...]
    o_ref[...] = y


def _ffn(x1, h, mod3, w_in, w_down, final_g, seq, final_norm):
    t, d = x1.shape
    hidden = w_down.shape[0]
    tm = _tile(seq, FFN_UP_ROW_TILE)
    tf = _tile(hidden, FFN_HIDDEN_BLOCK)
    nf = hidden // tf
    act = pl.pallas_call(
        _ffn_up_kernel,
        grid=(t // tm, nf),
        in_specs=[pl.BlockSpec((tm, d), lambda i, f: (i, 0)),
                  pl.BlockSpec((d, tf), lambda i, f: (0, f)),
                  pl.BlockSpec((d, tf), lambda i, f: (0, nf + f))],
        out_specs=pl.BlockSpec((tm, tf), lambda i, f: (i, f)),
        out_shape=jax.ShapeDtypeStruct((t, hidden), BF16),
        compiler_params=_params("parallel", "arbitrary"),
        name="ffn_up",
    )(h, w_in, w_in)
    tr = _tile(seq, FFN_DOWN_ROW_TILE)
    return pl.pallas_call(
        functools.partial(_ffn_down_kernel, final_norm=final_norm),
        grid=(t // tr,),
        in_specs=[pl.BlockSpec((tr, hidden), lambda i: (i, 0)),
                  pl.BlockSpec((tr, d), lambda i: (i, 0)),
                  pl.BlockSpec((1, N_MOD, d), lambda i: ((i * tr) // seq, 0, 0)),
                  pl.BlockSpec(w_down.shape, lambda i: (0, 0), pipeline_mode=pl.Buffered(1)),
                  pl.BlockSpec((1, d), lambda i: (0, 0))],
        out_specs=pl.BlockSpec((tr, d), lambda i: (i, 0)),
        out_shape=jax.ShapeDtypeStruct((t, d), F32),
        compiler_params=_params("parallel"),
        name="ffn_down",
    )(act, x1, mod3, w_down, final_g.reshape(1, d))


def _pack_w_in_kernel(wt_ref, o_ref, *, moves, kr_row, gk_row, tail_col):
    for dst, src, width in moves:
        o_ref[:, dst:dst + width] = wt_ref[src:src + width, :].T.astype(BF16)
    half = MLA_ROPE // 2
    kr = wt_ref[kr_row:kr_row + MLA_ROPE, :]
    gk = wt_ref[gk_row:gk_row + GLA_GATE_RANK, :]
    pad = jnp.zeros((V7X_LANES - GLA_GATE_RANK, kr.shape[1]), F32)
    o_ref[:, tail_col:] = jnp.concatenate([kr, kr[half:], kr[:half], gk, pad], axis=0).T.astype(BF16)


def _pack_w_in(w, d, dk_all, dv_all, q_rank, kv_rank):
    widths = (dk_all, dk_all, dv_all, dv_all, GLA_GATE_RANK, q_rank, kv_rank, MLA_ROPE, d, d)
    offs = [0]
    for wd in widths:
        offs.append(offs[-1] + wd)
    gla_w = offs[4]
    moves = ((0, 0, gla_w), (gla_w, offs[8], 2 * d), (gla_w + 2 * d, offs[5], q_rank + kv_rank))
    tail_col = gla_w + 2 * d + q_rank + kv_rank
    n_out = tail_col + 2 * V7X_LANES
    assert all(dst % V7X_LANES == 0 for dst, _, _ in moves) and tail_col % V7X_LANES == 0
    rows = w.shape[0]
    tr = _tile(rows, PACK_ROW_TILE)
    kern = functools.partial(_pack_w_in_kernel, moves=moves, kr_row=offs[7], gk_row=offs[4], tail_col=tail_col)
    return pl.pallas_call(
        kern,
        grid=(rows // tr,),
        in_specs=[pl.BlockSpec((w.shape[1], tr), lambda i: (0, i))],
        out_specs=pl.BlockSpec((tr, n_out), lambda i: (i, 0)),
        out_shape=jax.ShapeDtypeStruct((rows, n_out), BF16),
        compiler_params=_params("parallel"),
        name="pack_w_in",
    )(w.T)


def _pack_wuq_t(w):
    r = w.shape[0]
    half = MLA_ROPE // 2
    w3 = w.reshape(r, MLA_HEADS, MLA_NOPE + MLA_ROPE)
    rope = w3[:, :, MLA_NOPE:]
    rot = jnp.concatenate([rope[:, :, half:], rope[:, :, :half]], axis=2)
    return jnp.concatenate([w3[:, :, :MLA_NOPE], rope, rot], axis=2).reshape(r, -1).T.astype(BF16)


def _pack_wukv(w):
    r = w.shape[0]
    w3 = w.reshape(r, MLA_HEADS, MLA_NOPE + MLA_V)
    wk = w3[:, :, :MLA_NOPE].reshape(r, -1).astype(BF16)
    wvt = w3[:, :, MLA_NOPE:].reshape(r, -1).T.astype(BF16)
    return wk, wvt


def kernel(x, c, positions, ada_w, ada_b, norm_mix_g, w_in, gla_gk_w, gla_gk_b, gla_onorm_g, gla_wo, mla_q_norm_g, mla_wuq, mla_kv_norm_g, mla_wukv, mla_wo, w_out, norm_ffn_g, ffn_w_in, ffn_w_down, final_norm_g):
    bsz, seq, d = x.shape
    depth = w_in.shape[0]
    dk_all = gla_gk_w.shape[2]
    dv_all = gla_wo.shape[1]
    dk, dv = dk_all // GLA_HEADS, dv_all // GLA_HEADS
    q_rank, kv_rank = mla_wuq.shape[1], mla_wukv.shape[1]
    lat_col = 2 * dk_all + 2 * dv_all + 2 * d
    lat_width = q_rank + kv_rank + 2 * V7X_LANES
    gate_col = 2 * dk_all + 2 * dv_all
    assert lat_col % lat_width == 0 and (lat_col + lat_width - V7X_LANES) % V7X_LANES == 0

    x2 = x.reshape(bsz * seq, d)
    cos_t, sin_t = _rope_tab(positions)
    for l in range(depth):
        mod3 = _ada(c, ada_w[l], ada_b[l]).reshape(bsz, N_MOD, d)
        w_all = _pack_w_in(w_in[l], d, dk_all, dv_all, q_rank, kv_rank)
        p = _inproj(x2, mod3, norm_mix_g[l], w_all, seq)
        wk, wvt = _pack_wukv(mla_wukv[l])
        qt, k, vt = _mlaproj(p, lat_col // lat_width, lat_width, cos_t, sin_t, mla_q_norm_g[l], mla_kv_norm_g[l],
                             _pack_wuq_t(mla_wuq[l]), wk, wvt)
        gkw = jnp.zeros((V7X_LANES, dk_all), BF16).at[:GLA_GATE_RANK].set(gla_gk_w[l].astype(BF16))
        o_gla = _gla(p, (lat_col + lat_width - V7X_LANES) // V7X_LANES, gkw, gla_gk_b[l], gla_onorm_g[l],
                     bsz, seq, dk, dv)
        o_mla = _flash(qt, k, vt, bsz, seq)
        x1, h_ffn = _mixout(o_gla, o_mla, p, gate_col, x2, mod3, norm_ffn_g[l], gla_wo[l].astype(BF16),
                            mla_wo[l].astype(BF16), w_out[l].astype(BF16), seq)
        x2 = _ffn(x1, h_ffn, mod3, ffn_w_in[l].astype(BF16), ffn_w_down[l].astype(BF16), final_norm_g,
                  seq, final_norm=(l == depth - 1))
    return x2.reshape(bsz, seq, d)
```

```python
import functools

import jax
import jax.numpy as jnp
from jax import lax
from jax.experimental import pallas as pl
from jax.experimental.pallas import tpu as pltpu

F32 = jnp.float32
BF16 = jnp.bfloat16

GLA_HEADS = 4
GLA_GATE_RANK = 16
GLA_GATE_NORMALIZER = 16.0
GLA_CHUNK = 128
GLA_CHUNKS_PER_STEP = 4
MLA_HEADS = 16
MLA_NOPE = 128
MLA_ROPE = 64
MLA_V = 128
MLA_QW = MLA_NOPE + 2 * MLA_ROPE
MLA_QBLOCK = 1024
MLA_STRIP = 512
MLA_CHAIN = 512
MLA_VP = MLA_V + 16
MLA_HEAD_GROUP = 2
ROPE_THETA = 10000.0
LOG2_E = 1.4426950408889634
NORM_EPS = 1e-6
N_MOD = 6
MOD_SHIFT_MIX, MOD_SCALE_MIX, MOD_GATE_MIX, MOD_SHIFT_FFN, MOD_SCALE_FFN, MOD_GATE_FFN = range(N_MOD)

V7X_LANES = 128
V7X_VMEM_LIMIT_BYTES = 56 * 1024 * 1024

ADA_COL_BLOCK = 1024
ROPE_TAB_TOKENS = 4096
INPROJ_ROW_TILE, INPROJ_COL_BLOCK = 1024, 1280
MLAPROJ_ROW_TILE = 512
MIXOUT_ROW_TILE = 512
FFN_UP_ROW_TILE, FFN_HIDDEN_BLOCK = 1024, 512
FFN_DOWN_ROW_TILE = 256
PACK_ROW_TILE = 128

MASK_VALUE = -1e30


def _params(*semantics):
    return pltpu.CompilerParams(dimension_semantics=semantics, vmem_limit_bytes=V7X_VMEM_LIMIT_BYTES)


def _dot(a, b):
    return jnp.dot(a, b, preferred_element_type=F32)


def _dot_nt(a, b):
    return lax.dot_general(a, b, (((1,), (1,)), ((), ())), preferred_element_type=F32)


def _dot_tn(a, b):
    return lax.dot_general(a, b, (((0,), (0,)), ((), ())), preferred_element_type=F32)


def _rms(x):
    return x * lax.rsqrt(jnp.mean(x * x, axis=-1, keepdims=True) + NORM_EPS)


def _modnorm(x, gain, scale, shift):
    return _rms(x) * (gain * (1.0 + scale)) + shift


def _tile(n, want):
    t = min(n, want)
    assert n % t == 0, (n, want)
    return t


def _ada_kernel(c_ref, w_ref, b_ref, o_ref):
    c = c_ref[...]
    act = (c * jax.nn.sigmoid(c)).astype(BF16)
    o_ref[...] = _dot(act, w_ref[...].astype(BF16)) + b_ref[...]


def _ada(c, w, b):
    bsz, d = c.shape
    n = w.shape[1]
    tn = _tile(n, ADA_COL_BLOCK)
    return pl.pallas_call(
        _ada_kernel,
        grid=(n // tn,),
        in_specs=[pl.BlockSpec((bsz, d), lambda j: (0, 0)),
                  pl.BlockSpec((d, tn), lambda j: (0, j)),
                  pl.BlockSpec((1, tn), lambda j: (0, j))],
        out_specs=pl.BlockSpec((bsz, tn), lambda j: (0, j)),
        out_shape=jax.ShapeDtypeStruct((bsz, n), F32),
        compiler_params=_params("arbitrary"),
        name="ada",
    )(c, w, b.reshape(1, n))


def _rope_tab_kernel(pos_ref, f_ref, cos_ref, sin_ref):
    ang = pos_ref[...].astype(F32) * f_ref[...]
    cos_ref[...] = jnp.cos(ang)
    sin_ref[...] = jnp.sin(ang)


def _rope_tab(positions):
    t = positions.size
    half = MLA_ROPE // 2
    inv_freq = ROPE_THETA ** (-jnp.arange(0, MLA_ROPE, 2, dtype=F32) / MLA_ROPE)
    tt = _tile(t, ROPE_TAB_TOKENS)
    spec = pl.BlockSpec((half, tt), lambda i: (0, i))
    return pl.pallas_call(
        _rope_tab_kernel,
        grid=(t // tt,),
        in_specs=[pl.BlockSpec((1, tt), lambda i: (0, i)), pl.BlockSpec((half, 1), lambda i: (0, 0))],
        out_specs=[spec, spec],
        out_shape=[jax.ShapeDtypeStruct((half, t), F32)] * 2,
        compiler_params=_params("arbitrary"),
        name="rope_tab",
    )(positions.reshape(1, t), inv_freq.reshape(half, 1))


def _inproj_kernel(x_ref, mod_ref, g_ref, w_ref, o_ref, h_ref):
    @pl.when(pl.program_id(1) == 0)
    def _():
        h = _modnorm(x_ref[...], g_ref[...],
                     mod_ref[0, MOD_SCALE_MIX:MOD_SCALE_MIX + 1, :], mod_ref[0, MOD_SHIFT_MIX:MOD_SHIFT_MIX + 1, :])
        h_ref[...] = h.astype(BF16)

    o_ref[...] = _dot(h_ref[...], w_ref[...]).astype(o_ref.dtype)


def _inproj(x2, mod3, gain, w, seq):
    t, d = x2.shape
    n = w.shape[1]
    tm = _tile(seq, INPROJ_ROW_TILE)
    tn = _tile(n, INPROJ_COL_BLOCK)
    return pl.pallas_call(
        _inproj_kernel,
        grid=(t // tm, n // tn),
        in_specs=[pl.BlockSpec((tm, d), lambda i, j: (i, 0)),
                  pl.BlockSpec((1, N_MOD, d), lambda i, j: ((i * tm) // seq, 0, 0)),
                  pl.BlockSpec((1, d), lambda i, j: (0, 0)),
                  pl.BlockSpec((d, tn), lambda i, j: (0, j))],
        out_specs=pl.BlockSpec((tm, tn), lambda i, j: (i, j)),
        out_shape=jax.ShapeDtypeStruct((t, n), BF16),
        scratch_shapes=[pltpu.VMEM((tm, d), BF16)],
        compiler_params=_params("parallel", "arbitrary"),
        name="inproj",
    )(x2, mod3, gain.reshape(1, d), w)


def _mlaproj_kernel(lat_ref, cos_ref, sin_ref, gq_ref, gkv_ref, wqt_ref, wk_ref, wvt_ref, qt_ref, k_ref, vt_ref, *,
                    q_rank, kv_rank, scale):
    tm = lat_ref.shape[0]
    cos = cos_ref[...]
    sin = sin_ref[...]
    cs = jnp.concatenate([cos, cos], axis=0)
    sn = jnp.concatenate([-sin, sin], axis=0)

    cqn = _rms(lat_ref[:, 0:q_rank].astype(F32)) * gq_ref[...]
    ckvn = _rms(lat_ref[:, q_rank:q_rank + kv_rank].astype(F32)) * gkv_ref[...]
    cqn_t = cqn.T.astype(BF16)
    ckvn_t = ckvn.T.astype(BF16)
    ckvn = ckvn.astype(BF16)

    kr_t = lat_ref[:, q_rank + kv_rank:q_rank + kv_rank + V7X_LANES].astype(F32).T
    k_rope_t = kr_t[:MLA_ROPE] * cs + kr_t[MLA_ROPE:] * sn
    k_rope = jnp.concatenate([k_rope_t, jnp.zeros_like(k_rope_t)], axis=0).T.astype(BF16)

    for h in range(MLA_HEADS):
        qt = _dot(wqt_ref[h * MLA_QW:(h + 1) * MLA_QW, :], cqn_t)
        q_rope = qt[MLA_NOPE:MLA_NOPE + MLA_ROPE] * cs + qt[MLA_NOPE + MLA_ROPE:] * sn
        qt_ref[h * MLA_QW:h * MLA_QW + MLA_NOPE, :] = (qt[:MLA_NOPE] * scale).astype(BF16)
        qt_ref[h * MLA_QW + MLA_NOPE:h * MLA_QW + MLA_NOPE + MLA_ROPE, :] = (q_rope * scale).astype(BF16)
        qt_ref[h * MLA_QW + MLA_NOPE + MLA_ROPE:(h + 1) * MLA_QW, :] = jnp.zeros((MLA_ROPE, tm), BF16)
        vt_ref[h, 0, :MLA_V, :] = _dot(wvt_ref[h * MLA_V:(h + 1) * MLA_V, :], ckvn_t).astype(BF16)
        vt_ref[h, 0, MLA_V:, :] = jnp.ones((MLA_VP - MLA_V, tm), BF16)
    for g in range(MLA_HEADS // 2):
        kk = _dot(ckvn, wk_ref[:, 2 * g * MLA_NOPE:2 * (g + 1) * MLA_NOPE]).astype(BF16)
        for e in range(2):
            grp, col = divmod((2 * g + e) * MLA_QW, MLA_HEAD_GROUP * MLA_QW)
            k_ref[grp, :, col:col + MLA_NOPE] = kk[:, e * MLA_NOPE:(e + 1) * MLA_NOPE]
            k_ref[grp, :, col + MLA_NOPE:col + MLA_QW] = k_rope


def _mlaproj(p, lat_block, lat_width, cos_t, sin_t, gq, gkv, wqt, wk, wvt):
    t = p.shape[0]
    q_rank, kv_rank = gq.shape[0], gkv.shape[0]
    tm = _tile(MLA_STRIP, MLAPROJ_ROW_TILE)
    per_strip = MLA_STRIP // tm
    half = MLA_ROPE // 2
    scale = (MLA_NOPE + MLA_ROPE) ** -0.5 * LOG2_E
    kern = functools.partial(_mlaproj_kernel, q_rank=q_rank, kv_rank=kv_rank, scale=scale)
    return pl.pallas_call(
        kern,
        grid=(t // tm,),
        in_specs=[pl.BlockSpec((tm, lat_width), lambda i: (i, lat_block)),
                  pl.BlockSpec((half, tm), lambda i: (0, i)),
                  pl.BlockSpec((half, tm), lambda i: (0, i)),
                  pl.BlockSpec((1, q_rank), lambda i: (0, 0)),
                  pl.BlockSpec((1, kv_rank), lambda i: (0, 0)),
                  pl.BlockSpec(wqt.shape, lambda i: (0, 0)),
                  pl.BlockSpec(wk.shape, lambda i: (0, 0)),
                  pl.BlockSpec(wvt.shape, lambda i: (0, 0))],
        out_specs=[pl.BlockSpec((MLA_HEADS * MLA_QW, tm), lambda i: (0, i)),
                   pl.BlockSpec((MLA_HEADS // MLA_HEAD_GROUP, tm, MLA_HEAD_GROUP * MLA_QW), lambda i: (0, i, 0)),
                   pl.BlockSpec((MLA_HEADS, 1, MLA_VP, tm), lambda i: (0, i // per_strip, 0, i % per_strip))],
        out_shape=[jax.ShapeDtypeStruct((MLA_HEADS * MLA_QW, t), BF16),
                   jax.ShapeDtypeStruct((MLA_HEADS // MLA_HEAD_GROUP, t, MLA_HEAD_GROUP * MLA_QW), BF16),
                   jax.ShapeDtypeStruct((MLA_HEADS, t // MLA_STRIP, MLA_VP, MLA_STRIP), BF16)],
        compiler_params=_params("parallel"),
        name="mlaproj",
    )(p, cos_t, sin_t, gq.reshape(1, q_rank), gkv.reshape(1, kv_rank), wqt, wk, wvt)


def _gla_kernel(q_ref, k_ref, v_ref, g_ref, gk_ref, gkw_ref, gkb_ref, on_ref, scan_ref, o_ref, st_ref, *, dk, dv):
    c = GLA_CHUNK
    nsub = q_ref.shape[0] // c

    @pl.when(pl.program_id(1) == 0)
    def _():
        st_ref[...] = jnp.zeros_like(st_ref)

    z = _dot(gk_ref[...], gkw_ref[...]) + gkb_ref[...]
    log_a = jax.nn.log_sigmoid(z) * (LOG2_E / GLA_GATE_NORMALIZER)

    halves = [1 << e for e in range(c.bit_length() - 1)]
    ai = lax.broadcasted_iota(jnp.int32, (c, c), 0)
    aj = lax.broadcasted_iota(jnp.int32, (c, c), 1)
    diff = ai ^ aj
    causal_diff = jnp.where(ai > aj, diff, 0)
    level_mask = {s: (causal_diff >> (s.bit_length() - 1)) == 1 for s in halves}
    diag_mask = ai == aj

    g_hi = log_a.astype(BF16)
    g_lo = (log_a - g_hi.astype(F32)).astype(BF16)
    n_lv = len(halves) * c

    def chunk_local(h, u):
        rows, cols = slice(u * c, (u + 1) * c), slice(h * dk, (h + 1) * dk)
        level_sum = _dot(scan_ref[:n_lv, :], g_hi[rows, cols])
        cum = _dot(scan_ref[n_lv:, :], jnp.concatenate([g_hi[rows, cols], g_lo[rows, cols]], axis=1))
        prefix = cum[:, :dk] + cum[:, dk:]
        total = prefix[c - 1:c, :]
        q16 = q_ref[rows, cols] * (dk ** -0.5)
        k16 = k_ref[rows, cols]
        att = jnp.where(diag_mask, _dot_nt(q16, k16), 0.0)
        for li, s in enumerate(halves):
            decay = jnp.exp2(level_sum[li * c:(li + 1) * c, :].astype(BF16))
            att = jnp.where(level_mask[s], _dot_nt(q16 * decay, k16 * decay), att)
        q_dec = (q16.astype(F32) * jnp.exp2(prefix)).astype(BF16)
        k_dec = (k16.astype(F32) * jnp.exp2(total - prefix)).astype(BF16)
        return q_dec, k_dec, att.astype(BF16), jnp.exp2(total)

    for h in range(GLA_HEADS):
        local = [chunk_local(h, u) for u in range(nsub)]
        for u, (q_dec, k_dec, att, state_decay) in enumerate(local):
            rows = slice(u * c, (u + 1) * c)
            vh = v_ref[rows, h * dv:(h + 1) * dv]
            st = st_ref[h]
            o = _dot_nt(q_dec, st.astype(BF16)) + _dot(att, vh)
            st_ref[h] = st * state_decay + _dot_tn(vh, k_dec)
            gate = g_ref[rows, h * dv:(h + 1) * dv].astype(F32)
            o_ref[rows, h * dv:(h + 1) * dv] = (_rms(o) * on_ref[...] * (gate * jax.nn.sigmoid(gate))).astype(o_ref.dtype)


def _gla_scan_matrix(c):
    p = jnp.arange(c)[:, None]
    t = jnp.arange(c)[None, :]
    mats = []
    s = 1
    while s < c:
        start = (p // s) * s
        odd = (p // s) % 2 == 1
        mats.append(jnp.where(odd, (t >= start) & (t <= p), (t > p) & (t < start + s)))
        s *= 2
    mats.append(t <= p)
    return jnp.concatenate(mats, axis=0).astype(BF16)


def _gla(p, gk_block, gkw, gkb, onorm_g, bsz, seq, dk, dv):
    t = p.shape[0]
    c = _tile(seq, GLA_CHUNK * GLA_CHUNKS_PER_STEP)
    nc = seq // c
    hk, hv = GLA_HEADS * dk, GLA_HEADS * dv
    scan = _gla_scan_matrix(GLA_CHUNK)
    kern = functools.partial(_gla_kernel, dk=dk, dv=dv)
    row = lambda b, i: b * nc + i
    return pl.pallas_call(
        kern,
        grid=(bsz, nc),
        in_specs=[pl.BlockSpec((c, hk), lambda b, i: (row(b, i), 0)),
                  pl.BlockSpec((c, hk), lambda b, i: (row(b, i), 1)),
                  pl.BlockSpec((c, hv), lambda b, i: (row(b, i), (2 * hk) // hv)),
                  pl.BlockSpec((c, hv), lambda b, i: (row(b, i), (2 * hk) // hv + 1)),
                  pl.BlockSpec((c, V7X_LANES), lambda b, i: (row(b, i), gk_block)),
                  pl.BlockSpec(gkw.shape, lambda b, i: (0, 0)),
                  pl.BlockSpec((1, hk), lambda b, i: (0, 0)),
                  pl.BlockSpec((1, dv), lambda b, i: (0, 0)),
                  pl.BlockSpec(scan.shape, lambda b, i: (0, 0))],
        out_specs=pl.BlockSpec((c, hv), lambda b, i: (row(b, i), 0)),
        out_shape=jax.ShapeDtypeStruct((t, hv), BF16),
        scratch_shapes=[pltpu.VMEM((GLA_HEADS, dv, dk), F32)],
        compiler_params=_params("parallel", "arbitrary"),
        name="gla",
    )(p, p, p, p, p, gkw, gkb.reshape(1, hk), onorm_g.reshape(1, dv), scan)


def _flash_kernel(qt_ref, k_ref, vt_ref, o_ref, acc_ref, sa_ref, sb_ref, pa_ref, pb_ref):
    strip, cw = MLA_STRIP, MLA_CHAIN
    hp, nch = acc_ref.shape[0], acc_ref.shape[1]
    chains = [(h, a) for h in range(hp) for a in range(nch)]
    per_block = MLA_QBLOCK // strip
    assert per_block == 2 and nch == 2 and strip == cw
    i = pl.program_id(2)
    acc_ref[...] = jnp.zeros_like(acc_ref)
    pb_ref[...] = jnp.zeros_like(pb_ref)

    def scores(h, a, strip_idx):
        start = pl.multiple_of(strip_idx * strip, strip)
        return _dot(k_ref[0, pl.ds(start, strip), h * MLA_QW:(h + 1) * MLA_QW],
                    qt_ref[h * MLA_QW:(h + 1) * MLA_QW, a * cw:(a + 1) * cw])

    def col_max(s):
        return jnp.max(s, axis=0, keepdims=True)

    def softmax(s, s_max, m):
        m_new = jnp.maximum(m, s_max)
        return jnp.exp2((s - m_new).astype(BF16)), jnp.exp2(m - m_new), m_new

    def accumulate(h, a, strip_idx, p, alpha):
        acc_ref[h, a] = alpha * acc_ref[h, a] + _dot(vt_ref[h, strip_idx], p)

    def half_step(s_cur, s_nxt, p_cur, p_prev, t, carry):
        out = []
        for c, (h, a) in enumerate(chains):
            m, cur_max, alpha_prev = carry[c]
            s = scores(h, a, t + 1)
            s_nxt[h, a] = s
            accumulate(h, a, jnp.maximum(t - 1, 0), p_prev[h, a], alpha_prev)
            p, alpha, m = softmax(s_cur[h, a], cur_max, m)
            p_cur[h, a] = p
            out.append((m, col_max(s), alpha))
        return tuple(out)

    def body(jj, carry):
        carry = half_step(sa_ref, sb_ref, pa_ref, pb_ref, 2 * jj, carry)
        return half_step(sb_ref, sa_ref, pb_ref, pa_ref, 2 * jj + 1, carry)

    init = []
    for h, a in chains:
        s = scores(h, a, 0)
        sa_ref[h, a] = s
        init.append((jnp.full((1, cw), MASK_VALUE, F32), col_max(s), jnp.ones((1, cw), F32)))
    carry = lax.fori_loop(0, i, body, tuple(init))
    first = per_block * i
    causal = lax.broadcasted_iota(jnp.int32, (strip, cw), 0) <= lax.broadcasted_iota(jnp.int32, (strip, cw), 1)
    last = [scores(h, 1, first + 1) for h in range(hp)]

    def update(h, a, strip_idx, s, s_max, m):
        p, alpha, m = softmax(s, s_max, m)
        accumulate(h, a, strip_idx, p, alpha)
        return m

    for c, (h, a) in enumerate(chains):
        m, s_max, alpha_prev = carry[c]
        accumulate(h, a, jnp.maximum(first - 1, 0), pb_ref[h, a], alpha_prev)
        if a == 0:
            s = jnp.where(causal, sa_ref[h, a], MASK_VALUE)
            update(h, a, first, s, col_max(s), m)
        else:
            m = update(h, a, first, sa_ref[h, a], s_max, m)
            s = jnp.where(causal, last[h], MASK_VALUE)
            update(h, a, first + 1, s, col_max(s), m)
        o_t = acc_ref[h, a, :MLA_V, :] / acc_ref[h, a, MLA_V:MLA_V + 1, :]
        o_ref[a * cw:(a + 1) * cw, h * MLA_V:(h + 1) * MLA_V] = o_t.T.astype(o_ref.dtype)


def _flash(qt, k, vt, bsz, seq):
    t = k.shape[1]
    assert seq % MLA_QBLOCK == 0
    nq = seq // MLA_QBLOCK
    hp = MLA_HEAD_GROUP
    nch = MLA_QBLOCK // MLA_CHAIN
    return pl.pallas_call(
        _flash_kernel,
        grid=(bsz, MLA_HEADS // hp, nq),
        in_specs=[pl.BlockSpec((hp * MLA_QW, MLA_QBLOCK), lambda b, g, i: (g, b * nq + i)),
                  pl.BlockSpec((1, seq, hp * MLA_QW), lambda b, g, i: (g, b, 0)),
                  pl.BlockSpec((hp, seq // MLA_STRIP, MLA_VP, MLA_STRIP), lambda b, g, i: (g, b, 0, 0))],
        out_specs=pl.BlockSpec((MLA_QBLOCK, hp * MLA_V), lambda b, g, i: (b * nq + i, g)),
        out_shape=jax.ShapeDtypeStruct((t, MLA_HEADS * MLA_V), BF16),
        scratch_shapes=[pltpu.VMEM((hp, nch, MLA_VP, MLA_CHAIN), F32),
                        pltpu.VMEM((hp, nch, MLA_STRIP, MLA_CHAIN), F32),
                        pltpu.VMEM((hp, nch, MLA_STRIP, MLA_CHAIN), F32),
                        pltpu.VMEM((hp, nch, MLA_STRIP, MLA_CHAIN), BF16),
                        pltpu.VMEM((hp, nch, MLA_STRIP, MLA_CHAIN), BF16)],
        compiler_params=_params("parallel", "parallel", "arbitrary"),
        name="flash",
    )(qt, k, vt)


def _merge_kernel(og_ref, om_ref, ga_ref, gb_ref, wg_ref, wm_ref, o_ref):
    yg = _dot(og_ref[...], wg_ref[...])
    ym = _dot(om_ref[...], wm_ref[...])
    ga = jax.nn.sigmoid(ga_ref[...].astype(F32))
    gb = jax.nn.sigmoid(gb_ref[...].astype(F32))
    o_ref[...] = (ga * yg + gb * ym).astype(o_ref.dtype)


def _outproj_kernel(m_ref, x_ref, mod_ref, gf_ref, wo_ref, o_ref, h_ref):
    x1 = x_ref[...] + mod_ref[0, MOD_GATE_MIX:MOD_GATE_MIX + 1, :] * _dot(m_ref[...], wo_ref[...])
    o_ref[...] = x1
    h_ref[...] = _modnorm(x1, gf_ref[...], mod_ref[0, MOD_SCALE_FFN:MOD_SCALE_FFN + 1, :],
                          mod_ref[0, MOD_SHIFT_FFN:MOD_SHIFT_FFN + 1, :]).astype(BF16)


def _mixout(o_gla, o_mla, p, gate_col, x2, mod3, ffn_gain, w_gla, w_mla, w_out, seq):
    t, d = x2.shape
    tm = _tile(seq, MIXOUT_ROW_TILE)
    assert gate_col % d == 0
    row = lambda i: (i, 0)
    resident = lambda w: pl.BlockSpec(w.shape, lambda i: (0, 0), pipeline_mode=pl.Buffered(1))
    merged = pl.pallas_call(
        _merge_kernel,
        grid=(t // tm,),
        in_specs=[pl.BlockSpec((tm, o_gla.shape[1]), row),
                  pl.BlockSpec((tm, o_mla.shape[1]), row),
                  pl.BlockSpec((tm, d), lambda i: (i, gate_col // d)),
                  pl.BlockSpec((tm, d), lambda i: (i, gate_col // d + 1)),
                  resident(w_gla), resident(w_mla)],
        out_specs=pl.BlockSpec((tm, d), row),
        out_shape=jax.ShapeDtypeStruct((t, d), BF16),
        compiler_params=_params("parallel"),
        name="merge",
    )(o_gla, o_mla, p, p, w_gla, w_mla)
    return pl.pallas_call(
        _outproj_kernel,
        grid=(t // tm,),
        in_specs=[pl.BlockSpec((tm, d), row),
                  pl.BlockSpec((tm, d), row),
                  pl.BlockSpec((1, N_MOD, d), lambda i: ((i * tm) // seq, 0, 0)),
                  pl.BlockSpec((1, d), lambda i: (0, 0)),
                  resident(w_out)],
        out_specs=[pl.BlockSpec((tm, d), row), pl.BlockSpec((tm, d), row)],
        out_shape=[jax.ShapeDtypeStruct((t, d), F32), jax.ShapeDtypeStruct((t, d), BF16)],
        compiler_params=_params("parallel"),
        name="outproj",
    )(merged, x2, mod3, ffn_gain.reshape(1, d), w_out)


def _ffn_up_kernel(h_ref, wg_ref, wu_ref, o_ref):
    h = h_ref[...]
    gate = _dot(h, wg_ref[...])
    up = _dot(h, wu_ref[...])
    o_ref[...] = (gate * jax.nn.sigmoid(gate) * up).astype(o_ref.dtype)


def _ffn_down_kernel(a_ref, x_ref, mod_ref, wd_ref, fg_ref, o_ref, *, final_norm):
    y = x_ref[...] + mod_ref[0, MOD_GATE_FFN:MOD_GATE_FFN + 1, :] * _dot(a_ref[...], wd_ref[...])
    if final_norm:
        y = _rms(y) * fg_ref[...]
    o_ref[...] = y


def _ffn(x1, h, mod3, w_in, w_down, final_g, seq, final_norm):
    t, d = x1.shape
    hidden = w_down.shape[0]
    tm = _tile(seq, FFN_UP_ROW_TILE)
    tf = _tile(hidden, FFN_HIDDEN_BLOCK)
    nf = hidden // tf
    act = pl.pallas_call(
        _ffn_up_kernel,
        grid=(t // tm, nf),
        in_specs=[pl.BlockSpec((tm, d), lambda i, f: (i, 0)),
                  pl.BlockSpec((d, tf), lambda i, f: (0, f)),
                  pl.BlockSpec((d, tf), lambda i, f: (0, nf + f))],
        out_specs=pl.BlockSpec((tm, tf), lambda i, f: (i, f)),
        out_shape=jax.ShapeDtypeStruct((t, hidden), BF16),
        compiler_params=_params("parallel", "arbitrary"),
        name="ffn_up",
    )(h, w_in, w_in)
    tr = _tile(seq, FFN_DOWN_ROW_TILE)
    return pl.pallas_call(
        functools.partial(_ffn_down_kernel, final_norm=final_norm),
        grid=(t // tr,),
        in_specs=[pl.BlockSpec((tr, hidden), lambda i: (i, 0)),
                  pl.BlockSpec((tr, d), lambda i: (i, 0)),
                  pl.BlockSpec((1, N_MOD, d), lambda i: ((i * tr) // seq, 0, 0)),
                  pl.BlockSpec(w_down.shape, lambda i: (0, 0), pipeline_mode=pl.Buffered(1)),
                  pl.BlockSpec((1, d), lambda i: (0, 0))],
        out_specs=pl.BlockSpec((tr, d), lambda i: (i, 0)),
        out_shape=jax.ShapeDtypeStruct((t, d), F32),
        compiler_params=_params("parallel"),
        name="ffn_down",
    )(act, x1, mod3, w_down, final_g.reshape(1, d))


def _pack_w_in_kernel(wt_ref, o_ref, *, moves, kr_row, gk_row, tail_col):
    for dst, src, width in moves:
        o_ref[:, dst:dst + width] = wt_ref[src:src + width, :].T.astype(BF16)
    half = MLA_ROPE // 2
    kr = wt_ref[kr_row:kr_row + MLA_ROPE, :]
    gk = wt_ref[gk_row:gk_row + GLA_GATE_RANK, :]
    pad = jnp.zeros((V7X_LANES - GLA_GATE_RANK, kr.shape[1]), F32)
    o_ref[:, tail_col:] = jnp.concatenate([kr, kr[half:], kr[:half], gk, pad], axis=0).T.astype(BF16)


def _pack_w_in(w, d, dk_all, dv_all, q_rank, kv_rank):
    widths = (dk_all, dk_all, dv_all, dv_all, GLA_GATE_RANK, q_rank, kv_rank, MLA_ROPE, d, d)
    offs = [0]
    for wd in widths:
        offs.append(offs[-1] + wd)
    gla_w = offs[4]
    moves = ((0, 0, gla_w), (gla_w, offs[8], 2 * d), (gla_w + 2 * d, offs[5], q_rank + kv_rank))
    tail_col = gla_w + 2 * d + q_rank + kv_rank
    n_out = tail_col + 2 * V7X_LANES
    assert all(dst % V7X_LANES == 0 for dst, _, _ in moves) and tail_col % V7X_LANES == 0
    rows = w.shape[0]
    tr = _tile(rows, PACK_ROW_TILE)
    kern = functools.partial(_pack_w_in_kernel, moves=moves, kr_row=offs[7], gk_row=offs[4], tail_col=tail_col)
    return pl.pallas_call(
        kern,
        grid=(rows // tr,),
        in_specs=[pl.BlockSpec((w.shape[1], tr), lambda i: (0, i))],
        out_specs=pl.BlockSpec((tr, n_out), lambda i: (i, 0)),
        out_shape=jax.ShapeDtypeStruct((rows, n_out), BF16),
        compiler_params=_params("parallel"),
        name="pack_w_in",
    )(w.T)


def _pack_wuq_t(w):
    r = w.shape[0]
    half = MLA_ROPE // 2
    w3 = w.reshape(r, MLA_HEADS, MLA_NOPE + MLA_ROPE)
    rope = w3[:, :, MLA_NOPE:]
    rot = jnp.concatenate([rope[:, :, half:], rope[:, :, :half]], axis=2)
    return jnp.concatenate([w3[:, :, :MLA_NOPE], rope, rot], axis=2).reshape(r, -1).T.astype(BF16)


def _pack_wukv(w):
    r = w.shape[0]
    w3 = w.reshape(r, MLA_HEADS, MLA_NOPE + MLA_V)
    wk = w3[:, :, :MLA_NOPE].reshape(r, -1).astype(BF16)
    wvt = w3[:, :, MLA_NOPE:].reshape(r, -1).T.astype(BF16)
    return wk, wvt


def kernel(x, c, positions, ada_w, ada_b, norm_mix_g, w_in, gla_gk_w, gla_gk_b, gla_onorm_g, gla_wo, mla_q_norm_g, mla_wuq, mla_kv_norm_g, mla_wukv, mla_wo, w_out, norm_ffn_g, ffn_w_in, ffn_w_down, final_norm_g):
    bsz, seq, d = x.shape
    depth = w_in.shape[0]
    dk_all = gla_gk_w.shape[2]
    dv_all = gla_wo.shape[1]
    dk, dv = dk_all // GLA_HEADS, dv_all // GLA_HEADS
    q_rank, kv_rank = mla_wuq.shape[1], mla_wukv.shape[1]
    lat_col = 2 * dk_all + 2 * dv_all + 2 * d
    lat_width = q_rank + kv_rank + 2 * V7X_LANES
    gate_col = 2 * dk_all + 2 * dv_all
    assert lat_col % lat_width == 0 and (lat_col + lat_width - V7X_LANES) % V7X_LANES == 0

    x2 = x.reshape(bsz * seq, d)
    cos_t, sin_t = _rope_tab(positions)
    for l in range(depth):
        mod3 = _ada(c, ada_w[l], ada_b[l]).reshape(bsz, N_MOD, d)
        w_all = _pack_w_in(w_in[l], d, dk_all, dv_all, q_rank, kv_rank)
        p = _inproj(x2, mod3, norm_mix_g[l], w_all, seq)
        wk, wvt = _pack_wukv(mla_wukv[l])
        qt, k, vt = _mlaproj(p, lat_col // lat_width, lat_width, cos_t, sin_t, mla_q_norm_g[l], mla_kv_norm_g[l],
                             _pack_wuq_t(mla_wuq[l]), wk, wvt)
        gkw = jnp.zeros((V7X_LANES, dk_all), BF16).at[:GLA_GATE_RANK].set(gla_gk_w[l].astype(BF16))
        o_gla = _gla(p, (lat_col + lat_width - V7X_LANES) // V7X_LANES, gkw, gla_gk_b[l], gla_onorm_g[l],
                     bsz, seq, dk, dv)
        o_mla = _flash(qt, k, vt, bsz, seq)
        x1, h_ffn = _mixout(o_gla, o_mla, p, gate_col, x2, mod3, norm_ffn_g[l], gla_wo[l].astype(BF16),
                            mla_wo[l].astype(BF16), w_out[l].astype(BF16), seq)
        x2 = _ffn(x1, h_ffn, mod3, ffn_w_in[l].astype(BF16), ffn_w_down[l].astype(BF16), final_norm_g,
                  seq, final_norm=(l == depth - 1))
    return x2.reshape(bsz, seq, d)
```
